```python
import jax, jax.numpy as jnp
from jax import lax
import numpy as np

D_MODEL = 4096
BATCH = 2
SEQ = 8192
DEPTH = 2

GRID_W = 64
CTX_LEN = 256
HEAD_DIM = 128
ROPE_THETA = 10000.0
NORM_EPS = 1e-6
Q_BLOCK = 128
N_MOD = 6
N_BRANCH = 4
BRANCH_WIDTH = D_MODEL // N_BRANCH
MLA_HEADS = BRANCH_WIDTH // HEAD_DIM
MLA_Q_LORA = 1024
MLA_KV_LORA = 512
MLA_NOPE = 128
MLA_ROPE = 64
MLA_V = HEAD_DIM
GQA_HEADS = BRANCH_WIDTH // HEAD_DIM
GQA_KV_HEADS = 2
NA_HEADS = BRANCH_WIDTH // HEAD_DIM
NA_WIN_H = 8
NA_WIN_W = 16
LRU_WIDTH = BRANCH_WIDTH
LRU_BLOCKS = 8
LRU_CONV = 4
LRU_C = 8.0
N_EXPERTS = 16
N_GROUPS = 4
TOP_K = 2
D_EXPERT = 1024

ATTN_SCALE = HEAD_DIM ** -0.5
MLA_SCALE = (MLA_NOPE + MLA_ROPE) ** -0.5
IN_SPLITS = (MLA_Q_LORA, MLA_KV_LORA, MLA_ROPE,
             GQA_HEADS * HEAD_DIM, GQA_KV_HEADS * HEAD_DIM, GQA_KV_HEADS * HEAD_DIM,
             NA_HEADS * HEAD_DIM, NA_HEADS * HEAD_DIM, NA_HEADS * HEAD_DIM,
             LRU_WIDTH, LRU_WIDTH)
D_IN = sum(IN_SPLITS)

kernel_name = 'hybrid_dit_mla_gqa_natten_rglru_moe'

F32 = jnp.float32


def rmsnorm(x, g):
    xf = x.astype(F32)
    y = xf * lax.rsqrt(jnp.mean(xf * xf, axis=-1, keepdims=True) + NORM_EPS)
    return (y * g.astype(F32)).astype(x.dtype)


def modulate(x, g, shift, scale):
    return rmsnorm(x, g) * (1.0 + scale) + shift


def heads(x, n):
    return x.reshape(x.shape[:-1] + (n, HEAD_DIM))


def axial_rope_tables(n_tokens, rot_dim):
    t = jnp.arange(n_tokens, dtype=jnp.int32)
    row = (t // GRID_W).astype(F32)
    col = (t % GRID_W).astype(F32)
    n_freq = rot_dim // 4
    freqs = ROPE_THETA ** (-jnp.arange(n_freq, dtype=F32) / n_freq)
    ang = jnp.concatenate([row[:, None] * freqs, col[:, None] * freqs], axis=-1)
    return jnp.cos(ang), jnp.sin(ang)


def apply_rope(x, cos, sin):
    half = x.shape[-1] // 2
    x1 = x[..., :half].astype(F32)
    x2 = x[..., half:].astype(F32)
    cs, sn = cos[:, None, :], sin[:, None, :]
    return jnp.concatenate([x1 * cs - x2 * sn, x1 * sn + x2 * cs], axis=-1).astype(x.dtype)


def blocked_attention(q, k, v, scale):
    B, S, H, Dk = q.shape
    Hk = k.shape[2]
    G = H // Hk
    nb = S // Q_BLOCK
    qb = q.reshape(B, nb, Q_BLOCK, Hk, G, Dk).transpose(1, 0, 2, 3, 4, 5)

    def one_block(qblk):
        s = jnp.einsum('bqkgd,btkd->bkgqt', qblk, k).astype(F32) * scale
        p = jax.nn.softmax(s, axis=-1).astype(v.dtype)
        return jnp.einsum('bkgqt,btkd->bqkgd', p, v)

    out = lax.map(one_block, qb)
    return out.transpose(1, 0, 2, 3, 4, 5).reshape(B, S, H * v.shape[-1])


def mla_qkv(cq, ckv, kr, q_norm, w_uq, kv_norm, w_ukv, rope):
    B, T, _ = cq.shape
    q = (rmsnorm(cq, q_norm) @ w_uq).reshape(B, T, MLA_HEADS, MLA_NOPE + MLA_ROPE)
    kv = (rmsnorm(ckv, kv_norm) @ w_ukv).reshape(B, T, MLA_HEADS, MLA_NOPE + MLA_V)
    q_nope, q_rope = q[..., :MLA_NOPE], q[..., MLA_NOPE:]
    k_nope, v = kv[..., :MLA_NOPE], kv[..., MLA_NOPE:]
    kr = kr[:, :, None, :]
    if rope is not None:
        q_rope = apply_rope(q_rope, rope[0], rope[1])
        kr = apply_rope(kr, rope[0], rope[1])
    q = jnp.concatenate([q_nope, q_rope], axis=-1)
    k = jnp.concatenate([k_nope, jnp.broadcast_to(kr, (B, T, MLA_HEADS, MLA_ROPE))], axis=-1)
    return q, k, v


def neighbourhood_attention(q, k, v, k_ctx, v_ctx, rpb):
    B, S, H, Dh = q.shape
    rows = S // GRID_W
    kh = min(NA_WIN_H, rows)
    kw = NA_WIN_W
    qg = q.reshape(B, rows, GRID_W, H, Dh).transpose(1, 0, 2, 3, 4)
    kg = k.reshape(B, rows, GRID_W, H, Dh)
    vg = v.reshape(B, rows, GRID_W, H, Dh)
    row_ids = jnp.arange(rows, dtype=jnp.int32)
    row_start = jnp.clip(row_ids - kh // 2, 0, rows - kh)
    col_ids = np.arange(GRID_W)
    col_start = np.clip(col_ids - kw // 2, 0, GRID_W - kw)
    col_idx = col_start[:, None] + np.arange(kw)[None, :]
    dc_idx = col_idx - col_ids[:, None] + (NA_WIN_W - 1)
    rpb_cols = rpb[:, :, dc_idx]

    def one_row(args):
        qb, r, rs = args
        kb = lax.dynamic_slice_in_dim(kg, rs, kh, axis=1)[:, :, col_idx]
        vb = lax.dynamic_slice_in_dim(vg, rs, kh, axis=1)[:, :, col_idx]
        dr_idx = rs + jnp.arange(kh, dtype=jnp.int32) - r + (NA_WIN_H - 1)
        bias = rpb_cols[:, dr_idx].transpose(0, 2, 1, 3)
        s_loc = jnp.einsum('bqhd,biqjhd->bhqij', qb, kb).astype(F32) * ATTN_SCALE + bias.astype(F32)
        s_ctx = jnp.einsum('bqhd,bthd->bhqt', qb, k_ctx).astype(F32) * ATTN_SCALE
        s = jnp.concatenate([s_loc.reshape(B, H, GRID_W, kh * kw), s_ctx], axis=-1)
        p = jax.nn.softmax(s, axis=-1).astype(v.dtype)
        p_loc = p[..., :kh * kw].reshape(B, H, GRID_W, kh, kw)
        p_ctx = p[..., kh * kw:]
        return (jnp.einsum('bhqij,biqjhd->bqhd', p_loc, vb)
                + jnp.einsum('bhqt,bthd->bqhd', p_ctx, v_ctx))

    out = lax.map(one_row, (qg, row_ids, row_start))
    return out.transpose(1, 0, 2, 3, 4).reshape(B, S, H * Dh)


def centred_dwconv(x, w, b):
    y = lax.conv_general_dilated(
        x, w[:, None, :], window_strides=(1,),
        padding=[(LRU_CONV // 2, LRU_CONV - 1 - LRU_CONV // 2)],
        dimension_numbers=('NWC', 'WIO', 'NWC'), feature_group_count=x.shape[-1])
    return y + b


def _lin_combine(e1, e2):
    a1, b1 = e1
    a2, b2 = e2
    return a1 * a2, a2 * b1 + b2


def rglru_scan(xc, w_a, b_a, w_x, b_x, lam, h0, reverse):
    B, T, W = xc.shape
    xb = xc.reshape(B, T, LRU_BLOCKS, W // LRU_BLOCKS)
    r = jax.nn.sigmoid((jnp.einsum('btnc,ncd->btnd', xb, w_a).reshape(B, T, W) + b_a).astype(F32))
    i = jax.nn.sigmoid((jnp.einsum('btnc,ncd->btnd', xb, w_x).reshape(B, T, W) + b_x).astype(F32))
    log_a = -LRU_C * r * jax.nn.softplus(-lam.astype(F32))
    a = jnp.exp(log_a)
    bterm = jnp.sqrt(-jnp.expm1(2.0 * log_a)) * (i * xc.astype(F32))
    edge = T - 1 if reverse else 0
    bterm = bterm.at[:, edge].add(a[:, edge] * h0)
    _, h = lax.associative_scan(_lin_combine, (a, bterm), axis=1, reverse=reverse)
    return h


def rglru_branch(x_lat, g_lat, x_ctx, g_ctx, conv_w, conv_b, w_a, b_a, w_x, b_x, lam, need_ctx):
    xl = centred_dwconv(x_lat, conv_w, conv_b)
    xc = centred_dwconv(x_ctx, conv_w, conv_b)
    B = x_ctx.shape[0]
    h0 = jnp.zeros((B, LRU_WIDTH), F32)
    hc_f = rglru_scan(xc, w_a[0], b_a[0], w_x[0], b_x[0], lam[0], h0, False)
    hl_f = rglru_scan(xl, w_a[0], b_a[0], w_x[0], b_x[0], lam[0], hc_f[:, -1], False)
    hc_b = rglru_scan(xc, w_a[1], b_a[1], w_x[1], b_x[1], lam[1], h0, True)
    hl_b = rglru_scan(xl, w_a[1], b_a[1], w_x[1], b_x[1], lam[1], hc_b[:, 0], True)
    y_lat = ((hl_f + hl_b) * jax.nn.gelu(g_lat.astype(F32))).astype(x_lat.dtype)
    y_ctx = ((hc_f + hc_b) * jax.nn.gelu(g_ctx.astype(F32))).astype(x_ctx.dtype) if need_ctx else None
    return y_lat, y_ctx


def merge_branches(h, ys, w_branch_gate, b_branch_gate, w_branch, w_out):
    merged = None
    for i, y in enumerate(ys):
        g = jax.nn.sigmoid(h @ w_branch_gate[i] + b_branch_gate[i])
        term = g * (y @ w_branch[i])
        merged = term if merged is None else merged + term
    return merged @ w_out


def split_in(p):
    offs = np.cumsum(np.array(IN_SPLITS))[:-1].tolist()
    return jnp.split(p, offs, axis=-1)


def token_mixers(h_lat, h_ctx, need_ctx, w_in, mla_q_norm, mla_w_uq, mla_kv_norm, mla_w_ukv,
                 gqa_q_norm, gqa_k_norm, na_rpb, lru_conv_w, lru_conv_b, lru_w_a, lru_b_a,
                 lru_w_x, lru_b_x, lru_lambda, w_branch_gate, b_branch_gate, w_branch, w_out,
                 rope128, rope64):
    B, S, _ = h_lat.shape
    pl = split_in(h_lat @ w_in)
    pc = split_in(h_ctx @ w_in)

    qa, ka, va = mla_qkv(pl[0], pl[1], pl[2], mla_q_norm, mla_w_uq, mla_kv_norm, mla_w_ukv, rope64)
    qac, kac, vac = mla_qkv(pc[0], pc[1], pc[2], mla_q_norm, mla_w_uq, mla_kv_norm, mla_w_ukv, None)
    ya = blocked_attention(qa, jnp.concatenate([kac, ka], 1), jnp.concatenate([vac, va], 1), MLA_SCALE)

    qb = apply_rope(rmsnorm(heads(pl[3], GQA_HEADS), gqa_q_norm), rope128[0], rope128[1])
    kb = apply_rope(rmsnorm(heads(pl[4], GQA_KV_HEADS), gqa_k_norm), rope128[0], rope128[1])
    vb = heads(pl[5], GQA_KV_HEADS)
    qbc = rmsnorm(heads(pc[3], GQA_HEADS), gqa_q_norm)
    kbc = rmsnorm(heads(pc[4], GQA_KV_HEADS), gqa_k_norm)
    vbc = heads(pc[5], GQA_KV_HEADS)
    yb = blocked_attention(qb, jnp.concatenate([kbc, kb], 1), jnp.concatenate([vbc, vb], 1), ATTN_SCALE)

    kcc, vcc = heads(pc[7], NA_HEADS), heads(pc[8], NA_HEADS)
    yc = neighbourhood_attention(heads(pl[6], NA_HEADS), heads(pl[7], NA_HEADS), heads(pl[8], NA_HEADS),
                                 kcc, vcc, na_rpb)

    yd, ydc = rglru_branch(pl[9], pl[10], pc[9], pc[10], lru_conv_w, lru_conv_b, lru_w_a, lru_b_a,
                           lru_w_x, lru_b_x, lru_lambda, need_ctx)

    y_lat = merge_branches(h_lat, (ya, yb, yc, yd), w_branch_gate, b_branch_gate, w_branch, w_out)
    if not need_ctx:
        return y_lat, None
    yac = blocked_attention(qac, kac, vac, MLA_SCALE)
    ybc = blocked_attention(qbc, kbc, vbc, ATTN_SCALE)
    ycc = blocked_attention(heads(pc[6], NA_HEADS), kcc, vcc, ATTN_SCALE)
    y_ctx = merge_branches(h_ctx, (yac, ybc, ycc, ydc), w_branch_gate, b_branch_gate, w_branch, w_out)
    return y_lat, y_ctx


def moe_ffn(h, w_router, router_bias, w_gate, w_up, w_down):
    scores = jax.nn.sigmoid((h @ w_router).astype(F32))
    sel = scores + router_bias.astype(F32)
    per = N_EXPERTS // N_GROUPS
    group_score = lax.top_k(sel.reshape(-1, N_GROUPS, per), 2)[0].sum(-1)
    best = jnp.argmax(group_score, axis=-1)
    expert_group = jnp.arange(N_EXPERTS, dtype=jnp.int32) // per
    masked = jnp.where(expert_group[None, :] == best[:, None], sel, -jnp.inf)
    _, idx = lax.top_k(masked, TOP_K)
    wts = jnp.take_along_axis(scores, idx, axis=-1)
    wts = wts / jnp.sum(wts, axis=-1, keepdims=True)
    gates = jnp.einsum('nk,nke->ne', wts, jax.nn.one_hot(idx, N_EXPERTS, dtype=F32)).astype(h.dtype)
    y = None
    for e in range(N_EXPERTS):
        he = jax.nn.silu(h @ w_gate[e]) * (h @ w_up[e])
        term = gates[:, e:e + 1] * (he @ w_down[e])
        y = term if y is None else y + term
    return y


def setup_inputs(seed: int = 0) -> dict:
    key = jax.random.key(seed)
    ks = jax.random.split(key, 48)
    D, L = D_MODEL, DEPTH

    def nrm(i, shape, scale):
        return jax.random.normal(ks[i], shape, F32) * scale

    def gain(i, shape):
        return 1.0 + 0.02 * jax.random.normal(ks[i], shape, F32)

    a0 = jax.random.uniform(ks[40], (L, 2, LRU_WIDTH), F32, 0.9, 0.999) ** (1.0 / LRU_C)
    lam = jnp.log(a0) - jnp.log1p(-a0)
    bw = LRU_WIDTH // LRU_BLOCKS
    return {
        'x': nrm(0, (BATCH, SEQ, D), 1.0),
        'c': nrm(1, (BATCH, D), 1.0),
        'ctx': nrm(2, (BATCH, CTX_LEN, D), 1.0),
        'c_ctx': nrm(3, (D,), 1.0),
        'w_mod': nrm(4, (L, D, N_MOD * D), 0.5 * D ** -0.5),
        'b_mod': nrm(5, (L, N_MOD * D), 0.02),
        'norm_mix': gain(6, (L, D)),
        'norm_ffn': gain(7, (L, D)),
        'w_in': nrm(8, (L, D, D_IN), D ** -0.5),
        'mla_q_norm': gain(9, (L, MLA_Q_LORA)),
        'mla_w_uq': nrm(10, (L, MLA_Q_LORA, MLA_HEADS * (MLA_NOPE + MLA_ROPE)), MLA_Q_LORA ** -0.5),
        'mla_kv_norm': gain(11, (L, MLA_KV_LORA)),
        'mla_w_ukv': nrm(12, (L, MLA_KV_LORA, MLA_HEADS * (MLA_NOPE + MLA_V)), MLA_KV_LORA ** -0.5),
        'gqa_q_norm': gain(13, (L, HEAD_DIM)),
        'gqa_k_norm': gain(14, (L, HEAD_DIM)),
        'na_rpb': nrm(15, (L, NA_HEADS, 2 * NA_WIN_H - 1, 2 * NA_WIN_W - 1), 0.1),
        'lru_conv_w': nrm(16, (L, LRU_CONV, LRU_WIDTH), 0.5),
        'lru_conv_b': nrm(17, (L, LRU_WIDTH), 0.02),
        'lru_w_a': nrm(18, (L, 2, LRU_BLOCKS, bw, bw), bw ** -0.5),
        'lru_b_a': nrm(19, (L, 2, LRU_WIDTH), 0.02),
        'lru_w_x': nrm(20, (L, 2, LRU_BLOCKS, bw, bw), bw ** -0.5),
        'lru_b_x': nrm(21, (L, 2, LRU_WIDTH), 0.02),
        'lru_lambda': lam,
        'w_branch_gate': nrm(22, (L, N_BRANCH, D, D), D ** -0.5),
        'b_branch_gate': nrm(23, (L, N_BRANCH, D), 0.02),
        'w_branch': nrm(24, (L, N_BRANCH, BRANCH_WIDTH, D), BRANCH_WIDTH ** -0.5),
        'w_out': nrm(25, (L, D, D), D ** -0.5),
        'w_router': nrm(26, (D, N_EXPERTS), D ** -0.5),
        'router_bias': nrm(27, (N_EXPERTS,), 0.01),
        'w_exp_gate': nrm(28, (L, N_EXPERTS, D, D_EXPERT), D ** -0.5),
        'w_exp_up': nrm(29, (L, N_EXPERTS, D, D_EXPERT), D ** -0.5),
        'w_exp_down': nrm(30, (L, N_EXPERTS, D_EXPERT, D), D_EXPERT ** -0.5),
        'final_norm': gain(31, (D,)),
    }


def reference(x, c, ctx, c_ctx, w_mod, b_mod, norm_mix, norm_ffn, w_in, mla_q_norm, mla_w_uq,
              mla_kv_norm, mla_w_ukv, gqa_q_norm, gqa_k_norm, na_rpb, lru_conv_w, lru_conv_b,
              lru_w_a, lru_b_a, lru_w_x, lru_b_x, lru_lambda, w_branch_gate, b_branch_gate,
              w_branch, w_out, w_router, router_bias, w_exp_gate, w_exp_up, w_exp_down, final_norm):
    B, S, D = x.shape
    rope128 = axial_rope_tables(S, HEAD_DIM)
    rope64 = axial_rope_tables(S, MLA_ROPE)
    silu_c = jax.nn.silu(c)
    silu_cc = jax.nn.silu(c_ctx)
    for layer in range(DEPTH):
        need_ctx = layer < DEPTH - 1
        mod = (silu_c @ w_mod[layer] + b_mod[layer]).reshape(B, N_MOD, 1, D)
        mod_c = (silu_cc @ w_mod[layer] + b_mod[layer]).reshape(N_MOD, D)
        h = modulate(x, norm_mix[layer], mod[:, 0], mod[:, 1])
        hc = modulate(ctx, norm_mix[layer], mod_c[0], mod_c[1])
        y, yc = token_mixers(h, hc, need_ctx, w_in[layer], mla_q_norm[layer], mla_w_uq[layer],
                             mla_kv_norm[layer], mla_w_ukv[layer], gqa_q_norm[layer], gqa_k_norm[layer],
                             na_rpb[layer], lru_conv_w[layer], lru_conv_b[layer], lru_w_a[layer],
                             lru_b_a[layer], lru_w_x[layer], lru_b_x[layer], lru_lambda[layer],
                             w_branch_gate[layer], b_branch_gate[layer], w_branch[layer], w_out[layer],
                             rope128, rope64)
        x = x + mod[:, 2] * y
        h2 = modulate(x, norm_ffn[layer], mod[:, 3], mod[:, 4]).reshape(B * S, D)
        if need_ctx:
            ctx = ctx + mod_c[2] * yc
            h2c = modulate(ctx, norm_ffn[layer], mod_c[3], mod_c[4]).reshape(-1, D)
            f = moe_ffn(jnp.concatenate([h2, h2c], axis=0), w_router, router_bias,
                        w_exp_gate[layer], w_exp_up[layer], w_exp_down[layer])
            x = x + mod[:, 5] * f[:B * S].reshape(B, S, D)
            ctx = ctx + mod_c[5] * f[B * S:].reshape(ctx.shape)
        else:
            f = moe_ffn(h2, w_router, router_bias, w_exp_gate[layer], w_exp_up[layer], w_exp_down[layer])
            x = x + mod[:, 5] * f.reshape(B, S, D)
    return rmsnorm(x, final_norm)
```

```python
import functools
import math

import numpy as np
import jax
import jax.numpy as jnp
from jax import lax
from jax.experimental import pallas as pl
from jax.experimental.pallas import tpu as pltpu

F32 = jnp.float32
BF16 = jnp.bfloat16
I32 = jnp.int32

HEAD_DIM = 128
GRID_W = 64
ROPE_THETA = 10000.0
NORM_EPS = 1e-6
MLA_NOPE = 128
MLA_ROPE = 64
NA_WIN_H = 8
NA_WIN_W = 16
LRU_CONV = 4
LRU_C = 8.0
N_GROUPS = 4
N_MOD = 6

LANES = 128
SUBLANES = 8
MOD_ROWS = 16
VMEM_LIMIT_BYTES = 58 * 1024 * 1024
ROW_BLOCK = 256
NEG_BIG = -1e30


def _pick(n, prefs):
    for p in prefs:
        if n % p == 0:
            return p
    return n


def _cparams(n_axes):
    return pltpu.CompilerParams(dimension_semantics=("arbitrary",) * n_axes,
                                vmem_limit_bytes=VMEM_LIMIT_BYTES)


def _mm_body(*refs, n_extra, epilogue, cache_w):
    a_ref, w_ref = refs[0], refs[1]
    extra = refs[2:2 + n_extra]
    o_ref = refs[2 + n_extra]
    if cache_w:
        wbf_ref = refs[3 + n_extra]

        @pl.when(pl.program_id(1) == 0)
        def _():
            wbf_ref[...] = w_ref[...].astype(BF16)

        w = wbf_ref[...]
    else:
        w = w_ref[...].astype(BF16)
    acc = jnp.dot(a_ref[...], w, preferred_element_type=F32)
    if epilogue is not None:
        acc = epilogue(acc, *extra)
    o_ref[...] = acc.astype(o_ref.dtype)


def _matmul(a, w, *, n_cols, bm, bn, out_dtype, w_prefix=(), n_off=0, epilogue=None, extras=(), name):
    M, K = a.shape
    assert M % bm == 0 and n_cols % bn == 0 and n_off % bn == 0
    nb_off = n_off // bn
    grid = (n_cols // bn, M // bm)
    w_block = (None,) * len(w_prefix) + (K, bn)
    in_specs = [pl.BlockSpec((bm, K), lambda n, m: (m, 0)),
                pl.BlockSpec(w_block, lambda n, m: (*w_prefix, 0, n + nb_off))]
    args = [a, w]
    for arr, blk, imap in extras:
        in_specs.append(pl.BlockSpec(blk, imap))
        args.append(arr)
    cache_w = (w.dtype != BF16) and grid[1] > 1
    scratch = [pltpu.VMEM((K, bn), BF16)] if cache_w else []
    return pl.pallas_call(
        functools.partial(_mm_body, n_extra=len(extras), epilogue=epilogue, cache_w=cache_w),
        grid=grid, in_specs=in_specs,
        out_specs=pl.BlockSpec((bm, bn), lambda n, m: (m, n)),
        out_shape=jax.ShapeDtypeStruct((M, n_cols), out_dtype),
        scratch_shapes=scratch, compiler_params=_cparams(2), name=name)(*args)


def _rms(x, gain):
    return x * lax.rsqrt(jnp.mean(x * x, axis=-1, keepdims=True) + NORM_EPS) * gain


def _resid_norm_body(*refs, has_y, has_mod, gate_idx, shift_idx, scale_idx):
    it = iter(refs)
    x_ref = next(it)
    y_ref, modg_ref = (next(it), next(it)) if has_y else (None, None)
    modn_ref = next(it) if has_mod else None
    gain_ref = next(it)
    xo_ref = next(it) if has_y else None
    h_ref = next(it)
    x = x_ref[...]
    if has_y:
        x = x + modg_ref[0, gate_idx:gate_idx + 1, :] * y_ref[...]
        xo_ref[...] = x
    h = _rms(x, gain_ref[...])
    if has_mod:
        h = h * (1.0 + modn_ref[0, scale_idx:scale_idx + 1, :]) + modn_ref[0, shift_idx:shift_idx + 1, :]
    h_ref[...] = h.astype(h_ref.dtype)


def _mod_row_map(blocks_per_batch, lat_blocks, n_batch):
    def imap(i):
        b = i // blocks_per_batch
        j = i % blocks_per_batch
        return (jnp.where(j < lat_blocks, b, n_batch), 0, 0)
    return imap


def _resid_norm(x, y, mod_gate, mod_norm, gain, *, gate_idx, shift_idx, scale_idx, out_dtype, tok, name):
    M, D = x.shape
    nblk = M // ROW_BLOCK
    row = pl.BlockSpec((ROW_BLOCK, D), lambda i: (i, 0))
    mod_spec = pl.BlockSpec((1, N_MOD, D), _mod_row_map(tok.blocks_per_batch, tok.lat_blocks, tok.n_batch))
    in_specs, args = [row], [x]
    if y is not None:
        in_specs += [row, mod_spec]
        args += [y, mod_gate]
    if mod_norm is not None:
        in_specs.append(mod_spec)
        args.append(mod_norm)
    in_specs.append(pl.BlockSpec((1, D), lambda i: (0, 0)))
    args.append(gain.reshape(1, D))
    out_specs, out_shape = [], []
    if y is not None:
        out_specs.append(row)
        out_shape.append(jax.ShapeDtypeStruct((M, D), F32))
    out_specs.append(row)
    out_shape.append(jax.ShapeDtypeStruct((M, D), out_dtype))
    res = pl.pallas_call(
        functools.partial(_resid_norm_body, has_y=y is not None, has_mod=mod_norm is not None, gate_idx=gate_idx,
                          shift_idx=shift_idx, scale_idx=scale_idx),
        grid=(nblk,), in_specs=in_specs, out_specs=out_specs, out_shape=out_shape,
        compiler_params=_cparams(1), name=name)(*args)
    return res if y is not None else (x, res[0])


def _rope64(x, c, slo, shi):
    return x * c - pltpu.roll(x, 96, axis=1) * slo + pltpu.roll(x, 32, axis=1) * shi


def _rope128(x, c, ss):
    return x * c + pltpu.roll(x, 64, axis=1) * ss


def _mla_prep_body(cqkv_ref, kr_ref, qg_ref, kvg_ref, c_ref, slo_ref, shi_ref, qn_ref, kvn_ref, krr_ref, *, ql):
    cqkv = cqkv_ref[...]
    qn_ref[...] = _rms(cqkv[:, :ql], qg_ref[...]).astype(BF16)
    kvn_ref[...] = _rms(cqkv[:, ql:], kvg_ref[...]).astype(BF16)
    krr_ref[...] = _rope64(kr_ref[...], c_ref[...], slo_ref[...], shi_ref[...]).astype(BF16)


def _mla_prep(cqkv, kr, q_gain, kv_gain, rope64, tok):
    M, W = cqkv.shape
    ql, kvl = q_gain.shape[-1], kv_gain.shape[-1]
    bpb = tok.blocks_per_batch
    tab = pl.BlockSpec((ROW_BLOCK, LANES), lambda i: (i % bpb, 0))
    return pl.pallas_call(
        functools.partial(_mla_prep_body, ql=ql),
        grid=(M // ROW_BLOCK,),
        in_specs=[pl.BlockSpec((ROW_BLOCK, W), lambda i: (i, 0)),
                  pl.BlockSpec((ROW_BLOCK, LANES), lambda i: (i, 0)),
                  pl.BlockSpec((1, ql), lambda i: (0, 0)),
                  pl.BlockSpec((1, kvl), lambda i: (0, 0)), tab, tab, tab],
        out_specs=[pl.BlockSpec((ROW_BLOCK, ql), lambda i: (i, 0)),
                   pl.BlockSpec((ROW_BLOCK, kvl), lambda i: (i, 0)),
                   pl.BlockSpec((ROW_BLOCK, LANES), lambda i: (i, 0))],
        out_shape=[jax.ShapeDtypeStruct((M, ql), BF16), jax.ShapeDtypeStruct((M, kvl), BF16),
                   jax.ShapeDtypeStruct((M, LANES), BF16)],
        compiler_params=_cparams(1), name="mla_prep")(
            cqkv, kr, q_gain.reshape(1, ql), kv_gain.reshape(1, kvl), *rope64)


def _gqa_prep_body(p_ref, qg_ref, kg_ref, c_ref, ss_ref, q_ref, k_ref, v_ref, *, n_q, n_kv, scale):
    c, ss = c_ref[...], ss_ref[...]
    for h in range(n_q):
        xh = p_ref[:, h * HEAD_DIM:(h + 1) * HEAD_DIM]
        q_ref[:, h * HEAD_DIM:(h + 1) * HEAD_DIM] = (_rope128(_rms(xh, qg_ref[...]), c, ss) * scale).astype(BF16)
    off = n_q * HEAD_DIM
    for h in range(n_kv):
        xh = p_ref[:, off + h * HEAD_DIM:off + (h + 1) * HEAD_DIM]
        k_ref[:, h * HEAD_DIM:(h + 1) * HEAD_DIM] = _rope128(_rms(xh, kg_ref[...]), c, ss).astype(BF16)
    off += n_kv * HEAD_DIM
    v_ref[...] = p_ref[:, off:off + n_kv * HEAD_DIM].astype(BF16)


def _gqa_prep(p, q_gain, k_gain, rope128, n_q, n_kv, tok):
    M, W = p.shape
    bpb = tok.blocks_per_batch
    tab = pl.BlockSpec((ROW_BLOCK, LANES), lambda i: (i % bpb, 0))
    wq, wk = n_q * HEAD_DIM, n_kv * HEAD_DIM
    return pl.pallas_call(
        functools.partial(_gqa_prep_body, n_q=n_q, n_kv=n_kv, scale=HEAD_DIM ** -0.5),
        grid=(M // ROW_BLOCK,),
        in_specs=[pl.BlockSpec((ROW_BLOCK, W), lambda i: (i, 0)),
                  pl.BlockSpec((1, HEAD_DIM), lambda i: (0, 0)),
                  pl.BlockSpec((1, HEAD_DIM), lambda i: (0, 0)), tab, tab],
        out_specs=[pl.BlockSpec((ROW_BLOCK, wq), lambda i: (i, 0)),
                   pl.BlockSpec((ROW_BLOCK, wk), lambda i: (i, 0)),
                   pl.BlockSpec((ROW_BLOCK, wk), lambda i: (i, 0))],
        out_shape=[jax.ShapeDtypeStruct((M, wq), BF16), jax.ShapeDtypeStruct((M, wk), BF16),
                   jax.ShapeDtypeStruct((M, wk), BF16)],
        compiler_params=_cparams(1), name="gqa_prep")(
            p, q_gain.reshape(1, HEAD_DIM), k_gain.reshape(1, HEAD_DIM), *rope128)


def _flash_body(*refs, has_kr, s_scale):
    if has_kr:
        q_ref, k_ref, kr_ref, v_ref, o_ref, m_sc, l_sc, acc_sc = refs
    else:
        q_ref, k_ref, v_ref, o_ref, m_sc, l_sc, acc_sc = refs
    kv = pl.program_id(3)

    @pl.when(kv == 0)
    def _():
        m_sc[...] = jnp.full(m_sc.shape, -jnp.inf, F32)
        l_sc[...] = jnp.zeros(l_sc.shape, F32)
        acc_sc[...] = jnp.zeros(acc_sc.shape, F32)

    k = k_ref[0]
    if has_kr:
        k = jnp.concatenate([k, kr_ref[0]], axis=-1)
    s = lax.dot_general(q_ref[0], k, (((1,), (1,)), ((), ())), preferred_element_type=F32)
    if s_scale is not None:
        s = s * s_scale
    m_prev = m_sc[:, :1]
    m_cur = jnp.maximum(m_prev, jnp.max(s, axis=1, keepdims=True))
    alpha = jnp.exp(m_prev - m_cur)
    p = jnp.exp(s - m_cur)
    l_new = alpha * l_sc[:, :1] + jnp.sum(p, axis=1, keepdims=True)
    acc_sc[...] = alpha * acc_sc[...] + jnp.dot(p.astype(BF16), v_ref[0], preferred_element_type=F32)
    m_sc[...] = jnp.broadcast_to(m_cur, m_sc.shape)
    l_sc[...] = jnp.broadcast_to(l_new, l_sc.shape)

    @pl.when(kv == pl.num_programs(3) - 1)
    def _():
        o_ref[0] = (acc_sc[...] / l_sc[:, :1]).astype(o_ref.dtype)


def _flash(q, k, v, kr, *, n_heads, dq, k_lane, v_lane, bq, q_off, n_q, bk, k_off, n_k, name, s_scale=None):
    B = q.shape[0]
    in_specs = [pl.BlockSpec((1, bq, dq), lambda b, h, i, j: (b, i + q_off, h)),
                pl.BlockSpec((1, bk, HEAD_DIM), lambda b, h, i, j: (b, j + k_off, k_lane(h)))]
    args = [q, k]
    if kr is not None:
        in_specs.append(pl.BlockSpec((1, bk, LANES), lambda b, h, i, j: (b, j + k_off, 0)))
        args.append(kr)
    in_specs.append(pl.BlockSpec((1, bk, HEAD_DIM), lambda b, h, i, j: (b, j + k_off, v_lane(h))))
    args.append(v)
    return pl.pallas_call(
        functools.partial(_flash_body, has_kr=kr is not None, s_scale=s_scale),
        grid=(B, n_heads, n_q, n_k), in_specs=in_specs,
        out_specs=pl.BlockSpec((1, bq, HEAD_DIM), lambda b, h, i, j: (b, i, h)),
        out_shape=jax.ShapeDtypeStruct((B, n_q * bq, n_heads * HEAD_DIM), BF16),
        scratch_shapes=[pltpu.VMEM((bq, LANES), F32), pltpu.VMEM((bq, LANES), F32),
                        pltpu.VMEM((bq, HEAD_DIM), F32)],
        compiler_params=_cparams(4), name=name)(*args)


NA_Q_ROWS = 8
NA_K_ROWS = 16
NA_CHUNK = 256
NA_N_CHUNKS = NA_K_ROWS * GRID_W // NA_CHUNK


def _na_body(*refs, scale):
    q_ref = refs[0]
    k_refs = refs[1:1 + NA_N_CHUNKS + 1]
    v_refs = refs[2 + NA_N_CHUNKS:3 + 2 * NA_N_CHUNKS]
    bias_ref = refs[3 + 2 * NA_N_CHUNKS]
    o_ref = refs[4 + 2 * NA_N_CHUNKS]
    q = q_ref[0]
    parts = [lax.dot_general(q, kr[0], (((1,), (1,)), ((), ())), preferred_element_type=F32) * scale
             for kr in k_refs]
    s_loc = jnp.concatenate(parts[:NA_N_CHUNKS], axis=1) + bias_ref[0, 0]
    s = jnp.concatenate([s_loc, parts[NA_N_CHUNKS]], axis=1)
    m = jnp.max(s, axis=1, keepdims=True)
    p = jnp.exp(s - m)
    l = jnp.sum(p, axis=1, keepdims=True)
    pb = p.astype(BF16)
    acc = None
    for j, vr in enumerate(v_refs):
        t = jnp.dot(pb[:, j * NA_CHUNK:(j + 1) * NA_CHUNK], vr[0], preferred_element_type=F32)
        acc = t if acc is None else acc + t
    o_ref[0] = (acc / l).astype(o_ref.dtype)


def _na_bias_tables(rpb, rows):
    n_groups = rows // NA_Q_ROWS
    tabs = []
    j = np.arange(NA_Q_ROWS)[:, None, None, None]
    qc = np.arange(GRID_W)[None, :, None, None]
    i = np.arange(NA_K_ROWS)[None, None, :, None]
    kc = np.arange(GRID_W)[None, None, None, :]
    for g in (0, 1, n_groups - 1):
        qr = NA_Q_ROWS * g + j
        k0 = np.clip(NA_Q_ROWS * g - NA_WIN_H // 2, 0, rows - NA_K_ROWS)
        kr = k0 + i
        rs = np.clip(qr - NA_WIN_H // 2, 0, rows - NA_WIN_H)
        cs = np.clip(qc - NA_WIN_W // 2, 0, GRID_W - NA_WIN_W)
        valid = (kr >= rs) & (kr < rs + NA_WIN_H) & (kc >= cs) & (kc < cs + NA_WIN_W)
        dr = np.clip(kr - qr + (NA_WIN_H - 1), 0, 2 * NA_WIN_H - 2)
        dc = np.clip(kc - qc + (NA_WIN_W - 1), 0, 2 * NA_WIN_W - 2)
        shape = (NA_Q_ROWS, GRID_W, NA_K_ROWS, GRID_W)
        valid, dr, dc = (np.broadcast_to(a, shape).reshape(NA_Q_ROWS * GRID_W, NA_K_ROWS * GRID_W)
                         for a in (valid, dr, dc))
        tabs.append(jnp.where(valid[None], rpb[:, dr, dc].astype(F32), NEG_BIG))
    return jnp.stack(tabs)


def _na_attention(qkv, bias_tab, *, n_heads, seq, ctx_chunk):
    B = qkv.shape[0]
    rows = seq // GRID_W
    n_groups = rows // NA_Q_ROWS
    bq = NA_Q_ROWS * GRID_W
    max_cb = seq // NA_CHUNK - NA_N_CHUNKS

    def cb(g):
        return jnp.clip(2 * g - 1, 0, max_cb)

    in_specs = [pl.BlockSpec((1, bq, HEAD_DIM), lambda b, g, h: (b, g, h))]
    for part in (1, 2):
        for c in range(NA_N_CHUNKS):
            in_specs.append(pl.BlockSpec((1, NA_CHUNK, HEAD_DIM),
                                         lambda b, g, h, c=c, part=part: (b, cb(g) + c, part * n_heads + h)))
        in_specs.append(pl.BlockSpec((1, NA_CHUNK, HEAD_DIM),
                                     lambda b, g, h, part=part: (b, ctx_chunk, part * n_heads + h)))
    in_specs.append(pl.BlockSpec(
        (1, 1, bq, NA_K_ROWS * GRID_W),
        lambda b, g, h: (jnp.where(g == 0, 0, jnp.where(g == n_groups - 1, 2, 1)), h, 0, 0)))
    n_in = 2 * (NA_N_CHUNKS + 1)
    return pl.pallas_call(
        functools.partial(_na_body, scale=HEAD_DIM ** -0.5),
        grid=(B, n_groups, n_heads), in_specs=in_specs,
        out_specs=pl.BlockSpec((1, bq, HEAD_DIM), lambda b, g, h: (b, g, h)),
        out_shape=jax.ShapeDtypeStruct((B, seq, n_heads * HEAD_DIM), BF16),
        compiler_params=_cparams(3), name="na_attention")(qkv, *([qkv] * n_in), bias_tab)


def _softplus(z):
    return jnp.maximum(z, 0.0) + jnp.log(1.0 + jnp.exp(-jnp.abs(z)))


def _lru_gates_body(x_ref, hp_ref, hn_ref, cw_ref, cb_ref, w_ref, ba_ref, bx_ref, lam_ref,
                    af_ref, bf_ref, ab_ref, bb_ref, xs_ref, *, bpb, lat_blocks, n_blocks_lru):
    i = pl.program_id(0)
    j = i % bpb
    first = jnp.logical_or(j == 0, j == lat_blocks)
    last = jnp.logical_or(j == lat_blocks - 1, j == bpb - 1)
    x = x_ref[...]
    xs_ref[SUBLANES:SUBLANES + ROW_BLOCK, :] = x
    xs_ref[0:SUBLANES, :] = jnp.where(first, 0.0, hp_ref[...])
    xs_ref[SUBLANES + ROW_BLOCK:, :] = jnp.where(last, 0.0, hn_ref[...])
    xc = cb_ref[...] + cw_ref[2:3, :] * x
    for tap, off in ((0, -2), (1, -1), (3, 1)):
        xc = xc + cw_ref[tap:tap + 1, :] * xs_ref[SUBLANES + off:SUBLANES + off + ROW_BLOCK, :]
    xcb = xc.astype(BF16)
    zs = [jnp.dot(xcb[:, n * LANES:(n + 1) * LANES], w_ref[n].astype(BF16), preferred_element_type=F32)
          for n in range(n_blocks_lru)]
    for d, (a_ref, b_ref) in enumerate(((af_ref, bf_ref), (ab_ref, bb_ref))):
        za = jnp.concatenate([z[:, (2 * d) * LANES:(2 * d + 1) * LANES] for z in zs], axis=1)
        zx = jnp.concatenate([z[:, (2 * d + 1) * LANES:(2 * d + 2) * LANES] for z in zs], axis=1)
        r = jax.nn.sigmoid(za + ba_ref[d:d + 1, :])
        ig = jax.nn.sigmoid(zx + bx_ref[d:d + 1, :])
        log_a = (-LRU_C) * r * _softplus(-lam_ref[d:d + 1, :])
        a = jnp.exp(log_a)
        a_ref[...] = a
        b_ref[...] = jnp.sqrt(1.0 - jnp.exp(2.0 * log_a)) * (ig * xc)


def _lru_gates(p_lru, conv_w, conv_b, w_cat, b_a, b_x, lam, tok):
    M = p_lru.shape[0]
    W = conv_b.shape[-1]
    nb = W // LANES
    rb8 = ROW_BLOCK // SUBLANES
    n_tiles8 = M // SUBLANES
    row = pl.BlockSpec((ROW_BLOCK, W), lambda i: (i, 0))
    vec2 = pl.BlockSpec((2, W), lambda i: (0, 0))
    return pl.pallas_call(
        functools.partial(_lru_gates_body, bpb=tok.blocks_per_batch, lat_blocks=tok.lat_blocks, n_blocks_lru=nb),
        grid=(M // ROW_BLOCK,),
        in_specs=[row,
                  pl.BlockSpec((SUBLANES, W), lambda i: (jnp.maximum(i * rb8 - 1, 0), 0)),
                  pl.BlockSpec((SUBLANES, W), lambda i: (jnp.minimum((i + 1) * rb8, n_tiles8 - 1), 0)),
                  pl.BlockSpec((LRU_CONV, W), lambda i: (0, 0)),
                  pl.BlockSpec((1, W), lambda i: (0, 0)),
                  pl.BlockSpec((nb, LANES, 4 * LANES), lambda i: (0, 0, 0)),
                  vec2, vec2, vec2],
        out_specs=[row] * 4,
        out_shape=[jax.ShapeDtypeStruct((M, W), F32)] * 4,
        scratch_shapes=[pltpu.VMEM((ROW_BLOCK + 2 * SUBLANES, W), F32)],
        compiler_params=_cparams(1), name="lru_gates")(
            p_lru, p_lru, p_lru, conv_w, conv_b.reshape(1, W), w_cat, b_a, b_x, lam)


LRU_UNROLL = 8


def _lru_scan_body(af_ref, bf_ref, ab_ref, bb_ref, hf_ref, hb_ref, carry_ref):
    @pl.when(pl.program_id(1) == 0)
    def _():
        carry_ref[...] = jnp.zeros(carry_ref.shape, F32)

    n = af_ref.shape[1]

    def step(t, hs):
        hf, hb = hs
        tb = n - 1 - t
        hf = af_ref[0, t] * hf + bf_ref[0, t]
        hb = ab_ref[0, tb] * hb + bb_ref[0, tb]
        hf_ref[0, t] = hf
        hb_ref[0, tb] = hb
        return hf, hb

    hf, hb = lax.fori_loop(0, n, step, (carry_ref[0], carry_ref[1]), unroll=LRU_UNROLL)
    carry_ref[0] = hf
    carry_ref[1] = hb


def _lru_scan(a_f, b_f, a_b, b_b, tok):
    B, T, R, _ = a_f.shape
    lat = tok.lat_blocks
    n_chunks = tok.blocks_per_batch

    def fwd(b, k):
        return (b, jnp.where(k == 0, lat, k - 1), 0, 0)

    def bwd(b, k):
        return (b, jnp.where(k == 0, lat, lat - k), 0, 0)

    blk = (1, ROW_BLOCK, R, LANES)
    return pl.pallas_call(
        _lru_scan_body, grid=(B, n_chunks),
        in_specs=[pl.BlockSpec(blk, fwd), pl.BlockSpec(blk, fwd), pl.BlockSpec(blk, bwd), pl.BlockSpec(blk, bwd)],
        out_specs=[pl.BlockSpec(blk, fwd), pl.BlockSpec(blk, bwd)],
        out_shape=[jax.ShapeDtypeStruct(a_f.shape, F32)] * 2,
        scratch_shapes=[pltpu.VMEM((2, R, LANES), F32)],
        compiler_params=_cparams(2), name="lru_scan")(a_f, b_f, a_b, b_b)


def _gelu_tanh(x):
    return 0.5 * x * (1.0 + jnp.tanh(math.sqrt(2.0 / math.pi) * (x + 0.044715 * (x * x * x))))


def _lru_out_body(hf_ref, hb_ref, g_ref, y_ref):
    y_ref[...] = ((hf_ref[...] + hb_ref[...]) * _gelu_tanh(g_ref[...])).astype(y_ref.dtype)


def _lru_out(hf, hb, p_lru):
    M, W = hf.shape
    row = pl.BlockSpec((ROW_BLOCK, W), lambda i: (i, 0))
    return pl.pallas_call(
        _lru_out_body, grid=(M // ROW_BLOCK,),
        in_specs=[row, row, pl.BlockSpec((ROW_BLOCK, W), lambda i: (i, 1))],
        out_specs=row, out_shape=jax.ShapeDtypeStruct((M, W), BF16),
        compiler_params=_cparams(1), name="lru_out")(hf, hb, p_lru)


def _merge_body(*refs, has_prev):
    if has_prev:
        h_ref, y_ref, wg_ref, wb_ref, bg_ref, prev_ref, o_ref, wg_bf, wb_bf = refs
    else:
        h_ref, y_ref, wg_ref, wb_ref, bg_ref, o_ref, wg_bf, wb_bf = refs

    @pl.when(pl.program_id(1) == 0)
    def _():
        wg_bf[...] = wg_ref[...].astype(BF16)
        wb_bf[...] = wb_ref[...].astype(BF16)

    gate = jax.nn.sigmoid(jnp.dot(h_ref[...], wg_bf[...], preferred_element_type=F32) + bg_ref[0])
    term = gate * jnp.dot(y_ref[...], wb_bf[...], preferred_element_type=F32)
    if has_prev:
        term = term + prev_ref[...]
    o_ref[...] = term.astype(o_ref.dtype)


def _merge_branch(h, y, w_gate, b_gate, w_branch, prev, *, layer, branch, bm, bn, out_dtype):
    M, D = h.shape
    kb = y.shape[1]
    in_specs = [pl.BlockSpec((bm, D), lambda n, m: (m, 0)),
                pl.BlockSpec((bm, kb), lambda n, m: (m, 0)),
                pl.BlockSpec((None, None, D, bn), lambda n, m: (layer, branch, 0, n)),
                pl.BlockSpec((None, None, kb, bn), lambda n, m: (layer, branch, 0, n)),
                pl.BlockSpec((None, 1, 1, bn), lambda n, m: (layer, branch, 0, n))]
    args = [h, y, w_gate, w_branch, b_gate.reshape(b_gate.shape[0], b_gate.shape[1], 1, D)]
    if prev is not None:
        in_specs.append(pl.BlockSpec((bm, bn), lambda n, m: (m, n)))
        args.append(prev)
    return pl.pallas_call(
        functools.partial(_merge_body, has_prev=prev is not None),
        grid=(D // bn, M // bm), in_specs=in_specs,
        out_specs=pl.BlockSpec((bm, bn), lambda n, m: (m, n)),
        out_shape=jax.ShapeDtypeStruct((M, D), out_dtype),
        scratch_shapes=[pltpu.VMEM((D, bn), BF16), pltpu.VMEM((kb, bn), BF16)],
        compiler_params=_cparams(2), name=f"merge_branch{branch}")(*args)


ROUTER_BLOCK = 512
EXPERT_BLOCK = 512


def _router_body(h_ref, wr_ref, rb_ref, tri_ref, ids_ref, wts_ref, cnt_ref, carry_ref, *, n_exp):
    @pl.when(pl.program_id(0) == 0)
    def _():
        carry_ref[...] = jnp.zeros(carry_ref.shape, F32)

    per = n_exp // N_GROUPS
    logits = lax.dot_general(wr_ref[...], h_ref[...], (((1,), (1,)), ((), ())), preferred_element_type=F32)
    scores = jax.nn.sigmoid(logits)
    sel = scores + rb_ref[:, :1]
    sel_r = [sel[e:e + 1, :] for e in range(n_exp)]
    sc_r = [scores[e:e + 1, :] for e in range(n_exp)]
    gs = []
    for g in range(N_GROUPS):
        rows = sel_r[g * per:(g + 1) * per]
        best = None
        for a in range(per):
            for b in range(a + 1, per):
                ps = rows[a] + rows[b]
                best = ps if best is None else jnp.maximum(best, ps)
        gs.append(best)
    bestg = jnp.full(gs[0].shape, N_GROUPS - 1, I32)
    run = gs[N_GROUPS - 1]
    for g in range(N_GROUPS - 2, -1, -1):
        take = gs[g] >= run
        bestg = jnp.where(take, g, bestg)
        run = jnp.where(take, gs[g], run)
    v = [sel_r[i] for i in range(per)]
    sc = [sc_r[i] for i in range(per)]
    for g in range(1, N_GROUPS):
        isg = bestg == g
        v = [jnp.where(isg, sel_r[g * per + i], v[i]) for i in range(per)]
        sc = [jnp.where(isg, sc_r[g * per + i], sc[i]) for i in range(per)]

    def first_argmax(vals):
        idx = jnp.full(vals[0].shape, per - 1, I32)
        mx = vals[per - 1]
        sv = sc[per - 1]
        for i in range(per - 2, -1, -1):
            take = vals[i] >= mx
            idx = jnp.where(take, i, idx)
            mx = jnp.where(take, vals[i], mx)
            sv = jnp.where(take, sc[i], sv)
        return idx, sv

    i1, s1 = first_argmax(v)
    v2 = [jnp.where(i1 == i, -jnp.inf, v[i]) for i in range(per)]
    i2, s2 = first_argmax(v2)
    denom = s1 + s2
    e0 = bestg * per + i1
    e1 = bestg * per + i2
    eid = lax.broadcasted_iota(I32, scores.shape, 0)
    oh0 = (eid == e0).astype(F32)
    oh1 = (eid == e1).astype(F32)
    oh = oh0 + oh1
    prefix = jnp.dot(oh.astype(BF16), tri_ref[...], preferred_element_type=F32) + carry_ref[:, :1]
    r0 = jnp.sum(oh0 * prefix, axis=0, keepdims=True)
    r1 = jnp.sum(oh1 * prefix, axis=0, keepdims=True)
    new_carry = carry_ref[:, :1] + jnp.sum(oh, axis=1, keepdims=True)
    carry_ref[...] = jnp.broadcast_to(new_carry, carry_ref.shape)
    cnt_ref[...] = jnp.broadcast_to(new_carry, cnt_ref.shape)
    ids_ref[...] = jnp.zeros(ids_ref.shape, I32)
    wts_ref[...] = jnp.zeros(wts_ref.shape, F32)
    for r, val in enumerate((e0, e1, r0.astype(I32), r1.astype(I32))):
        ids_ref[r:r + 1, :] = val
    wts_ref[0:1, :] = s1 / denom
    wts_ref[1:2, :] = s2 / denom


def _router(h2, w_router, router_bias):
    M, D = h2.shape
    n_exp = w_router.shape[1]
    bm = ROUTER_BLOCK
    tri = jnp.asarray(np.triu(np.ones((bm, bm), np.float32), k=1), BF16)
    wr_t = w_router.T.astype(BF16)
    rb = jnp.broadcast_to(router_bias.astype(F32)[:, None], (n_exp, LANES))
    return pl.pallas_call(
        functools.partial(_router_body, n_exp=n_exp),
        grid=(M // bm,),
        in_specs=[pl.BlockSpec((bm, D), lambda i: (i, 0)),
                  pl.BlockSpec((n_exp, D), lambda i: (0, 0)),
                  pl.BlockSpec((n_exp, LANES), lambda i: (0, 0)),
                  pl.BlockSpec((bm, bm), lambda i: (0, 0))],
        out_specs=[pl.BlockSpec((SUBLANES, bm), lambda i: (0, i)),
                   pl.BlockSpec((SUBLANES, bm), lambda i: (0, i)),
                   pl.BlockSpec((n_exp, LANES), lambda i: (0, 0))],
        out_shape=[jax.ShapeDtypeStruct((SUBLANES, M), I32), jax.ShapeDtypeStruct((SUBLANES, M), F32),
                   jax.ShapeDtypeStruct((n_exp, LANES), F32)],
        scratch_shapes=[pltpu.VMEM((n_exp, LANES), F32)],
        compiler_params=_cparams(1), name="moe_router")(h2, wr_t, rb, tri)


DISPATCH_ROWS = 256


def _dispatch_body(pos_ref, h_hbm, xs_in_hbm, xs_hbm, sem):
    del xs_in_hbm
    base = pl.program_id(0) * DISPATCH_ROWS

    def copy(t, k):
        return pltpu.make_async_copy(h_hbm.at[pl.ds(base + t, 1)], xs_hbm.at[pl.ds(pos_ref[0, k, t], 1)], sem)

    def start(t, c):
        copy(t, 0).start()
        copy(t, 1).start()
        return c

    def wait(t, c):
        copy(t, 0).wait()
        copy(t, 1).wait()
        return c

    lax.fori_loop(0, DISPATCH_ROWS, start, 0)
    lax.fori_loop(0, DISPATCH_ROWS, wait, 0)


def _dispatch(h3, pos, n_slots):
    M = h3.shape[0]
    xs0 = jnp.zeros((n_slots,) + h3.shape[1:], h3.dtype)
    return pl.pallas_call(
        _dispatch_body, grid=(M // DISPATCH_ROWS,),
        in_specs=[pl.BlockSpec((1, 2, DISPATCH_ROWS), lambda i: (i, 0, 0), memory_space=pltpu.SMEM),
                  pl.BlockSpec(memory_space=pl.ANY), pl.BlockSpec(memory_space=pl.ANY)],
        out_specs=pl.BlockSpec(memory_space=pl.ANY),
        out_shape=jax.ShapeDtypeStruct(xs0.shape, h3.dtype),
        scratch_shapes=[pltpu.SemaphoreType.DMA(())],
        input_output_aliases={2: 0},
        compiler_params=_cparams(1), name="moe_dispatch")(pos, h3, xs0)


def _expert_up_body(be_ref, nu_ref, x_ref, wg_ref, wu_ref, o_ref, wg_bf, wu_bf):
    i = pl.program_id(1)
    fresh = jnp.logical_or(i == 0, be_ref[i] != be_ref[jnp.maximum(i - 1, 0)])

    @pl.when(jnp.logical_and(fresh, i < nu_ref[0]))
    def _():
        wg_bf[...] = wg_ref[...].astype(BF16)
        wu_bf[...] = wu_ref[...].astype(BF16)

    @pl.when(i < nu_ref[0])
    def _():
        x = x_ref[...]
        g = jnp.dot(x, wg_bf[...], preferred_element_type=F32)
        u = jnp.dot(x, wu_bf[...], preferred_element_type=F32)
        o_ref[...] = (g * jax.nn.sigmoid(g) * u).astype(o_ref.dtype)

    @pl.when(i >= nu_ref[0])
    def _():
        o_ref[...] = jnp.zeros(o_ref.shape, o_ref.dtype)


def _expert_up(xs, w_gate, w_up, blk_expert, n_used, *, layer, bn):
    P, D = xs.shape
    de = w_gate.shape[-1]
    nblk = P // EXPERT_BLOCK

    def xmap(n, i, be, nu):
        return (jnp.minimum(i, nu[0] - 1), 0)

    def wmap(n, i, be, nu):
        return (layer, be[i], 0, n)

    return pl.pallas_call(
        _expert_up_body,
        grid_spec=pltpu.PrefetchScalarGridSpec(
            num_scalar_prefetch=2, grid=(de // bn, nblk),
            in_specs=[pl.BlockSpec((EXPERT_BLOCK, D), xmap),
                      pl.BlockSpec((None, None, D, bn), wmap),
                      pl.BlockSpec((None, None, D, bn), wmap)],
            out_specs=pl.BlockSpec((EXPERT_BLOCK, bn), lambda n, i, be, nu: (i, n)),
            scratch_shapes=[pltpu.VMEM((D, bn), BF16), pltpu.VMEM((D, bn), BF16)]),
        out_shape=jax.ShapeDtypeStruct((P, de), BF16),
        compiler_params=_cparams(2), name="moe_expert_up")(blk_expert, n_used, xs, w_gate, w_up)


def _expert_down_body(be_ref, nu_ref, x_ref, wd_ref, o_ref, wd_bf):
    i = pl.program_id(1)
    fresh = jnp.logical_or(i == 0, be_ref[i] != be_ref[jnp.maximum(i - 1, 0)])

    @pl.when(jnp.logical_and(fresh, i < nu_ref[0]))
    def _():
        wd_bf[...] = wd_ref[...].astype(BF16)

    @pl.when(i < nu_ref[0])
    def _():
        o_ref[...] = jnp.dot(x_ref[...], wd_bf[...], preferred_element_type=F32).astype(o_ref.dtype)

    @pl.when(i >= nu_ref[0])
    def _():
        o_ref[...] = jnp.zeros(o_ref.shape, o_ref.dtype)


def _expert_down(he, w_down, blk_expert, n_used, *, layer, bn):
    P, de = he.shape
    D = w_down.shape[-1]
    nblk = P // EXPERT_BLOCK
    return pl.pallas_call(
        _expert_down_body,
        grid_spec=pltpu.PrefetchScalarGridSpec(
            num_scalar_prefetch=2, grid=(D // bn, nblk),
            in_specs=[pl.BlockSpec((EXPERT_BLOCK, de), lambda n, i, be, nu: (jnp.minimum(i, nu[0] - 1), 0)),
                      pl.BlockSpec((None, None, de, bn), lambda n, i, be, nu: (layer, be[i], 0, n))],
            out_specs=pl.BlockSpec((EXPERT_BLOCK, bn), lambda n, i, be, nu: (i, n)),
            scratch_shapes=[pltpu.VMEM((de, bn), BF16)]),
        out_shape=jax.ShapeDtypeStruct((P, D), F32),
        compiler_params=_cparams(2), name="moe_expert_down")(blk_expert, n_used, he, w_down)


COMBINE_UNROLL = 8


def _combine_body(pos_ref, w_ref, o_hbm, y_ref, buf, sem):
    def copy(t, k):
        return pltpu.make_async_copy(o_hbm.at[pl.ds(pos_ref[0, k, t], 1)], buf.at[k, pl.ds(t, 1)], sem)

    def start(t, c):
        copy(t, 0).start()
        copy(t, 1).start()
        return c

    def wait(t, c):
        copy(t, 0).wait()
        copy(t, 1).wait()
        return c

    def mix(t, c):
        y_ref[t] = w_ref[0, 0, t] * buf[0, t] + w_ref[0, 1, t] * buf[1, t]
        return c

    lax.fori_loop(0, DISPATCH_ROWS, start, 0)
    lax.fori_loop(0, DISPATCH_ROWS, wait, 0)
    lax.fori_loop(0, DISPATCH_ROWS, mix, 0, unroll=COMBINE_UNROLL)


def _combine(o3, pos, wts, n_rows):
    slab = o3.shape[1:]
    smem = pl.BlockSpec((1, 2, DISPATCH_ROWS), lambda i: (i, 0, 0), memory_space=pltpu.SMEM)
    return pl.pallas_call(
        _combine_body, grid=(n_rows // DISPATCH_ROWS,),
        in_specs=[smem, smem, pl.BlockSpec(memory_space=pl.ANY)],
        out_specs=pl.BlockSpec((DISPATCH_ROWS,) + slab, lambda i: (i, 0, 0)),
        out_shape=jax.ShapeDtypeStruct((n_rows,) + slab, F32),
        scratch_shapes=[pltpu.VMEM((2, DISPATCH_ROWS) + slab, F32), pltpu.SemaphoreType.DMA(())],
        compiler_params=_cparams(1), name="moe_combine")(pos, wts, o3)


def _moe(h2, w_router, router_bias, w_gate, w_up, w_down, *, layer):
    M, D = h2.shape
    n_exp = w_router.shape[1]
    ids, wts, cnt = _router(h2, w_router, router_bias)
    counts = cnt[:, 0].astype(I32)
    padded = (counts + EXPERT_BLOCK - 1) // EXPERT_BLOCK * EXPERT_BLOCK
    ends = jnp.cumsum(padded)
    offs = ends - padded
    n_slots = (2 * M + n_exp * (EXPERT_BLOCK - 1)) // EXPERT_BLOCK * EXPERT_BLOCK
    nblk = n_slots // EXPERT_BLOCK
    pos0 = offs[ids[0]] + ids[2]
    pos1 = offs[ids[1]] + ids[3]
    pos = jnp.stack([pos0.reshape(-1, DISPATCH_ROWS), pos1.reshape(-1, DISPATCH_ROWS)], axis=1)
    wpair = jnp.stack([wts[0].reshape(-1, DISPATCH_ROWS), wts[1].reshape(-1, DISPATCH_ROWS)], axis=1)
    blk_start = jnp.arange(nblk, dtype=I32) * EXPERT_BLOCK
    blk_expert = jnp.minimum(jnp.searchsorted(ends, blk_start, side="right"), n_exp - 1).astype(I32)
    n_used = (ends[-1] // EXPERT_BLOCK).astype(I32).reshape(1)
    xs = _dispatch(h2.reshape(M, D // LANES, LANES), pos, n_slots).reshape(n_slots, D)
    he = _expert_up(xs, w_gate, w_up, blk_expert, n_used, layer=layer, bn=_pick(w_gate.shape[-1], (256, 128)))
    o = _expert_down(he, w_down, blk_expert, n_used, layer=layer, bn=_pick(D, (2048, 1024, 512)))
    return _combine(o.reshape(n_slots, D // LANES, LANES), pos, wpair, M).reshape(M, D)


class _Tok:
    def __init__(self, n_batch, seq, ctx):
        self.n_batch, self.seq, self.ctx = n_batch, seq, ctx
        self.t_all = seq + ctx
        self.lat_blocks = seq // ROW_BLOCK
        self.blocks_per_batch = self.t_all // ROW_BLOCK


def _rope_tables(seq, ctx):
    t = jnp.arange(seq, dtype=I32)
    row = (t // GRID_W).astype(F32)[:, None]
    col = (t % GRID_W).astype(F32)[:, None]

    def cs(rot_dim):
        n_freq = rot_dim // 4
        freqs = ROPE_THETA ** (-jnp.arange(n_freq, dtype=F32) / n_freq)
        ang = jnp.concatenate([row * freqs, col * freqs], axis=-1)
        cos = jnp.concatenate([jnp.cos(ang), jnp.ones((ctx, rot_dim // 2), F32)], axis=0)
        sin = jnp.concatenate([jnp.sin(ang), jnp.zeros((ctx, rot_dim // 2), F32)], axis=0)
        return cos, sin

    cos, sin = cs(HEAD_DIM)
    rope128 = (jnp.concatenate([cos, cos], axis=1), jnp.concatenate([-sin, sin], axis=1))
    cos, sin = cs(MLA_ROPE)
    z32 = jnp.zeros_like(sin)
    z64 = jnp.zeros((seq + ctx, LANES - MLA_ROPE), F32)
    rope64 = (jnp.concatenate([cos, cos, z64], axis=1), jnp.concatenate([sin, z32, z64], axis=1),
              jnp.concatenate([z32, sin, z64], axis=1))
    return rope128, rope64


def _q_rope_epilogue(acc, c_ref, slo_ref, shi_ref, *, scale):
    nope = acc[:, :MLA_NOPE] * scale
    rope = _rope64(acc[:, MLA_NOPE:], c_ref[...], slo_ref[...], shi_ref[...]) * scale
    return jnp.concatenate([nope, rope], axis=1)


def kernel(x, c, ctx, c_ctx, w_mod, b_mod, norm_mix, norm_ffn, w_in, mla_q_norm, mla_w_uq, mla_kv_norm,
           mla_w_ukv, gqa_q_norm, gqa_k_norm, na_rpb, lru_conv_w, lru_conv_b, lru_w_a, lru_b_a, lru_w_x,
           lru_b_x, lru_lambda, w_branch_gate, b_branch_gate, w_branch, w_out, w_router, router_bias,
           w_exp_gate, w_exp_up, w_exp_down, final_norm):
    B, S, D = x.shape
    CTX = ctx.shape[1]
    depth = w_in.shape[0]
    tok = _Tok(B, S, CTX)
    T = tok.t_all
    M = B * T
    bw = D // 4
    n_heads = bw // HEAD_DIM
    ql, kvl = mla_q_norm.shape[-1], mla_kv_norm.shape[-1]
    d_in = w_in.shape[-1]
    n_kv = (d_in - (ql + kvl + MLA_ROPE + bw + 3 * bw + 2 * bw)) // (2 * HEAD_DIM)
    q_per_kv = n_heads // n_kv
    n_lru_blocks = lru_w_a.shape[2]
    assert bw // n_lru_blocks == LANES and S % ROW_BLOCK == 0 and CTX == ROW_BLOCK and S % (GRID_W * NA_Q_ROWS) == 0
    mla_dq = 2 * LANES
    mla_scale = (MLA_NOPE + MLA_ROPE) ** -0.5
    bm = _pick(M, (768, 512, 256))
    bm_tok = _pick(T, (768, 256))

    rope128, rope64 = _rope_tables(S, CTX)
    xa = jnp.concatenate([x, ctx], axis=1).reshape(M, D)
    cc = jnp.zeros((MOD_ROWS, D), F32).at[:B].set(c).at[B].set(c_ctx)
    silu_c = (cc * jax.nn.sigmoid(cc)).astype(BF16)

    o_mla = ql + kvl
    o_gqa = o_mla + MLA_ROPE
    w_gqa = bw + 2 * n_kv * HEAD_DIM
    o_na = o_gqa + w_gqa
    o_lru = o_na + 3 * bw

    y_ffn = mod_prev = None
    for layer in range(depth):
        need_ctx = layer < depth - 1
        modv = _matmul(silu_c, w_mod, w_prefix=(layer,), n_cols=N_MOD * D, bm=MOD_ROWS,
                       bn=_pick(N_MOD * D, (512, 256)), out_dtype=F32,
                       epilogue=lambda acc, b_ref: acc + b_ref[0],
                       extras=[(b_mod.reshape(depth, 1, N_MOD * D), (1, 1, _pick(N_MOD * D, (512, 256))),
                                lambda n, m, layer=layer: (layer, 0, n))],
                       name="mod_proj")
        modv = modv[:B + 1].reshape(B + 1, N_MOD, D)

        xa, h = _resid_norm(xa, y_ffn, mod_prev, modv, norm_mix[layer], gate_idx=5, shift_idx=0, scale_idx=1,
                            out_dtype=BF16, tok=tok, name="norm_mix")
        w_l = w_in[layer]
        cqkv = _matmul(h, w_l[:, :o_mla], n_cols=o_mla, bm=bm, bn=_pick(o_mla, (512, 384, 256, 128)),
                       out_dtype=F32, name="in_mla")
        w_kr = jnp.pad(w_l[:, o_mla:o_gqa], ((0, 0), (0, LANES - MLA_ROPE)))
        kr = _matmul(h, w_kr, n_cols=LANES, bm=bm, bn=LANES, out_dtype=F32, name="in_kr")
        p_gqa = _matmul(h, w_l[:, o_gqa:o_na], n_cols=w_gqa, bm=bm, bn=_pick(w_gqa, (512, 384, 256, 128)),
                        out_dtype=F32, name="in_gqa")
        p_na = _matmul(h, w_l[:, o_na:o_lru], n_cols=3 * bw, bm=bm, bn=_pick(3 * bw, (512, 256)),
                       out_dtype=BF16, name="in_na")
        p_lru = _matmul(h, w_l[:, o_lru:], n_cols=2 * bw, bm=bm, bn=_pick(2 * bw, (512, 256)),
                        out_dtype=F32, name="in_lru")

        qn, kvn, krr = _mla_prep(cqkv, kr, mla_q_norm[layer], mla_kv_norm[layer], rope64, tok)
        w_uq = jnp.pad(mla_w_uq[layer].reshape(ql, n_heads, MLA_NOPE + MLA_ROPE),
                       ((0, 0), (0, 0), (0, mla_dq - MLA_NOPE - MLA_ROPE))).reshape(ql, n_heads * mla_dq)
        tpb = T // bm_tok
        tab_spec = ((bm_tok, LANES), lambda n, m: (m % tpb, 0))
        q_mla = _matmul(qn, w_uq, n_cols=n_heads * mla_dq, bm=bm_tok, bn=mla_dq, out_dtype=BF16,
                        epilogue=functools.partial(_q_rope_epilogue, scale=mla_scale),
                        extras=[(t, *tab_spec) for t in rope64], name="mla_uq")
        kv_mla = _matmul(kvn, mla_w_ukv, w_prefix=(layer,), n_cols=n_heads * 2 * HEAD_DIM, bm=bm,
                         bn=_pick(n_heads * 2 * HEAD_DIM, (512, 256)), out_dtype=BF16, name="mla_ukv")
        bq = _pick(S, (512, 256))
        bk = _pick(T, (768, 256))
        ctx_blk = S // ROW_BLOCK
        q3, kv3, kr3 = (a.reshape(B, T, -1) for a in (q_mla, kv_mla, krr))
        mla_args = dict(n_heads=n_heads, dq=mla_dq, k_lane=lambda h: 2 * h, v_lane=lambda h: 2 * h + 1)
        ya = _flash(q3, kv3, kv3, kr3, bq=bq, q_off=0, n_q=S // bq, bk=bk, k_off=0, n_k=T // bk,
                    name="mla_attn", **mla_args)
        q_g, k_g, v_g = (a.reshape(B, T, -1) for a in
                         _gqa_prep(p_gqa, gqa_q_norm[layer], gqa_k_norm[layer], rope128, n_heads, n_kv, tok))
        gqa_args = dict(n_heads=n_heads, dq=HEAD_DIM, k_lane=lambda h: h // q_per_kv, v_lane=lambda h: h // q_per_kv)
        yb = _flash(q_g, k_g, v_g, None, bq=bq, q_off=0, n_q=S // bq, bk=bk, k_off=0, n_k=T // bk,
                    name="gqa_attn", **gqa_args)
        na3 = p_na.reshape(B, T, 3 * bw)
        yc = _na_attention(na3, _na_bias_tables(na_rpb[layer], S // GRID_W), n_heads=n_heads, seq=S,
                           ctx_chunk=S // NA_CHUNK)
        if need_ctx:
            ctx_q = dict(bq=ROW_BLOCK, q_off=ctx_blk, n_q=1, bk=ROW_BLOCK, k_off=ctx_blk, n_k=1)
            ya_c = _flash(q3, kv3, kv3, kr3, name="mla_attn_ctx", **ctx_q, **mla_args)
            yb_c = _flash(q_g, k_g, v_g, None, name="gqa_attn_ctx", **ctx_q, **gqa_args)
            yc_c = _flash(na3, na3, na3, None, name="na_attn_ctx", n_heads=n_heads, dq=HEAD_DIM,
                          k_lane=lambda h: n_heads + h, v_lane=lambda h: 2 * n_heads + h,
                          s_scale=HEAD_DIM ** -0.5, **ctx_q)
        else:
            ya_c = yb_c = yc_c = jnp.zeros((B, CTX, bw), BF16)
        ya, yb, yc = (jnp.concatenate([l_, c_], axis=1).reshape(M, bw)
                      for l_, c_ in ((ya, ya_c), (yb, yb_c), (yc, yc_c)))
        w_cat = jnp.concatenate([lru_w_a[layer, 0], lru_w_x[layer, 0], lru_w_a[layer, 1], lru_w_x[layer, 1]],
                                axis=-1)
        gates = _lru_gates(p_lru, lru_conv_w[layer], lru_conv_b[layer], w_cat, lru_b_a[layer], lru_b_x[layer],
                           lru_lambda[layer], tok)
        hf, hb = _lru_scan(*(g.reshape(B, T, n_lru_blocks, LANES) for g in gates), tok)
        yd = _lru_out(hf.reshape(M, bw), hb.reshape(M, bw), p_lru)

        bn_merge = _pick(D, (512, 256))
        bm_merge = _pick(M, (512, 256))
        merged = None
        for i_br, y_br in enumerate((ya, yb, yc, yd)):
            merged = _merge_branch(h, y_br, w_branch_gate, b_branch_gate, w_branch, merged, layer=layer,
                                   branch=i_br, bm=bm_merge, bn=bn_merge,
                                   out_dtype=BF16 if i_br == 3 else F32)
        y = _matmul(merged, w_out, w_prefix=(layer,), n_cols=D, bm=bm, bn=_pick(D, (512, 256)), out_dtype=F32,
                    name="out_proj")
        xa, h2 = _resid_norm(xa, y, modv, modv, norm_ffn[layer], gate_idx=2, shift_idx=3, scale_idx=4,
                             out_dtype=BF16, tok=tok, name="resid_norm_ffn")
        y_ffn = _moe(h2, w_router, router_bias, w_exp_gate, w_exp_up, w_exp_down, layer=layer)
        mod_prev = modv

    _, out = _resid_norm(xa, y_ffn, mod_prev, None, final_norm, gate_idx=5, shift_idx=None, scale_idx=None,
                         out_dtype=F32, tok=tok, name="final_norm")
    return out.reshape(B, T, D)[:, :S]
```

```python
import functools
import math

import numpy as np
import jax
import jax.numpy as jnp
from jax import lax
from jax.experimental import pallas as pl
from jax.experimental.pallas import tpu as pltpu

F32 = jnp.float32
BF16 = jnp.bfloat16
I32 = jnp.int32

HEAD_DIM = 128
GRID_W = 64
ROPE_THETA = 10000.0
NORM_EPS = 1e-6
MLA_NOPE = 128
MLA_ROPE = 64
NA_WIN_H = 8
NA_WIN_W = 16
LRU_CONV = 4
LRU_C = 8.0
N_GROUPS = 4
N_MOD = 6

LANES = 128
SUBLANES = 8
MOD_ROWS = 16
VMEM_LIMIT_BYTES = 58 * 1024 * 1024
ROW_BLOCK = 256
NEG_BIG = -1e30
LOG2E = math.log2(math.e)


def _pick(n, prefs):
    for p in prefs:
        if n % p == 0:
            return p
    return n


def _cparams(n_axes):
    return pltpu.CompilerParams(dimension_semantics=("arbitrary",) * n_axes,
                                vmem_limit_bytes=VMEM_LIMIT_BYTES)


def _mm_body(*refs, n_extra, epilogue, cache_w):
    a_ref, w_ref = refs[0], refs[1]
    extra = refs[2:2 + n_extra]
    o_ref = refs[2 + n_extra]
    if cache_w:
        wbf_ref = refs[3 + n_extra]

        @pl.when(pl.program_id(1) == 0)
        def _():
            wbf_ref[...] = w_ref[...].astype(BF16)

        w = wbf_ref[...]
    else:
        w = w_ref[...].astype(BF16)
    acc = jnp.dot(a_ref[...], w, preferred_element_type=F32)
    if epilogue is not None:
        acc = epilogue(acc, *extra)
    o_ref[...] = acc.astype(o_ref.dtype)


def _matmul(a, w, *, n_cols, bm, bn, out_dtype, w_prefix=(), n_off=0, epilogue=None, extras=(), name):
    M, K = a.shape
    assert M % bm == 0 and n_cols % bn == 0 and n_off % bn == 0
    nb_off = n_off // bn
    grid = (n_cols // bn, M // bm)
    w_block = (None,) * len(w_prefix) + (K, bn)
    in_specs = [pl.BlockSpec((bm, K), lambda n, m: (m, 0)),
                pl.BlockSpec(w_block, lambda n, m: (*w_prefix, 0, n + nb_off))]
    args = [a, w]
    for arr, blk, imap in extras:
        in_specs.append(pl.BlockSpec(blk, imap))
        args.append(arr)
    cache_w = (w.dtype != BF16) and grid[1] > 1
    scratch = [pltpu.VMEM((K, bn), BF16)] if cache_w else []
    return pl.pallas_call(
        functools.partial(_mm_body, n_extra=len(extras), epilogue=epilogue, cache_w=cache_w),
        grid=grid, in_specs=in_specs,
        out_specs=pl.BlockSpec((bm, bn), lambda n, m: (m, n)),
        out_shape=jax.ShapeDtypeStruct((M, n_cols), out_dtype),
        scratch_shapes=scratch, compiler_params=_cparams(2), name=name)(*args)


def _rms(x, gain):
    return x * lax.rsqrt(jnp.mean(x * x, axis=-1, keepdims=True) + NORM_EPS) * gain


def _resid_norm_body(*refs, has_y, has_mod, gate_idx, shift_idx, scale_idx):
    it = iter(refs)
    x_ref = next(it)
    y_ref, modg_ref = (next(it), next(it)) if has_y else (None, None)
    modn_ref = next(it) if has_mod else None
    gain_ref = next(it)
    xo_ref = next(it) if has_y else None
    h_ref = next(it)
    x = x_ref[...]
    if has_y:
        x = x + modg_ref[0, gate_idx:gate_idx + 1, :] * y_ref[...]
        xo_ref[...] = x
    h = _rms(x, gain_ref[...])
    if has_mod:
        h = h * (1.0 + modn_ref[0, scale_idx:scale_idx + 1, :]) + modn_ref[0, shift_idx:shift_idx + 1, :]
    h_ref[...] = h.astype(h_ref.dtype)


def _mod_row_map(blocks_per_batch, lat_blocks, n_batch):
    def imap(i):
        b = i // blocks_per_batch
        j = i % blocks_per_batch
        return (jnp.where(j < lat_blocks, b, n_batch), 0, 0)
    return imap


def _resid_norm(x, y, mod_gate, mod_norm, gain, *, gate_idx, shift_idx, scale_idx, out_dtype, tok, name):
    M, D = x.shape
    nblk = M // ROW_BLOCK
    row = pl.BlockSpec((ROW_BLOCK, D), lambda i: (i, 0))
    mod_spec = pl.BlockSpec((1, N_MOD, D), _mod_row_map(tok.blocks_per_batch, tok.lat_blocks, tok.n_batch))
    in_specs, args = [row], [x]
    if y is not None:
        in_specs += [row, mod_spec]
        args += [y, mod_gate]
    if mod_norm is not None:
        in_specs.append(mod_spec)
        args.append(mod_norm)
    in_specs.append(pl.BlockSpec((1, D), lambda i: (0, 0)))
    args.append(gain.reshape(1, D))
    out_specs, out_shape = [], []
    if y is not None:
        out_specs.append(row)
        out_shape.append(jax.ShapeDtypeStruct((M, D), F32))
    out_specs.append(row)
    out_shape.append(jax.ShapeDtypeStruct((M, D), out_dtype))
    res = pl.pallas_call(
        functools.partial(_resid_norm_body, has_y=y is not None, has_mod=mod_norm is not None, gate_idx=gate_idx,
                          shift_idx=shift_idx, scale_idx=scale_idx),
        grid=(nblk,), in_specs=in_specs, out_specs=out_specs, out_shape=out_shape,
        compiler_params=_cparams(1), name=name)(*args)
    return res if y is not None else (x, res[0])


def _rope64(x, c, slo, shi):
    return x * c - pltpu.roll(x, 96, axis=1) * slo + pltpu.roll(x, 32, axis=1) * shi


def _rope128(x, c, ss):
    return x * c + pltpu.roll(x, 64, axis=1) * ss


def _mla_prep_body(cqkv_ref, kr_ref, qg_ref, kvg_ref, c_ref, slo_ref, shi_ref, qn_ref, kvn_ref, krr_ref, *, ql):
    cqkv = cqkv_ref[...]
    qn_ref[...] = _rms(cqkv[:, :ql], qg_ref[...]).astype(BF16)
    kvn_ref[...] = _rms(cqkv[:, ql:], kvg_ref[...]).astype(BF16)
    krr_ref[...] = _rope64(kr_ref[...], c_ref[...], slo_ref[...], shi_ref[...]).astype(BF16)


def _mla_prep(cqkv, kr, q_gain, kv_gain, rope64, tok):
    M, W = cqkv.shape
    ql, kvl = q_gain.shape[-1], kv_gain.shape[-1]
    bpb = tok.blocks_per_batch
    tab = pl.BlockSpec((ROW_BLOCK, LANES), lambda i: (i % bpb, 0))
    return pl.pallas_call(
        functools.partial(_mla_prep_body, ql=ql),
        grid=(M // ROW_BLOCK,),
        in_specs=[pl.BlockSpec((ROW_BLOCK, W), lambda i: (i, 0)),
                  pl.BlockSpec((ROW_BLOCK, LANES), lambda i: (i, 0)),
                  pl.BlockSpec((1, ql), lambda i: (0, 0)),
                  pl.BlockSpec((1, kvl), lambda i: (0, 0)), tab, tab, tab],
        out_specs=[pl.BlockSpec((ROW_BLOCK, ql), lambda i: (i, 0)),
                   pl.BlockSpec((ROW_BLOCK, kvl), lambda i: (i, 0)),
                   pl.BlockSpec((ROW_BLOCK, LANES), lambda i: (i, 0))],
        out_shape=[jax.ShapeDtypeStruct((M, ql), BF16), jax.ShapeDtypeStruct((M, kvl), BF16),
                   jax.ShapeDtypeStruct((M, LANES), BF16)],
        compiler_params=_cparams(1), name="mla_prep")(
            cqkv, kr, q_gain.reshape(1, ql), kv_gain.reshape(1, kvl), *rope64)


def _gqa_prep_body(p_ref, qg_ref, kg_ref, c_ref, ss_ref, q_ref, k_ref, v_ref, *, n_q, n_kv, scale):
    c, ss = c_ref[...], ss_ref[...]
    for h in range(n_q):
        xh = p_ref[:, h * HEAD_DIM:(h + 1) * HEAD_DIM]
        q_ref[:, h * HEAD_DIM:(h + 1) * HEAD_DIM] = (_rope128(_rms(xh, qg_ref[...]), c, ss) * scale).astype(BF16)
    off = n_q * HEAD_DIM
    for h in range(n_kv):
        xh = p_ref[:, off + h * HEAD_DIM:off + (h + 1) * HEAD_DIM]
        k_ref[:, h * HEAD_DIM:(h + 1) * HEAD_DIM] = _rope128(_rms(xh, kg_ref[...]), c, ss).astype(BF16)
    off += n_kv * HEAD_DIM
    v_ref[...] = p_ref[:, off:off + n_kv * HEAD_DIM].astype(BF16)


def _gqa_prep(p, q_gain, k_gain, rope128, n_q, n_kv, tok):
    M, W = p.shape
    bpb = tok.blocks_per_batch
    tab = pl.BlockSpec((ROW_BLOCK, LANES), lambda i: (i % bpb, 0))
    wq, wk = n_q * HEAD_DIM, n_kv * HEAD_DIM
    return pl.pallas_call(
        functools.partial(_gqa_prep_body, n_q=n_q, n_kv=n_kv, scale=HEAD_DIM ** -0.5 * LOG2E),
        grid=(M // ROW_BLOCK,),
        in_specs=[pl.BlockSpec((ROW_BLOCK, W), lambda i: (i, 0)),
                  pl.BlockSpec((1, HEAD_DIM), lambda i: (0, 0)),
                  pl.BlockSpec((1, HEAD_DIM), lambda i: (0, 0)), tab, tab],
        out_specs=[pl.BlockSpec((ROW_BLOCK, wq), lambda i: (i, 0)),
                   pl.BlockSpec((ROW_BLOCK, wk), lambda i: (i, 0)),
                   pl.BlockSpec((ROW_BLOCK, wk), lambda i: (i, 0))],
        out_shape=[jax.ShapeDtypeStruct((M, wq), BF16), jax.ShapeDtypeStruct((M, wk), BF16),
                   jax.ShapeDtypeStruct((M, wk), BF16)],
        compiler_params=_cparams(1), name="gqa_prep")(
            p, q_gain.reshape(1, HEAD_DIM), k_gain.reshape(1, HEAD_DIM), *rope128)


def _lane_fold(x, op, init):
    for j in range(x.shape[1] // LANES):
        init = op(init, x[:, j * LANES:(j + 1) * LANES])
    return init


def _flash_body(*refs, has_kr, s_scale, n_k, bk):
    if has_kr:
        q_ref, k_ref, kr_ref, v_ref, o_ref, s_sc = refs
    else:
        q_ref, k_ref, v_ref, o_ref, s_sc = refs
    q = q_ref[0]
    bq = q.shape[0]
    m_lane = jnp.full((bq, LANES), -jnp.inf, F32)
    for c in range(n_k):
        k = k_ref[0, c * bk:(c + 1) * bk, :]
        if has_kr:
            k = jnp.concatenate([k, kr_ref[0, c * bk:(c + 1) * bk, :]], axis=-1)
        s = lax.dot_general(q, k, (((1,), (1,)), ((), ())), preferred_element_type=F32)
        if s_scale is not None:
            s = s * s_scale
        s_sc[c] = s
        m_lane = _lane_fold(s, jnp.maximum, m_lane)
    m = jnp.max(m_lane, axis=1, keepdims=True)
    ones = jnp.ones((bk, LANES), BF16)
    acc = jnp.zeros((bq, HEAD_DIM + LANES), F32)
    for c in range(n_k):
        p = jnp.exp2(s_sc[c] - m)
        v1 = jnp.concatenate([v_ref[0, c * bk:(c + 1) * bk, :], ones], axis=-1)
        acc = acc + jnp.dot(p.astype(BF16), v1, preferred_element_type=F32)
    o_ref[0] = (acc[:, :HEAD_DIM] / acc[:, HEAD_DIM:HEAD_DIM + 1]).astype(o_ref.dtype)


def _flash(q, k, v, kr, *, n_heads, dq, k_lane, v_lane, bq, q_off, n_q, bk, k_off, n_k, name, s_scale=None):
    B = q.shape[0]
    tk = n_k * bk
    in_specs = [pl.BlockSpec((1, bq, dq), lambda b, h, i: (b, i + q_off, h)),
                pl.BlockSpec((1, tk, HEAD_DIM), lambda b, h, i: (b, k_off, k_lane(h)))]
    args = [q, k]
    if kr is not None:
        in_specs.append(pl.BlockSpec((1, tk, LANES), lambda b, h, i: (b, k_off, 0)))
        args.append(kr)
    in_specs.append(pl.BlockSpec((1, tk, HEAD_DIM), lambda b, h, i: (b, k_off, v_lane(h))))
    args.append(v)
    return pl.pallas_call(
        functools.partial(_flash_body, has_kr=kr is not None, s_scale=s_scale, n_k=n_k, bk=bk),
        grid=(B, n_heads, n_q), in_specs=in_specs,
        out_specs=pl.BlockSpec((1, bq, HEAD_DIM), lambda b, h, i: (b, i, h)),
        out_shape=jax.ShapeDtypeStruct((B, n_q * bq, n_heads * HEAD_DIM), BF16),
        scratch_shapes=[pltpu.VMEM((n_k, bq, bk), F32)],
        compiler_params=_cparams(3), name=name)(*args)


NA_Q_ROWS = 8
NA_K_ROWS = 16
NA_CHUNK = 256
NA_N_CHUNKS = NA_K_ROWS * GRID_W // NA_CHUNK


def _na_body(*refs, scale):
    q_ref = refs[0]
    k_refs = refs[1:1 + NA_N_CHUNKS + 1]
    v_refs = refs[2 + NA_N_CHUNKS:3 + 2 * NA_N_CHUNKS]
    bias_ref = refs[3 + 2 * NA_N_CHUNKS]
    o_ref = refs[4 + 2 * NA_N_CHUNKS]
    q = q_ref[0]
    parts = [lax.dot_general(q, kr[0], (((1,), (1,)), ((), ())), preferred_element_type=F32) * scale
             for kr in k_refs]
    s_loc = jnp.concatenate(parts[:NA_N_CHUNKS], axis=1) + bias_ref[0, 0]
    s = jnp.concatenate([s_loc, parts[NA_N_CHUNKS]], axis=1)
    m = jnp.max(s, axis=1, keepdims=True)
    p = jnp.exp(s - m)
    l = jnp.sum(p, axis=1, keepdims=True)
    pb = p.astype(BF16)
    acc = None
    for j, vr in enumerate(v_refs):
        t = jnp.dot(pb[:, j * NA_CHUNK:(j + 1) * NA_CHUNK], vr[0], preferred_element_type=F32)
        acc = t if acc is None else acc + t
    o_ref[0] = (acc / l).astype(o_ref.dtype)


def _na_bias_tables(rpb, rows):
    n_groups = rows // NA_Q_ROWS
    n_h = rpb.shape[0]
    n_dr, n_dc = 2 * NA_WIN_H - 1, 2 * NA_WIN_W - 1
    exact = lax.Precision.HIGHEST
    qc = np.arange(GRID_W)[:, None]
    kc = np.arange(GRID_W)[None, :]
    cs = np.clip(qc - NA_WIN_W // 2, 0, GRID_W - NA_WIN_W)
    col_valid = (kc >= cs) & (kc < cs + NA_WIN_W)
    dc = np.clip(kc - qc + (NA_WIN_W - 1), 0, n_dc - 1)
    oh_c = (np.arange(n_dc)[:, None, None] == dc[None]).astype(np.float32).reshape(n_dc, GRID_W * GRID_W)
    cols = jnp.einsum('hrd,dx->hrx', rpb.astype(F32), jnp.asarray(oh_c), precision=exact)
    tabs = []
    j = np.arange(NA_Q_ROWS)[:, None]
    i = np.arange(NA_K_ROWS)[None, :]
    for g in (0, 1, n_groups - 1):
        qr = NA_Q_ROWS * g + j
        kr = np.clip(NA_Q_ROWS * g - NA_WIN_H // 2, 0, rows - NA_K_ROWS) + i
        rs = np.clip(qr - NA_WIN_H // 2, 0, rows - NA_WIN_H)
        row_valid = (kr >= rs) & (kr < rs + NA_WIN_H)
        dr = np.clip(kr - qr + (NA_WIN_H - 1), 0, n_dr - 1)
        oh_r = (np.arange(n_dr)[None, None, :] == dr[:, :, None]).astype(np.float32)
        oh_r = oh_r.reshape(NA_Q_ROWS * NA_K_ROWS, n_dr)
        t = jnp.einsum('pr,hrx->hpx', jnp.asarray(oh_r), cols, precision=exact)
        t = t.reshape(n_h, NA_Q_ROWS, NA_K_ROWS, GRID_W, GRID_W)
        valid = row_valid[:, :, None, None] & col_valid[None, None]
        t = jnp.where(valid[None], t, NEG_BIG).transpose(0, 1, 3, 2, 4)
        tabs.append(t.reshape(n_h, NA_Q_ROWS * GRID_W, NA_K_ROWS * GRID_W))
    return jnp.stack(tabs)


def _na_attention(qkv, bias_tab, *, n_heads, seq, ctx_chunk):
    B = qkv.shape[0]
    rows = seq // GRID_W
    n_groups = rows // NA_Q_ROWS
    bq = NA_Q_ROWS * GRID_W
    max_cb = seq // NA_CHUNK - NA_N_CHUNKS

    def cb(g):
        return jnp.clip(2 * g - 1, 0, max_cb)

    in_specs = [pl.BlockSpec((1, bq, HEAD_DIM), lambda b, g, h: (b, g, h))]
    for part in (1, 2):
        for c in range(NA_N_CHUNKS):
            in_specs.append(pl.BlockSpec((1, NA_CHUNK, HEAD_DIM),
                                         lambda b, g, h, c=c, part=part: (b, cb(g) + c, part * n_heads + h)))
        in_specs.append(pl.BlockSpec((1, NA_CHUNK, HEAD_DIM),
                                     lambda b, g, h, part=part: (b, ctx_chunk, part * n_heads + h)))
    in_specs.append(pl.BlockSpec(
        (1, 1, bq, NA_K_ROWS * GRID_W),
        lambda b, g, h: (jnp.where(g == 0, 0, jnp.where(g == n_groups - 1, 2, 1)), h, 0, 0)))
    n_in = 2 * (NA_N_CHUNKS + 1)
    return pl.pallas_call(
        functools.partial(_na_body, scale=HEAD_DIM ** -0.5),
        grid=(B, n_groups, n_heads), in_specs=in_specs,
        out_specs=pl.BlockSpec((1, bq, HEAD_DIM), lambda b, g, h: (b, g, h)),
        out_shape=jax.ShapeDtypeStruct((B, seq, n_heads * HEAD_DIM), BF16),
        compiler_params=_cparams(3), name="na_attention")(qkv, *([qkv] * n_in), bias_tab)


def _softplus(z):
    return jnp.maximum(z, 0.0) + jnp.log(1.0 + jnp.exp(-jnp.abs(z)))


def _lru_gates_body(x_ref, hp_ref, hn_ref, cw_ref, cb_ref, w_ref, ba_ref, bx_ref, lam_ref,
                    af_ref, bf_ref, ab_ref, bb_ref, xs_ref, *, bpb, lat_blocks, n_blocks_lru):
    i = pl.program_id(0)
    j = i % bpb
    first = jnp.logical_or(j == 0, j == lat_blocks)
    last = jnp.logical_or(j == lat_blocks - 1, j == bpb - 1)
    x = x_ref[...]
    xs_ref[SUBLANES:SUBLANES + ROW_BLOCK, :] = x
    xs_ref[0:SUBLANES, :] = jnp.where(first, 0.0, hp_ref[...])
    xs_ref[SUBLANES + ROW_BLOCK:, :] = jnp.where(last, 0.0, hn_ref[...])
    xc = cb_ref[...] + cw_ref[2:3, :] * x
    for tap, off in ((0, -2), (1, -1), (3, 1)):
        xc = xc + cw_ref[tap:tap + 1, :] * xs_ref[SUBLANES + off:SUBLANES + off + ROW_BLOCK, :]
    xcb = xc.astype(BF16)
    zs = [jnp.dot(xcb[:, n * LANES:(n + 1) * LANES], w_ref[n].astype(BF16), preferred_element_type=F32)
          for n in range(n_blocks_lru)]
    for d, (a_ref, b_ref) in enumerate(((af_ref, bf_ref), (ab_ref, bb_ref))):
        za = jnp.concatenate([z[:, (2 * d) * LANES:(2 * d + 1) * LANES] for z in zs], axis=1)
        zx = jnp.concatenate([z[:, (2 * d + 1) * LANES:(2 * d + 2) * LANES] for z in zs], axis=1)
        r = jax.nn.sigmoid(za + ba_ref[d:d + 1, :])
        ig = jax.nn.sigmoid(zx + bx_ref[d:d + 1, :])
        log_a = (-LRU_C) * r * _softplus(-lam_ref[d:d + 1, :])
        a = jnp.exp(log_a)
        a_ref[...] = a
        b_ref[...] = jnp.sqrt(1.0 - jnp.exp(2.0 * log_a)) * (ig * xc)


def _lru_gates(p_lru, conv_w, conv_b, w_cat, b_a, b_x, lam, tok):
    M = p_lru.shape[0]
    W = conv_b.shape[-1]
    nb = W // LANES
    rb8 = ROW_BLOCK // SUBLANES
    n_tiles8 = M // SUBLANES
    row = pl.BlockSpec((ROW_BLOCK, W), lambda i: (i, 0))
    vec2 = pl.BlockSpec((2, W), lambda i: (0, 0))
    return pl.pallas_call(
        functools.partial(_lru_gates_body, bpb=tok.blocks_per_batch, lat_blocks=tok.lat_blocks, n_blocks_lru=nb),
        grid=(M // ROW_BLOCK,),
        in_specs=[row,
                  pl.BlockSpec((SUBLANES, W), lambda i: (jnp.maximum(i * rb8 - 1, 0), 0)),
                  pl.BlockSpec((SUBLANES, W), lambda i: (jnp.minimum((i + 1) * rb8, n_tiles8 - 1), 0)),
                  pl.BlockSpec((LRU_CONV, W), lambda i: (0, 0)),
                  pl.BlockSpec((1, W), lambda i: (0, 0)),
                  pl.BlockSpec((nb, LANES, 4 * LANES), lambda i: (0, 0, 0)),
                  vec2, vec2, vec2],
        out_specs=[row] * 4,
        out_shape=[jax.ShapeDtypeStruct((M, W), F32)] * 4,
        scratch_shapes=[pltpu.VMEM((ROW_BLOCK + 2 * SUBLANES, W), F32)],
        compiler_params=_cparams(1), name="lru_gates")(
            p_lru, p_lru, p_lru, conv_w, conv_b.reshape(1, W), w_cat, b_a, b_x, lam)


LRU_UNROLL = 8


def _lru_scan_body(af_ref, bf_ref, ab_ref, bb_ref, hf_ref, hb_ref, carry_ref):
    @pl.when(pl.program_id(1) == 0)
    def _():
        carry_ref[...] = jnp.zeros(carry_ref.shape, F32)

    n = af_ref.shape[1]

    def step(t, hs):
        hf, hb = hs
        tb = n - 1 - t
        hf = af_ref[0, t] * hf + bf_ref[0, t]
        hb = ab_ref[0, tb] * hb + bb_ref[0, tb]
        hf_ref[0, t] = hf
        hb_ref[0, tb] = hb
        return hf, hb

    hf, hb = lax.fori_loop(0, n, step, (carry_ref[0], carry_ref[1]), unroll=LRU_UNROLL)
    carry_ref[0] = hf
    carry_ref[1] = hb


def _lru_scan(a_f, b_f, a_b, b_b, tok):
    B, T, R, _ = a_f.shape
    lat = tok.lat_blocks
    n_chunks = tok.blocks_per_batch

    def fwd(b, k):
        return (b, jnp.where(k == 0, lat, k - 1), 0, 0)

    def bwd(b, k):
        return (b, jnp.where(k == 0, lat, lat - k), 0, 0)

    blk = (1, ROW_BLOCK, R, LANES)
    return pl.pallas_call(
        _lru_scan_body, grid=(B, n_chunks),
        in_specs=[pl.BlockSpec(blk, fwd), pl.BlockSpec(blk, fwd), pl.BlockSpec(blk, bwd), pl.BlockSpec(blk, bwd)],
        out_specs=[pl.BlockSpec(blk, fwd), pl.BlockSpec(blk, bwd)],
        out_shape=[jax.ShapeDtypeStruct(a_f.shape, F32)] * 2,
        scratch_shapes=[pltpu.VMEM((2, R, LANES), F32)],
        compiler_params=_cparams(2), name="lru_scan")(a_f, b_f, a_b, b_b)


def _gelu_tanh(x):
    return 0.5 * x * (1.0 + jnp.tanh(math.sqrt(2.0 / math.pi) * (x + 0.044715 * (x * x * x))))


def _lru_out_body(hf_ref, hb_ref, g_ref, y_ref):
    y_ref[...] = ((hf_ref[...] + hb_ref[...]) * _gelu_tanh(g_ref[...])).astype(y_ref.dtype)


def _lru_out(hf, hb, p_lru):
    M, W = hf.shape
    row = pl.BlockSpec((ROW_BLOCK, W), lambda i: (i, 0))
    return pl.pallas_call(
        _lru_out_body, grid=(M // ROW_BLOCK,),
        in_specs=[row, row, pl.BlockSpec((ROW_BLOCK, W), lambda i: (i, 1))],
        out_specs=row, out_shape=jax.ShapeDtypeStruct((M, W), BF16),
        compiler_params=_cparams(1), name="lru_out")(hf, hb, p_lru)


def _merge_body(*refs, has_prev):
    if has_prev:
        h_ref, y_ref, wg_ref, wb_ref, bg_ref, prev_ref, o_ref, wg_bf, wb_bf = refs
    else:
        h_ref, y_ref, wg_ref, wb_ref, bg_ref, o_ref, wg_bf, wb_bf = refs

    @pl.when(pl.program_id(1) == 0)
    def _():
        wg_bf[...] = wg_ref[...].astype(BF16)
        wb_bf[...] = wb_ref[...].astype(BF16)

    gate = jax.nn.sigmoid(jnp.dot(h_ref[...], wg_bf[...], preferred_element_type=F32) + bg_ref[0])
    term = gate * jnp.dot(y_ref[...], wb_bf[...], preferred_element_type=F32)
    if has_prev:
        term = term + prev_ref[...]
    o_ref[...] = term.astype(o_ref.dtype)


def _merge_branch(h, y, w_gate, b_gate, w_branch, prev, *, layer, branch, bm, bn, out_dtype):
    M, D = h.shape
    kb = y.shape[1]
    in_specs = [pl.BlockSpec((bm, D), lambda n, m: (m, 0)),
                pl.BlockSpec((bm, kb), lambda n, m: (m, 0)),
                pl.BlockSpec((None, None, D, bn), lambda n, m: (layer, branch, 0, n)),
                pl.BlockSpec((None, None, kb, bn), lambda n, m: (layer, branch, 0, n)),
                pl.BlockSpec((None, 1, 1, bn), lambda n, m: (layer, branch, 0, n))]
    args = [h, y, w_gate, w_branch, b_gate.reshape(b_gate.shape[0], b_gate.shape[1], 1, D)]
    if prev is not None:
        in_specs.append(pl.BlockSpec((bm, bn), lambda n, m: (m, n)))
        args.append(prev)
    return pl.pallas_call(
        functools.partial(_merge_body, has_prev=prev is not None),
        grid=(D // bn, M // bm), in_specs=in_specs,
        out_specs=pl.BlockSpec((bm, bn), lambda n, m: (m, n)),
        out_shape=jax.ShapeDtypeStruct((M, D), out_dtype),
        scratch_shapes=[pltpu.VMEM((D, bn), BF16), pltpu.VMEM((kb, bn), BF16)],
        compiler_params=_cparams(2), name=f"merge_branch{branch}")(*args)


ROUTER_BLOCK = 512
EXPERT_BLOCK = 512


def _router_body(h_ref, wr_ref, rb_ref, tri_ref, ids_ref, wts_ref, cnt_ref, carry_ref, *, n_exp):
    @pl.when(pl.program_id(0) == 0)
    def _():
        carry_ref[...] = jnp.zeros(carry_ref.shape, F32)

    per = n_exp // N_GROUPS
    logits = lax.dot_general(wr_ref[...], h_ref[...], (((1,), (1,)), ((), ())), preferred_element_type=F32)
    scores = jax.nn.sigmoid(logits)
    sel = scores + rb_ref[:, :1]
    sel_r = [sel[e:e + 1, :] for e in range(n_exp)]
    sc_r = [scores[e:e + 1, :] for e in range(n_exp)]
    gs = []
    for g in range(N_GROUPS):
        rows = sel_r[g * per:(g + 1) * per]
        best = None
        for a in range(per):
            for b in range(a + 1, per):
                ps = rows[a] + rows[b]
                best = ps if best is None else jnp.maximum(best, ps)
        gs.append(best)
    bestg = jnp.full(gs[0].shape, N_GROUPS - 1, I32)
    run = gs[N_GROUPS - 1]
    for g in range(N_GROUPS - 2, -1, -1):
        take = gs[g] >= run
        bestg = jnp.where(take, g, bestg)
        run = jnp.where(take, gs[g], run)
    v = [sel_r[i] for i in range(per)]
    sc = [sc_r[i] for i in range(per)]
    for g in range(1, N_GROUPS):
        isg = bestg == g
        v = [jnp.where(isg, sel_r[g * per + i], v[i]) for i in range(per)]
        sc = [jnp.where(isg, sc_r[g * per + i], sc[i]) for i in range(per)]

    def first_argmax(vals):
        idx = jnp.full(vals[0].shape, per - 1, I32)
        mx = vals[per - 1]
        sv = sc[per - 1]
        for i in range(per - 2, -1, -1):
            take = vals[i] >= mx
            idx = jnp.where(take, i, idx)
            mx = jnp.where(take, vals[i], mx)
            sv = jnp.where(take, sc[i], sv)
        return idx, sv

    i1, s1 = first_argmax(v)
    v2 = [jnp.where(i1 == i, -jnp.inf, v[i]) for i in range(per)]
    i2, s2 = first_argmax(v2)
    denom = s1 + s2
    e0 = bestg * per + i1
    e1 = bestg * per + i2
    eid = lax.broadcasted_iota(I32, scores.shape, 0)
    oh0 = (eid == e0).astype(F32)
    oh1 = (eid == e1).astype(F32)
    oh = oh0 + oh1
    prefix = jnp.dot(oh.astype(BF16), tri_ref[...], preferred_element_type=F32) + carry_ref[:, :1]
    r0 = jnp.sum(oh0 * prefix, axis=0, keepdims=True)
    r1 = jnp.sum(oh1 * prefix, axis=0, keepdims=True)
    new_carry = carry_ref[:, :1] + jnp.sum(oh, axis=1, keepdims=True)
    carry_ref[...] = jnp.broadcast_to(new_carry, carry_ref.shape)
    cnt_ref[...] = jnp.broadcast_to(new_carry, cnt_ref.shape)
    ids_ref[...] = jnp.zeros(ids_ref.shape, I32)
    wts_ref[...] = jnp.zeros(wts_ref.shape, F32)
    for r, val in enumerate((e0, e1, r0.astype(I32), r1.astype(I32))):
        ids_ref[r:r + 1, :] = val
    wts_ref[0:1, :] = s1 / denom
    wts_ref[1:2, :] = s2 / denom


def _router(h2, w_router, router_bias):
    M, D = h2.shape
    n_exp = w_router.shape[1]
    bm = ROUTER_BLOCK
    tri = jnp.asarray(np.triu(np.ones((bm, bm), np.float32), k=1), BF16)
    wr_t = w_router.T.astype(BF16)
    rb = jnp.broadcast_to(router_bias.astype(F32)[:, None], (n_exp, LANES))
    return pl.pallas_call(
        functools.partial(_router_body, n_exp=n_exp),
        grid=(M // bm,),
        in_specs=[pl.BlockSpec((bm, D), lambda i: (i, 0)),
                  pl.BlockSpec((n_exp, D), lambda i: (0, 0)),
                  pl.BlockSpec((n_exp, LANES), lambda i: (0, 0)),
                  pl.BlockSpec((bm, bm), lambda i: (0, 0))],
        out_specs=[pl.BlockSpec((SUBLANES, bm), lambda i: (0, i)),
                   pl.BlockSpec((SUBLANES, bm), lambda i: (0, i)),
                   pl.BlockSpec((n_exp, LANES), lambda i: (0, 0))],
        out_shape=[jax.ShapeDtypeStruct((SUBLANES, M), I32), jax.ShapeDtypeStruct((SUBLANES, M), F32),
                   jax.ShapeDtypeStruct((n_exp, LANES), F32)],
        scratch_shapes=[pltpu.VMEM((n_exp, LANES), F32)],
        compiler_params=_cparams(1), name="moe_router")(h2, wr_t, rb, tri)


DISPATCH_ROWS = 256


def _dispatch_body(src_ref, h_hbm, xs_ref, sem):
    def copy(t):
        return pltpu.make_async_copy(h_hbm.at[pl.ds(src_ref[0, 0, t], 1)], xs_ref.at[pl.ds(t, 1)], sem)

    def start(t, c):
        copy(t).start()
        return c

    def wait(t, c):
        copy(t).wait()
        return c

    lax.fori_loop(0, DISPATCH_ROWS, start, 0)
    lax.fori_loop(0, DISPATCH_ROWS, wait, 0)


def _dispatch(h3, src, n_slots):
    slab = h3.shape[1:]
    return pl.pallas_call(
        _dispatch_body, grid=(n_slots // DISPATCH_ROWS,),
        in_specs=[pl.BlockSpec((1, 1, DISPATCH_ROWS), lambda i: (i, 0, 0), memory_space=pltpu.SMEM),
                  pl.BlockSpec(memory_space=pl.ANY)],
        out_specs=pl.BlockSpec((DISPATCH_ROWS,) + slab, lambda i: (i, 0, 0)),
        out_shape=jax.ShapeDtypeStruct((n_slots,) + slab, h3.dtype),
        scratch_shapes=[pltpu.SemaphoreType.DMA(())],
        compiler_params=_cparams(1), name="moe_dispatch")(src, h3)


def _expert_up_body(be_ref, nu_ref, x_ref, wg_ref, wu_ref, o_ref, wg_bf, wu_bf):
    i = pl.program_id(1)
    fresh = jnp.logical_or(i == 0, be_ref[i] != be_ref[jnp.maximum(i - 1, 0)])

    @pl.when(jnp.logical_and(fresh, i < nu_ref[0]))
    def _():
        wg_bf[...] = wg_ref[...].astype(BF16)
        wu_bf[...] = wu_ref[...].astype(BF16)

    @pl.when(i < nu_ref[0])
    def _():
        x = x_ref[...]
        g = jnp.dot(x, wg_bf[...], preferred_element_type=F32)
        u = jnp.dot(x, wu_bf[...], preferred_element_type=F32)
        o_ref[...] = (g * jax.nn.sigmoid(g) * u).astype(o_ref.dtype)

    @pl.when(i >= nu_ref[0])
    def _():
        o_ref[...] = jnp.zeros(o_ref.shape, o_ref.dtype)


def _expert_up(xs, w_gate, w_up, blk_expert, n_used, *, layer, bn):
    P, D = xs.shape
    de = w_gate.shape[-1]
    nblk = P // EXPERT_BLOCK

    def xmap(n, i, be, nu):
        return (jnp.minimum(i, nu[0] - 1), 0)

    def wmap(n, i, be, nu):
        return (layer, be[i], 0, n)

    return pl.pallas_call(
        _expert_up_body,
        grid_spec=pltpu.PrefetchScalarGridSpec(
            num_scalar_prefetch=2, grid=(de // bn, nblk),
            in_specs=[pl.BlockSpec((EXPERT_BLOCK, D), xmap),
                      pl.BlockSpec((None, None, D, bn), wmap),
                      pl.BlockSpec((None, None, D, bn), wmap)],
            out_specs=pl.BlockSpec((EXPERT_BLOCK, bn), lambda n, i, be, nu: (i, n)),
            scratch_shapes=[pltpu.VMEM((D, bn), BF16), pltpu.VMEM((D, bn), BF16)]),
        out_shape=jax.ShapeDtypeStruct((P, de), BF16),
        compiler_params=_cparams(2), name="moe_expert_up")(blk_expert, n_used, xs, w_gate, w_up)


def _expert_down_body(be_ref, nu_ref, x_ref, wd_ref, o_ref, wd_bf):
    i = pl.program_id(1)
    fresh = jnp.logical_or(i == 0, be_ref[i] != be_ref[jnp.maximum(i - 1, 0)])

    @pl.when(jnp.logical_and(fresh, i < nu_ref[0]))
    def _():
        wd_bf[...] = wd_ref[...].astype(BF16)

    @pl.when(i < nu_ref[0])
    def _():
        o_ref[...] = jnp.dot(x_ref[...], wd_bf[...], preferred_element_type=F32).astype(o_ref.dtype)

    @pl.when(i >= nu_ref[0])
    def _():
        o_ref[...] = jnp.zeros(o_ref.shape, o_ref.dtype)


def _expert_down(he, w_down, blk_expert, n_used, *, layer, bn):
    P, de = he.shape
    D = w_down.shape[-1]
    nblk = P // EXPERT_BLOCK
    return pl.pallas_call(
        _expert_down_body,
        grid_spec=pltpu.PrefetchScalarGridSpec(
            num_scalar_prefetch=2, grid=(D // bn, nblk),
            in_specs=[pl.BlockSpec((EXPERT_BLOCK, de), lambda n, i, be, nu: (jnp.minimum(i, nu[0] - 1), 0)),
                      pl.BlockSpec((None, None, de, bn), lambda n, i, be, nu: (layer, be[i], 0, n))],
            out_specs=pl.BlockSpec((EXPERT_BLOCK, bn), lambda n, i, be, nu: (i, n)),
            scratch_shapes=[pltpu.VMEM((de, bn), BF16)]),
        out_shape=jax.ShapeDtypeStruct((P, D), F32),
        compiler_params=_cparams(2), name="moe_expert_down")(blk_expert, n_used, he, w_down)


COMBINE_UNROLL = 8


def _combine_body(pos_ref, w_ref, o_hbm, y_ref, buf, sem):
    def copy(t, k):
        return pltpu.make_async_copy(o_hbm.at[pl.ds(pos_ref[0, k, t], 1)], buf.at[k, pl.ds(t, 1)], sem)

    def start(t, c):
        copy(t, 0).start()
        copy(t, 1).start()
        return c

    def wait(t, c):
        copy(t, 0).wait()
        copy(t, 1).wait()
        return c

    def mix(t, c):
        y_ref[t] = w_ref[0, 0, t] * buf[0, t] + w_ref[0, 1, t] * buf[1, t]
        return c

    lax.fori_loop(0, DISPATCH_ROWS, start, 0)
    lax.fori_loop(0, DISPATCH_ROWS, wait, 0)
    lax.fori_loop(0, DISPATCH_ROWS, mix, 0, unroll=COMBINE_UNROLL)


def _combine(o3, pos, wts, n_rows):
    slab = o3.shape[1:]
    smem = pl.BlockSpec((1, 2, DISPATCH_ROWS), lambda i: (i, 0, 0), memory_space=pltpu.SMEM)
    return pl.pallas_call(
        _combine_body, grid=(n_rows // DISPATCH_ROWS,),
        in_specs=[smem, smem, pl.BlockSpec(memory_space=pl.ANY)],
        out_specs=pl.BlockSpec((DISPATCH_ROWS,) + slab, lambda i: (i, 0, 0)),
        out_shape=jax.ShapeDtypeStruct((n_rows,) + slab, F32),
        scratch_shapes=[pltpu.VMEM((2, DISPATCH_ROWS) + slab, F32), pltpu.SemaphoreType.DMA(())],
        compiler_params=_cparams(1), name="moe_combine")(pos, wts, o3)


def _moe(h2, w_router, router_bias, w_gate, w_up, w_down, *, layer):
    M, D = h2.shape
    n_exp = w_router.shape[1]
    ids, wts, cnt = _router(h2, w_router, router_bias)
    counts = cnt[:, 0].astype(I32)
    padded = (counts + EXPERT_BLOCK - 1) // EXPERT_BLOCK * EXPERT_BLOCK
    ends = jnp.cumsum(padded)
    offs = ends - padded
    n_slots = (2 * M + n_exp * (EXPERT_BLOCK - 1)) // EXPERT_BLOCK * EXPERT_BLOCK
    nblk = n_slots // EXPERT_BLOCK
    pos0 = offs[ids[0]] + ids[2]
    pos1 = offs[ids[1]] + ids[3]
    pos = jnp.stack([pos0.reshape(-1, DISPATCH_ROWS), pos1.reshape(-1, DISPATCH_ROWS)], axis=1)
    wpair = jnp.stack([wts[0].reshape(-1, DISPATCH_ROWS), wts[1].reshape(-1, DISPATCH_ROWS)], axis=1)
    blk_start = jnp.arange(nblk, dtype=I32) * EXPERT_BLOCK
    blk_expert = jnp.minimum(jnp.sum((ends[None, :] <= blk_start[:, None]).astype(I32), axis=1), n_exp - 1)
    n_used = (ends[-1] // EXPERT_BLOCK).astype(I32).reshape(1)
    tok_ids = jnp.arange(M, dtype=I32)
    src = jnp.zeros((n_slots,), I32).at[jnp.concatenate([pos0, pos1])].set(jnp.concatenate([tok_ids, tok_ids]))
    xs = _dispatch(h2.reshape(M, D // LANES, LANES), src.reshape(-1, 1, DISPATCH_ROWS), n_slots).reshape(n_slots, D)
    he = _expert_up(xs, w_gate, w_up, blk_expert, n_used, layer=layer, bn=_pick(w_gate.shape[-1], (256, 128)))
    o = _expert_down(he, w_down, blk_expert, n_used, layer=layer, bn=_pick(D, (2048, 1024, 512)))
    return _combine(o.reshape(n_slots, D // LANES, LANES), pos, wpair, M).reshape(M, D)


class _Tok:
    def __init__(self, n_batch, seq, ctx):
        self.n_batch, self.seq, self.ctx = n_batch, seq, ctx
        self.t_all = seq + ctx
        self.lat_blocks = seq // ROW_BLOCK
        self.blocks_per_batch = self.t_all // ROW_BLOCK


def _rope_tables(seq, ctx):
    t = jnp.arange(seq, dtype=I32)
    row = (t // GRID_W).astype(F32)[:, None]
    col = (t % GRID_W).astype(F32)[:, None]

    def cs(rot_dim):
        n_freq = rot_dim // 4
        freqs = ROPE_THETA ** (-jnp.arange(n_freq, dtype=F32) / n_freq)
        ang = jnp.concatenate([row * freqs, col * freqs], axis=-1)
        cos = jnp.concatenate([jnp.cos(ang), jnp.ones((ctx, rot_dim // 2), F32)], axis=0)
        sin = jnp.concatenate([jnp.sin(ang), jnp.zeros((ctx, rot_dim // 2), F32)], axis=0)
        return cos, sin

    cos, sin = cs(HEAD_DIM)
    rope128 = (jnp.concatenate([cos, cos], axis=1), jnp.concatenate([-sin, sin], axis=1))
    cos, sin = cs(MLA_ROPE)
    z32 = jnp.zeros_like(sin)
    z64 = jnp.zeros((seq + ctx, LANES - MLA_ROPE), F32)
    rope64 = (jnp.concatenate([cos, cos, z64], axis=1), jnp.concatenate([sin, z32, z64], axis=1),
              jnp.concatenate([z32, sin, z64], axis=1))
    return rope128, rope64


def _q_rope_epilogue(acc, c_ref, slo_ref, shi_ref, *, scale):
    nope = acc[:, :MLA_NOPE] * scale
    rope = _rope64(acc[:, MLA_NOPE:], c_ref[...], slo_ref[...], shi_ref[...]) * scale
    return jnp.concatenate([nope, rope], axis=1)


def kernel(x, c, ctx, c_ctx, w_mod, b_mod, norm_mix, norm_ffn, w_in, mla_q_norm, mla_w_uq, mla_kv_norm,
           mla_w_ukv, gqa_q_norm, gqa_k_norm, na_rpb, lru_conv_w, lru_conv_b, lru_w_a, lru_b_a, lru_w_x,
           lru_b_x, lru_lambda, w_branch_gate, b_branch_gate, w_branch, w_out, w_router, router_bias,
           w_exp_gate, w_exp_up, w_exp_down, final_norm):
    B, S, D = x.shape
    CTX = ctx.shape[1]
    depth = w_in.shape[0]
    tok = _Tok(B, S, CTX)
    T = tok.t_all
    M = B * T
    bw = D // 4
    n_heads = bw // HEAD_DIM
    ql, kvl = mla_q_norm.shape[-1], mla_kv_norm.shape[-1]
    d_in = w_in.shape[-1]
    n_kv = (d_in - (ql + kvl + MLA_ROPE + bw + 3 * bw + 2 * bw)) // (2 * HEAD_DIM)
    q_per_kv = n_heads // n_kv
    n_lru_blocks = lru_w_a.shape[2]
    assert bw // n_lru_blocks == LANES and S % ROW_BLOCK == 0 and CTX == ROW_BLOCK and S % (GRID_W * NA_Q_ROWS) == 0
    mla_dq = 2 * LANES
    mla_scale = (MLA_NOPE + MLA_ROPE) ** -0.5 * LOG2E
    bm = _pick(M, (768, 512, 256))
    bm_tok = _pick(T, (768, 256))

    rope128, rope64 = _rope_tables(S, CTX)
    xa = jnp.concatenate([x, ctx], axis=1).reshape(M, D)
    cc = jnp.zeros((MOD_ROWS, D), F32).at[:B].set(c).at[B].set(c_ctx)
    silu_c = (cc * jax.nn.sigmoid(cc)).astype(BF16)

    o_mla = ql + kvl
    o_gqa = o_mla + MLA_ROPE
    w_gqa = bw + 2 * n_kv * HEAD_DIM
    o_na = o_gqa + w_gqa
    o_lru = o_na + 3 * bw

    y_ffn = mod_prev = None
    for layer in range(depth):
        need_ctx = layer < depth - 1
        modv = _matmul(silu_c, w_mod, w_prefix=(layer,), n_cols=N_MOD * D, bm=MOD_ROWS,
                       bn=_pick(N_MOD * D, (512, 256)), out_dtype=F32,
                       epilogue=lambda acc, b_ref: acc + b_ref[0],
                       extras=[(b_mod.reshape(depth, 1, N_MOD * D), (1, 1, _pick(N_MOD * D, (512, 256))),
                                lambda n, m, layer=layer: (layer, 0, n))],
                       name="mod_proj")
        modv = modv[:B + 1].reshape(B + 1, N_MOD, D)

        xa, h = _resid_norm(xa, y_ffn, mod_prev, modv, norm_mix[layer], gate_idx=5, shift_idx=0, scale_idx=1,
                            out_dtype=BF16, tok=tok, name="norm_mix")
        w_l = w_in[layer]
        cqkv = _matmul(h, w_l[:, :o_mla], n_cols=o_mla, bm=bm, bn=_pick(o_mla, (512, 384, 256, 128)),
                       out_dtype=F32, name="in_mla")
        w_kr = jnp.pad(w_l[:, o_mla:o_gqa], ((0, 0), (0, LANES - MLA_ROPE)))
        kr = _matmul(h, w_kr, n_cols=LANES, bm=bm, bn=LANES, out_dtype=F32, name="in_kr")
        p_gqa = _matmul(h, w_l[:, o_gqa:o_na], n_cols=w_gqa, bm=bm, bn=_pick(w_gqa, (512, 384, 256, 128)),
                        out_dtype=F32, name="in_gqa")
        p_na = _matmul(h, w_l[:, o_na:o_lru], n_cols=3 * bw, bm=bm, bn=_pick(3 * bw, (512, 256)),
                       out_dtype=BF16, name="in_na")
        p_lru = _matmul(h, w_l[:, o_lru:], n_cols=2 * bw, bm=bm, bn=_pick(2 * bw, (512, 256)),
                        out_dtype=F32, name="in_lru")

        qn, kvn, krr = _mla_prep(cqkv, kr, mla_q_norm[layer], mla_kv_norm[layer], rope64, tok)
        w_uq = jnp.pad(mla_w_uq[layer].reshape(ql, n_heads, MLA_NOPE + MLA_ROPE),
                       ((0, 0), (0, 0), (0, mla_dq - MLA_NOPE - MLA_ROPE))).reshape(ql, n_heads * mla_dq)
        tpb = T // bm_tok
        tab_spec = ((bm_tok, LANES), lambda n, m: (m % tpb, 0))
        q_mla = _matmul(qn, w_uq, n_cols=n_heads * mla_dq, bm=bm_tok, bn=mla_dq, out_dtype=BF16,
                        epilogue=functools.partial(_q_rope_epilogue, scale=mla_scale),
                        extras=[(t, *tab_spec) for t in rope64], name="mla_uq")
        kv_mla = _matmul(kvn, mla_w_ukv, w_prefix=(layer,), n_cols=n_heads * 2 * HEAD_DIM, bm=bm,
                         bn=_pick(n_heads * 2 * HEAD_DIM, (512, 256)), out_dtype=BF16, name="mla_ukv")
        bq = _pick(S, (512, 256))
        bk = _pick(T, (768, 256))
        ctx_blk = S // ROW_BLOCK
        q3, kv3, kr3 = (a.reshape(B, T, -1) for a in (q_mla, kv_mla, krr))
        mla_args = dict(n_heads=n_heads, dq=mla_dq, k_lane=lambda h: 2 * h, v_lane=lambda h: 2 * h + 1)
        ya = _flash(q3, kv3, kv3, kr3, bq=bq, q_off=0, n_q=S // bq, bk=bk, k_off=0, n_k=T // bk,
                    name="mla_attn", **mla_args)
        q_g, k_g, v_g = (a.reshape(B, T, -1) for a in
                         _gqa_prep(p_gqa, gqa_q_norm[layer], gqa_k_norm[layer], rope128, n_heads, n_kv, tok))
        gqa_args = dict(n_heads=n_heads, dq=HEAD_DIM, k_lane=lambda h: h // q_per_kv, v_lane=lambda h: h // q_per_kv)
        yb = _flash(q_g, k_g, v_g, None, bq=bq, q_off=0, n_q=S // bq, bk=bk, k_off=0, n_k=T // bk,
                    name="gqa_attn", **gqa_args)
        na3 = p_na.reshape(B, T, 3 * bw)
        yc = _na_attention(na3, _na_bias_tables(na_rpb[layer], S // GRID_W), n_heads=n_heads, seq=S,
                           ctx_chunk=S // NA_CHUNK)
        if need_ctx:
            ctx_q = dict(bq=ROW_BLOCK, q_off=ctx_blk, n_q=1, bk=ROW_BLOCK, k_off=ctx_blk, n_k=1)
            ya_c = _flash(q3, kv3, kv3, kr3, name="mla_attn_ctx", **ctx_q, **mla_args)
            yb_c = _flash(q_g, k_g, v_g, None, name="gqa_attn_ctx", **ctx_q, **gqa_args)
            yc_c = _flash(na3, na3, na3, None, name="na_attn_ctx", n_heads=n_heads, dq=HEAD_DIM,
                          k_lane=lambda h: n_heads + h, v_lane=lambda h: 2 * n_heads + h,
                          s_scale=HEAD_DIM ** -0.5 * LOG2E, **ctx_q)
        else:
            ya_c = yb_c = yc_c = jnp.zeros((B, CTX, bw), BF16)
        ya, yb, yc = (jnp.concatenate([l_, c_], axis=1).reshape(M, bw)
                      for l_, c_ in ((ya, ya_c), (yb, yb_c), (yc, yc_c)))
        w_cat = jnp.concatenate([lru_w_a[layer, 0], lru_w_x[layer, 0], lru_w_a[layer, 1], lru_w_x[layer, 1]],
                                axis=-1)
        gates = _lru_gates(p_lru, lru_conv_w[layer], lru_conv_b[layer], w_cat, lru_b_a[layer], lru_b_x[layer],
                           lru_lambda[layer], tok)
        hf, hb = _lru_scan(*(g.reshape(B, T, n_lru_blocks, LANES) for g in gates), tok)
        yd = _lru_out(hf.reshape(M, bw), hb.reshape(M, bw), p_lru)

        bn_merge = _pick(D, (512, 256))
        bm_merge = _pick(M, (512, 256))
        merged = None
        for i_br, y_br in enumerate((ya, yb, yc, yd)):
            merged = _merge_branch(h, y_br, w_branch_gate, b_branch_gate, w_branch, merged, layer=layer,
                                   branch=i_br, bm=bm_merge, bn=bn_merge,
                                   out_dtype=BF16 if i_br == 3 else F32)
        y = _matmul(merged, w_out, w_prefix=(layer,), n_cols=D, bm=bm, bn=_pick(D, (512, 256)), out_dtype=F32,
                    name="out_proj")
        xa, h2 = _resid_norm(xa, y, modv, modv, norm_ffn[layer], gate_idx=2, shift_idx=3, scale_idx=4,
                             out_dtype=BF16, tok=tok, name="resid_norm_ffn")
        y_ffn = _moe(h2, w_router, router_bias, w_exp_gate, w_exp_up, w_exp_down, layer=layer)
        mod_prev = modv

    _, out = _resid_norm(xa, y_ffn, mod_prev, None, final_norm, gate_idx=5, shift_idx=None, scale_idx=None,
                         out_dtype=F32, tok=tok, name="final_norm")
    return out.reshape(B, T, D)[:, :S]
```

```python
import functools
import math

import numpy as np
import jax
import jax.numpy as jnp
from jax import lax
from jax.experimental import pallas as pl
from jax.experimental.pallas import tpu as pltpu

F32 = jnp.float32
BF16 = jnp.bfloat16
I32 = jnp.int32

HEAD_DIM = 128
GRID_W = 64
ROPE_THETA = 10000.0
NORM_EPS = 1e-6
MLA_NOPE = 128
MLA_ROPE = 64
NA_WIN_H = 8
NA_WIN_W = 16
LRU_CONV = 4
LRU_C = 8.0
N_GROUPS = 4
N_MOD = 6

LANES = 128
SUBLANES = 8
MOD_ROWS = 16
VMEM_LIMIT_BYTES = 58 * 1024 * 1024
ROW_BLOCK = 256
NEG_BIG = -1e30
LOG2E = math.log2(math.e)


def _pick(n, prefs):
    for p in prefs:
        if n % p == 0:
            return p
    return n


def _pick_bn(n_cols, n_off):
    return _pick(math.gcd(n_cols, n_off) if n_off else n_cols, (512, 256, 128))


def _cparams(n_axes):
    return pltpu.CompilerParams(dimension_semantics=("arbitrary",) * n_axes,
                                vmem_limit_bytes=VMEM_LIMIT_BYTES)


def _mm_body(*refs, n_extra, n_out, epilogue, cache_w):
    a_ref, w_ref = refs[0], refs[1]
    extra = refs[2:2 + n_extra]
    o_refs = refs[2 + n_extra:2 + n_extra + n_out]
    if cache_w:
        wbf_ref = refs[2 + n_extra + n_out]

        @pl.when(pl.program_id(1) == 0)
        def _():
            wbf_ref[...] = w_ref[...].astype(BF16)

        w = wbf_ref[...]
    else:
        w = w_ref[...].astype(BF16)
    acc = jnp.dot(a_ref[...], w, preferred_element_type=F32)
    if epilogue is not None:
        acc = epilogue(acc, *extra)
    outs = acc if n_out > 1 else (acc,)
    for o_ref, val in zip(o_refs, outs):
        o_ref[...] = val.astype(o_ref.dtype)


def _matmul(a, w, *, n_cols, bm, bn, out_dtype, w_prefix=(), n_off=0, epilogue=None, extras=(), n_out=1, name):
    M, K = a.shape
    assert M % bm == 0 and n_cols % bn == 0 and n_off % bn == 0
    nb_off = n_off // bn
    grid = (n_cols // bn, M // bm)
    w_block = (None,) * len(w_prefix) + (K, bn)
    in_specs = [pl.BlockSpec((bm, K), lambda n, m: (m, 0)),
                pl.BlockSpec(w_block, lambda n, m: (*w_prefix, 0, n + nb_off))]
    args = [a, w]
    for arr, blk, imap in extras:
        in_specs.append(pl.BlockSpec(blk, imap))
        args.append(arr)
    cache_w = (w.dtype != BF16) and grid[1] > 1
    scratch = [pltpu.VMEM((K, bn), BF16)] if cache_w else []
    out_spec = pl.BlockSpec((bm, bn), lambda n, m: (m, n))
    out_sds = jax.ShapeDtypeStruct((M, n_cols), out_dtype)
    res = pl.pallas_call(
        functools.partial(_mm_body, n_extra=len(extras), n_out=n_out, epilogue=epilogue, cache_w=cache_w),
        grid=grid, in_specs=in_specs, out_specs=[out_spec] * n_out, out_shape=[out_sds] * n_out,
        scratch_shapes=scratch, compiler_params=_cparams(2), name=name)(*args)
    return res if n_out > 1 else res[0]


def _rms(x, gain):
    return x * lax.rsqrt(jnp.mean(x * x, axis=-1, keepdims=True) + NORM_EPS) * gain


def _resid_norm_body(*refs, has_y, has_mod, gate_idx, shift_idx, scale_idx):
    it = iter(refs)
    x_ref = next(it)
    y_ref, modg_ref = (next(it), next(it)) if has_y else (None, None)
    modn_ref = next(it) if has_mod else None
    gain_ref = next(it)
    xo_ref = next(it) if has_y else None
    h_ref = next(it)
    x = x_ref[...]
    if has_y:
        x = x + modg_ref[0, gate_idx:gate_idx + 1, :] * y_ref[...]
        xo_ref[...] = x
    h = _rms(x, gain_ref[...])
    if has_mod:
        h = h * (1.0 + modn_ref[0, scale_idx:scale_idx + 1, :]) + modn_ref[0, shift_idx:shift_idx + 1, :]
    h_ref[...] = h.astype(h_ref.dtype)


def _mod_row_map(blocks_per_batch, lat_blocks, n_batch):
    def imap(i):
        b = i // blocks_per_batch
        j = i % blocks_per_batch
        return (jnp.where(j < lat_blocks, b, n_batch), 0, 0)
    return imap


def _resid_norm(x, y, mod_gate, mod_norm, gain, *, gate_idx, shift_idx, scale_idx, out_dtype, tok, name):
    M, D = x.shape
    nblk = M // ROW_BLOCK
    row = pl.BlockSpec((ROW_BLOCK, D), lambda i: (i, 0))
    mod_spec = pl.BlockSpec((1, N_MOD, D), _mod_row_map(tok.blocks_per_batch, tok.lat_blocks, tok.n_batch))
    in_specs, args = [row], [x]
    if y is not None:
        in_specs += [row, mod_spec]
        args += [y, mod_gate]
    if mod_norm is not None:
        in_specs.append(mod_spec)
        args.append(mod_norm)
    in_specs.append(pl.BlockSpec((1, D), lambda i: (0, 0)))
    args.append(gain.reshape(1, D))
    out_specs, out_shape = [], []
    if y is not None:
        out_specs.append(row)
        out_shape.append(jax.ShapeDtypeStruct((M, D), F32))
    out_specs.append(row)
    out_shape.append(jax.ShapeDtypeStruct((M, D), out_dtype))
    res = pl.pallas_call(
        functools.partial(_resid_norm_body, has_y=y is not None, has_mod=mod_norm is not None, gate_idx=gate_idx,
                          shift_idx=shift_idx, scale_idx=scale_idx),
        grid=(nblk,), in_specs=in_specs, out_specs=out_specs, out_shape=out_shape,
        compiler_params=_cparams(1), name=name)(*args)
    return res if y is not None else (x, res[0])


def _rope64(x, c, slo, shi):
    return x * c - pltpu.roll(x, 96, axis=1) * slo + pltpu.roll(x, 32, axis=1) * shi


def _rope128(x, c, ss):
    return x * c + pltpu.roll(x, 64, axis=1) * ss


def _mla_prep_body(cqkv_ref, kr_ref, qg_ref, kvg_ref, c_ref, slo_ref, shi_ref, qn_ref, kvn_ref, krr_ref, *, ql):
    cqkv = cqkv_ref[...]
    qn_ref[...] = _rms(cqkv[:, :ql], qg_ref[...]).astype(BF16)
    kvn_ref[...] = _rms(cqkv[:, ql:], kvg_ref[...]).astype(BF16)
    krr_ref[...] = _rope64(kr_ref[...], c_ref[...], slo_ref[...], shi_ref[...]).astype(BF16)


def _mla_prep(cqkv, kr, q_gain, kv_gain, rope64, tok):
    M, W = cqkv.shape
    ql, kvl = q_gain.shape[-1], kv_gain.shape[-1]
    bpb = tok.blocks_per_batch
    tab = pl.BlockSpec((ROW_BLOCK, LANES), lambda i: (i % bpb, 0))
    return pl.pallas_call(
        functools.partial(_mla_prep_body, ql=ql),
        grid=(M // ROW_BLOCK,),
        in_specs=[pl.BlockSpec((ROW_BLOCK, W), lambda i: (i, 0)),
                  pl.BlockSpec((ROW_BLOCK, LANES), lambda i: (i, 0)),
                  pl.BlockSpec((1, ql), lambda i: (0, 0)),
                  pl.BlockSpec((1, kvl), lambda i: (0, 0)), tab, tab, tab],
        out_specs=[pl.BlockSpec((ROW_BLOCK, ql), lambda i: (i, 0)),
                   pl.BlockSpec((ROW_BLOCK, kvl), lambda i: (i, 0)),
                   pl.BlockSpec((ROW_BLOCK, LANES), lambda i: (i, 0))],
        out_shape=[jax.ShapeDtypeStruct((M, ql), BF16), jax.ShapeDtypeStruct((M, kvl), BF16),
                   jax.ShapeDtypeStruct((M, LANES), BF16)],
        compiler_params=_cparams(1), name="mla_prep")(
            cqkv, kr, q_gain.reshape(1, ql), kv_gain.reshape(1, kvl), *rope64)


def _gqa_prep_body(p_ref, qg_ref, kg_ref, c_ref, ss_ref, q_ref, k_ref, v_ref, *, n_q, n_kv, scale):
    c, ss = c_ref[...], ss_ref[...]
    for h in range(n_q):
        xh = p_ref[:, h * HEAD_DIM:(h + 1) * HEAD_DIM]
        q_ref[:, h * HEAD_DIM:(h + 1) * HEAD_DIM] = (_rope128(_rms(xh, qg_ref[...]), c, ss) * scale).astype(BF16)
    off = n_q * HEAD_DIM
    for h in range(n_kv):
        xh = p_ref[:, off + h * HEAD_DIM:off + (h + 1) * HEAD_DIM]
        k_ref[:, h * HEAD_DIM:(h + 1) * HEAD_DIM] = _rope128(_rms(xh, kg_ref[...]), c, ss).astype(BF16)
    off += n_kv * HEAD_DIM
    for h in range(n_kv):
        v_ref[:, 2 * h * HEAD_DIM:(2 * h + 1) * HEAD_DIM] = p_ref[:, off + h * HEAD_DIM:off + (h + 1) * HEAD_DIM].astype(BF16)
        v_ref[:, (2 * h + 1) * HEAD_DIM:(2 * h + 2) * HEAD_DIM] = jnp.ones((ROW_BLOCK, HEAD_DIM), BF16)


def _gqa_prep(p, q_gain, k_gain, rope128, n_q, n_kv, tok):
    M, W = p.shape
    bpb = tok.blocks_per_batch
    tab = pl.BlockSpec((ROW_BLOCK, LANES), lambda i: (i % bpb, 0))
    wq, wk = n_q * HEAD_DIM, n_kv * HEAD_DIM
    return pl.pallas_call(
        functools.partial(_gqa_prep_body, n_q=n_q, n_kv=n_kv, scale=HEAD_DIM ** -0.5 * LOG2E),
        grid=(M // ROW_BLOCK,),
        in_specs=[pl.BlockSpec((ROW_BLOCK, W), lambda i: (i, 0)),
                  pl.BlockSpec((1, HEAD_DIM), lambda i: (0, 0)),
                  pl.BlockSpec((1, HEAD_DIM), lambda i: (0, 0)), tab, tab],
        out_specs=[pl.BlockSpec((ROW_BLOCK, wq), lambda i: (i, 0)),
                   pl.BlockSpec((ROW_BLOCK, wk), lambda i: (i, 0)),
                   pl.BlockSpec((ROW_BLOCK, 2 * wk), lambda i: (i, 0))],
        out_shape=[jax.ShapeDtypeStruct((M, wq), BF16), jax.ShapeDtypeStruct((M, wk), BF16),
                   jax.ShapeDtypeStruct((M, 2 * wk), BF16)],
        compiler_params=_cparams(1), name="gqa_prep")(
            p, q_gain.reshape(1, HEAD_DIM), k_gain.reshape(1, HEAD_DIM), *rope128)


def _lane_fold(x, op, init):
    for j in range(x.shape[1] // LANES):
        init = op(init, x[:, j * LANES:(j + 1) * LANES])
    return init


def _score_pass(q, k_ref, s_ref, *, n_k, bk, s_scale):
    m_lane = jnp.full((q.shape[0], LANES), -jnp.inf, F32)
    for c in range(n_k):
        s = lax.dot_general(q, k_ref[0, c * bk:(c + 1) * bk, :], (((1,), (1,)), ((), ())),
                            preferred_element_type=F32)
        if s_scale is not None:
            s = s * s_scale
        s_ref[c] = s
        m_lane = _lane_fold(s, jnp.maximum, m_lane)
    return jnp.max(m_lane, axis=1, keepdims=True)


def _value_pass(s_ref, m, v_ref, *, n_k, bk, v_has_ones):
    acc = jnp.zeros((s_ref.shape[1], HEAD_DIM + LANES), F32)
    for c in range(n_k):
        p = jnp.exp2(s_ref[c] - m)
        v1 = v_ref[0, c * bk:(c + 1) * bk, :]
        if not v_has_ones:
            v1 = jnp.concatenate([v1, jnp.ones((bk, LANES), BF16)], axis=-1)
        acc = acc + jnp.dot(p.astype(BF16), v1, preferred_element_type=F32)
    return acc[:, :HEAD_DIM] / acc[:, HEAD_DIM:HEAD_DIM + 1]


def _attn_body(q_ref, k_ref, v_ref, o_ref, s_sc, *, v_has_ones, n_k, bk, s_scale):
    m = _score_pass(q_ref[0], k_ref, s_sc, n_k=n_k, bk=bk, s_scale=s_scale)
    o_ref[0] = _value_pass(s_sc, m, v_ref, n_k=n_k, bk=bk, v_has_ones=v_has_ones).astype(o_ref.dtype)


def _attn_pipelined_body(q_ref, k_ref, v_ref, o_ref, sa, sb, ma, mb, *, n_k, bk, s_scale):
    i = pl.program_id(2)

    @pl.when(i == 0)
    def _():
        sb[...] = jnp.zeros(sb.shape, F32)
        mb[...] = jnp.zeros(mb.shape, F32)

    def step(s_w, m_w, s_r, m_r):
        m = _score_pass(q_ref[0], k_ref, s_w, n_k=n_k, bk=bk, s_scale=s_scale)
        m_w[...] = jnp.broadcast_to(m, m_w.shape)
        o_ref[0] = _value_pass(s_r, m_r[:, :1], v_ref, n_k=n_k, bk=bk, v_has_ones=True).astype(o_ref.dtype)

    @pl.when(i % 2 == 0)
    def _():
        step(sa, ma, sb, mb)

    @pl.when(i % 2 == 1)
    def _():
        step(sb, mb, sa, ma)


def _attention(q, k, v, *, n_heads, dq, dk, k_lane, v_lane, v_has_ones, bq, q_off, n_q, bk, k_off, n_k, name,
               pipelined, s_scale=None):
    B = q.shape[0]
    tk = n_k * bk
    dv = HEAD_DIM + LANES if v_has_ones else HEAD_DIM
    out_shape = jax.ShapeDtypeStruct((B, n_q * bq, n_heads * HEAD_DIM), BF16)
    if not pipelined:
        return pl.pallas_call(
            functools.partial(_attn_body, v_has_ones=v_has_ones, n_k=n_k, bk=bk, s_scale=s_scale),
            grid=(B, n_heads, n_q),
            in_specs=[pl.BlockSpec((1, bq, dq), lambda b, h, i: (b, i + q_off, h)),
                      pl.BlockSpec((1, tk, dk), lambda b, h, i: (b, k_off, k_lane(h))),
                      pl.BlockSpec((1, tk, dv), lambda b, h, i: (b, k_off, v_lane(h)))],
            out_specs=pl.BlockSpec((1, bq, HEAD_DIM), lambda b, h, i: (b, i, h)),
            out_shape=out_shape, scratch_shapes=[pltpu.VMEM((n_k, bq, bk), F32)],
            compiler_params=_cparams(3), name=name)(q, k, v)
    assert v_has_ones
    resident = dict(pipeline_mode=pl.Buffered(1))
    return pl.pallas_call(
        functools.partial(_attn_pipelined_body, n_k=n_k, bk=bk, s_scale=s_scale),
        grid=(B, n_heads, n_q + 1),
        in_specs=[pl.BlockSpec((1, bq, dq), lambda b, h, i: (b, jnp.minimum(i, n_q - 1) + q_off, h)),
                  pl.BlockSpec((1, tk, dk), lambda b, h, i: (b, k_off, k_lane(h)), **resident),
                  pl.BlockSpec((1, tk, dv), lambda b, h, i: (b, k_off, v_lane(h)), **resident)],
        out_specs=pl.BlockSpec((1, bq, HEAD_DIM), lambda b, h, i: (b, jnp.maximum(i - 1, 0), h)),
        out_shape=out_shape,
        scratch_shapes=[pltpu.VMEM((n_k, bq, bk), F32), pltpu.VMEM((n_k, bq, bk), F32),
                        pltpu.VMEM((bq, LANES), F32), pltpu.VMEM((bq, LANES), F32)],
        compiler_params=_cparams(3), name=name)(q, k, v)


NA_Q_ROWS = 8
NA_K_ROWS = 16
NA_CHUNK = 256
NA_N_CHUNKS = NA_K_ROWS * GRID_W // NA_CHUNK


def _na_body(*refs, scale):
    q_ref = refs[0]
    k_refs = refs[1:1 + NA_N_CHUNKS + 1]
    v_refs = refs[2 + NA_N_CHUNKS:3 + 2 * NA_N_CHUNKS]
    bias_ref = refs[3 + 2 * NA_N_CHUNKS]
    o_ref = refs[4 + 2 * NA_N_CHUNKS]
    q = q_ref[0]
    parts = [lax.dot_general(q, kr[0], (((1,), (1,)), ((), ())), preferred_element_type=F32) * scale
             for kr in k_refs]
    s_loc = jnp.concatenate(parts[:NA_N_CHUNKS], axis=1) + bias_ref[0, 0]
    s = jnp.concatenate([s_loc, parts[NA_N_CHUNKS]], axis=1)
    m = jnp.max(s, axis=1, keepdims=True)
    p = jnp.exp(s - m)
    l = jnp.sum(p, axis=1, keepdims=True)
    pb = p.astype(BF16)
    acc = None
    for j, vr in enumerate(v_refs):
        t = jnp.dot(pb[:, j * NA_CHUNK:(j + 1) * NA_CHUNK], vr[0], preferred_element_type=F32)
        acc = t if acc is None else acc + t
    o_ref[0] = (acc / l).astype(o_ref.dtype)


def _na_bias_tables(rpb, rows):
    n_groups = rows // NA_Q_ROWS
    n_h = rpb.shape[0]
    n_dr, n_dc = 2 * NA_WIN_H - 1, 2 * NA_WIN_W - 1
    exact = lax.Precision.HIGHEST
    qc = np.arange(GRID_W)[:, None]
    kc = np.arange(GRID_W)[None, :]
    cs = np.clip(qc - NA_WIN_W // 2, 0, GRID_W - NA_WIN_W)
    col_valid = (kc >= cs) & (kc < cs + NA_WIN_W)
    dc = np.clip(kc - qc + (NA_WIN_W - 1), 0, n_dc - 1)
    oh_c = (np.arange(n_dc)[:, None, None] == dc[None]).astype(np.float32).reshape(n_dc, GRID_W * GRID_W)
    cols = jnp.einsum('hrd,dx->hrx', rpb.astype(F32), jnp.asarray(oh_c), precision=exact)
    tabs = []
    j = np.arange(NA_Q_ROWS)[:, None]
    i = np.arange(NA_K_ROWS)[None, :]
    for g in (0, 1, n_groups - 1):
        qr = NA_Q_ROWS * g + j
        kr = np.clip(NA_Q_ROWS * g - NA_WIN_H // 2, 0, rows - NA_K_ROWS) + i
        rs = np.clip(qr - NA_WIN_H // 2, 0, rows - NA_WIN_H)
        row_valid = (kr >= rs) & (kr < rs + NA_WIN_H)
        dr = np.clip(kr - qr + (NA_WIN_H - 1), 0, n_dr - 1)
        oh_r = (np.arange(n_dr)[None, None, :] == dr[:, :, None]).astype(np.float32)
        oh_r = oh_r.reshape(NA_Q_ROWS * NA_K_ROWS, n_dr)
        t = jnp.einsum('pr,hrx->hpx', jnp.asarray(oh_r), cols, precision=exact)
        t = t.reshape(n_h, NA_Q_ROWS, NA_K_ROWS, GRID_W, GRID_W)
        valid = row_valid[:, :, None, None] & col_valid[None, None]
        t = jnp.where(valid[None], t, NEG_BIG).transpose(0, 1, 3, 2, 4)
        tabs.append(t.reshape(n_h, NA_Q_ROWS * GRID_W, NA_K_ROWS * GRID_W))
    return jnp.stack(tabs)


def _na_attention(qkv, bias_tab, *, n_heads, seq, ctx_chunk):
    B = qkv.shape[0]
    rows = seq // GRID_W
    n_groups = rows // NA_Q_ROWS
    bq = NA_Q_ROWS * GRID_W
    max_cb = seq // NA_CHUNK - NA_N_CHUNKS

    def cb(g):
        return jnp.clip(2 * g - 1, 0, max_cb)

    in_specs = [pl.BlockSpec((1, bq, HEAD_DIM), lambda b, g, h: (b, g, h))]
    for part in (1, 2):
        for c in range(NA_N_CHUNKS):
            in_specs.append(pl.BlockSpec((1, NA_CHUNK, HEAD_DIM),
                                         lambda b, g, h, c=c, part=part: (b, cb(g) + c, part * n_heads + h)))
        in_specs.append(pl.BlockSpec((1, NA_CHUNK, HEAD_DIM),
                                     lambda b, g, h, part=part: (b, ctx_chunk, part * n_heads + h)))
    in_specs.append(pl.BlockSpec(
        (1, 1, bq, NA_K_ROWS * GRID_W),
        lambda b, g, h: (jnp.where(g == 0, 0, jnp.where(g == n_groups - 1, 2, 1)), h, 0, 0)))
    n_in = 2 * (NA_N_CHUNKS + 1)
    return pl.pallas_call(
        functools.partial(_na_body, scale=HEAD_DIM ** -0.5),
        grid=(B, n_groups, n_heads), in_specs=in_specs,
        out_specs=pl.BlockSpec((1, bq, HEAD_DIM), lambda b, g, h: (b, g, h)),
        out_shape=jax.ShapeDtypeStruct((B, seq, n_heads * HEAD_DIM), BF16),
        compiler_params=_cparams(3), name="na_attention")(qkv, *([qkv] * n_in), bias_tab)


def _softplus(z):
    return jnp.maximum(z, 0.0) + jnp.log(1.0 + jnp.exp(-jnp.abs(z)))


def _lru_gates_body(x_ref, hp_ref, hn_ref, cw_ref, cb_ref, w_ref, ba_ref, bx_ref, lam_ref,
                    af_ref, bf_ref, ab_ref, bb_ref, xs_ref, *, bpb, lat_blocks, n_blocks_lru):
    i = pl.program_id(0)
    j = i % bpb
    first = jnp.logical_or(j == 0, j == lat_blocks)
    last = jnp.logical_or(j == lat_blocks - 1, j == bpb - 1)
    x = x_ref[...]
    xs_ref[SUBLANES:SUBLANES + ROW_BLOCK, :] = x
    xs_ref[0:SUBLANES, :] = jnp.where(first, 0.0, hp_ref[...])
    xs_ref[SUBLANES + ROW_BLOCK:, :] = jnp.where(last, 0.0, hn_ref[...])
    xc = cb_ref[...] + cw_ref[2:3, :] * x
    for tap, off in ((0, -2), (1, -1), (3, 1)):
        xc = xc + cw_ref[tap:tap + 1, :] * xs_ref[SUBLANES + off:SUBLANES + off + ROW_BLOCK, :]
    xcb = xc.astype(BF16)
    zs = [jnp.dot(xcb[:, n * LANES:(n + 1) * LANES], w_ref[n].astype(BF16), preferred_element_type=F32)
          for n in range(n_blocks_lru)]
    for d, (a_ref, b_ref) in enumerate(((af_ref, bf_ref), (ab_ref, bb_ref))):
        za = jnp.concatenate([z[:, (2 * d) * LANES:(2 * d + 1) * LANES] for z in zs], axis=1)
        zx = jnp.concatenate([z[:, (2 * d + 1) * LANES:(2 * d + 2) * LANES] for z in zs], axis=1)
        r = jax.nn.sigmoid(za + ba_ref[d:d + 1, :])
        ig = jax.nn.sigmoid(zx + bx_ref[d:d + 1, :])
        log_a = (-LRU_C) * r * _softplus(-lam_ref[d:d + 1, :])
        a = jnp.exp(log_a)
        a_ref[...] = a
        b_ref[...] = jnp.sqrt(1.0 - jnp.exp(2.0 * log_a)) * (ig * xc)


def _lru_gates(p_lru, conv_w, conv_b, w_cat, b_a, b_x, lam, tok):
    M = p_lru.shape[0]
    W = conv_b.shape[-1]
    nb = W // LANES
    rb8 = ROW_BLOCK // SUBLANES
    n_tiles8 = M // SUBLANES
    row = pl.BlockSpec((ROW_BLOCK, W), lambda i: (i, 0))
    vec2 = pl.BlockSpec((2, W), lambda i: (0, 0))
    return pl.pallas_call(
        functools.partial(_lru_gates_body, bpb=tok.blocks_per_batch, lat_blocks=tok.lat_blocks, n_blocks_lru=nb),
        grid=(M // ROW_BLOCK,),
        in_specs=[row,
                  pl.BlockSpec((SUBLANES, W), lambda i: (jnp.maximum(i * rb8 - 1, 0), 0)),
                  pl.BlockSpec((SUBLANES, W), lambda i: (jnp.minimum((i + 1) * rb8, n_tiles8 - 1), 0)),
                  pl.BlockSpec((LRU_CONV, W), lambda i: (0, 0)),
                  pl.BlockSpec((1, W), lambda i: (0, 0)),
                  pl.BlockSpec((nb, LANES, 4 * LANES), lambda i: (0, 0, 0)),
                  vec2, vec2, vec2],
        out_specs=[row] * 4,
        out_shape=[jax.ShapeDtypeStruct((M, W), F32)] * 4,
        scratch_shapes=[pltpu.VMEM((ROW_BLOCK + 2 * SUBLANES, W), F32)],
        compiler_params=_cparams(1), name="lru_gates")(
            p_lru, p_lru, p_lru, conv_w, conv_b.reshape(1, W), w_cat, b_a, b_x, lam)


LRU_UNROLL = 8


def _lru_scan_body(af_ref, bf_ref, ab_ref, bb_ref, hf_ref, hb_ref, carry_ref):
    @pl.when(pl.program_id(1) == 0)
    def _():
        carry_ref[...] = jnp.zeros(carry_ref.shape, F32)

    n = af_ref.shape[1]

    def step(t, hs):
        hf, hb = hs
        tb = n - 1 - t
        hf = af_ref[0, t] * hf + bf_ref[0, t]
        hb = ab_ref[0, tb] * hb + bb_ref[0, tb]
        hf_ref[0, t] = hf
        hb_ref[0, tb] = hb
        return hf, hb

    hf, hb = lax.fori_loop(0, n, step, (carry_ref[0], carry_ref[1]), unroll=LRU_UNROLL)
    carry_ref[0] = hf
    carry_ref[1] = hb


def _lru_scan(a_f, b_f, a_b, b_b, tok):
    B, T, R, _ = a_f.shape
    lat = tok.lat_blocks
    n_chunks = tok.blocks_per_batch

    def fwd(b, k):
        return (b, jnp.where(k == 0, lat, k - 1), 0, 0)

    def bwd(b, k):
        return (b, jnp.where(k == 0, lat, lat - k), 0, 0)

    blk = (1, ROW_BLOCK, R, LANES)
    return pl.pallas_call(
        _lru_scan_body, grid=(B, n_chunks),
        in_specs=[pl.BlockSpec(blk, fwd), pl.BlockSpec(blk, fwd), pl.BlockSpec(blk, bwd), pl.BlockSpec(blk, bwd)],
        out_specs=[pl.BlockSpec(blk, fwd), pl.BlockSpec(blk, bwd)],
        out_shape=[jax.ShapeDtypeStruct(a_f.shape, F32)] * 2,
        scratch_shapes=[pltpu.VMEM((2, R, LANES), F32)],
        compiler_params=_cparams(2), name="lru_scan")(a_f, b_f, a_b, b_b)


def _gelu_tanh(x):
    return 0.5 * x * (1.0 + jnp.tanh(math.sqrt(2.0 / math.pi) * (x + 0.044715 * (x * x * x))))


def _lru_out_body(hf_ref, hb_ref, g_ref, y_ref):
    y_ref[...] = ((hf_ref[...] + hb_ref[...]) * _gelu_tanh(g_ref[...])).astype(y_ref.dtype)


def _lru_out(hf, hb, p_lru):
    M, W = hf.shape
    row = pl.BlockSpec((ROW_BLOCK, W), lambda i: (i, 0))
    return pl.pallas_call(
        _lru_out_body, grid=(M // ROW_BLOCK,),
        in_specs=[row, row, pl.BlockSpec((ROW_BLOCK, W), lambda i: (i, 1))],
        out_specs=row, out_shape=jax.ShapeDtypeStruct((M, W), BF16),
        compiler_params=_cparams(1), name="lru_out")(hf, hb, p_lru)


def _merge_body(*refs, has_prev):
    if has_prev:
        h_ref, y_ref, wg_ref, wb_ref, bg_ref, prev_ref, o_ref, wg_bf, wb_bf = refs
    else:
        h_ref, y_ref, wg_ref, wb_ref, bg_ref, o_ref, wg_bf, wb_bf = refs

    @pl.when(pl.program_id(1) == 0)
    def _():
        wg_bf[...] = wg_ref[...].astype(BF16)
        wb_bf[...] = wb_ref[...].astype(BF16)

    gate = jax.nn.sigmoid(jnp.dot(h_ref[...], wg_bf[...], preferred_element_type=F32) + bg_ref[0])
    term = gate * jnp.dot(y_ref[...], wb_bf[...], preferred_element_type=F32)
    if has_prev:
        term = term + prev_ref[...]
    o_ref[...] = term.astype(o_ref.dtype)


def _merge_branch(h, y, w_gate, b_gate, w_branch, prev, *, layer, branch, bm, bn, out_dtype):
    M, D = h.shape
    kb = y.shape[1]
    in_specs = [pl.BlockSpec((bm, D), lambda n, m: (m, 0)),
                pl.BlockSpec((bm, kb), lambda n, m: (m, 0)),
                pl.BlockSpec((None, None, D, bn), lambda n, m: (layer, branch, 0, n)),
                pl.BlockSpec((None, None, kb, bn), lambda n, m: (layer, branch, 0, n)),
                pl.BlockSpec((None, 1, 1, bn), lambda n, m: (layer, branch, 0, n))]
    args = [h, y, w_gate, w_branch, b_gate.reshape(b_gate.shape[0], b_gate.shape[1], 1, D)]
    if prev is not None:
        in_specs.append(pl.BlockSpec((bm, bn), lambda n, m: (m, n)))
        args.append(prev)
    return pl.pallas_call(
        functools.partial(_merge_body, has_prev=prev is not None),
        grid=(D // bn, M // bm), in_specs=in_specs,
        out_specs=pl.BlockSpec((bm, bn), lambda n, m: (m, n)),
        out_shape=jax.ShapeDtypeStruct((M, D), out_dtype),
        scratch_shapes=[pltpu.VMEM((D, bn), BF16), pltpu.VMEM((kb, bn), BF16)],
        compiler_params=_cparams(2), name=f"merge_branch{branch}")(*args)


ROUTER_BLOCK = 512
EXPERT_BLOCK = 512


def _router_body(h_ref, wr_ref, rb_ref, tri_ref, ids_ref, wts_ref, cnt_ref, carry_ref, *, n_exp):
    @pl.when(pl.program_id(0) == 0)
    def _():
        carry_ref[...] = jnp.zeros(carry_ref.shape, F32)

    per = n_exp // N_GROUPS
    logits = lax.dot_general(wr_ref[...], h_ref[...], (((1,), (1,)), ((), ())), preferred_element_type=F32)
    scores = jax.nn.sigmoid(logits)
    sel = scores + rb_ref[:, :1]
    sel_r = [sel[e:e + 1, :] for e in range(n_exp)]
    sc_r = [scores[e:e + 1, :] for e in range(n_exp)]
    gs = []
    for g in range(N_GROUPS):
        rows = sel_r[g * per:(g + 1) * per]
        best = None
        for a in range(per):
            for b in range(a + 1, per):
                ps = rows[a] + rows[b]
                best = ps if best is None else jnp.maximum(best, ps)
        gs.append(best)
    bestg = jnp.full(gs[0].shape, N_GROUPS - 1, I32)
    run = gs[N_GROUPS - 1]
    for g in range(N_GROUPS - 2, -1, -1):
        take = gs[g] >= run
        bestg = jnp.where(take, g, bestg)
        run = jnp.where(take, gs[g], run)
    v = [sel_r[i] for i in range(per)]
    sc = [sc_r[i] for i in range(per)]
    for g in range(1, N_GROUPS):
        isg = bestg == g
        v = [jnp.where(isg, sel_r[g * per + i], v[i]) for i in range(per)]
        sc = [jnp.where(isg, sc_r[g * per + i], sc[i]) for i in range(per)]

    def first_argmax(vals):
        idx = jnp.full(vals[0].shape, per - 1, I32)
        mx = vals[per - 1]
        sv = sc[per - 1]
        for i in range(per - 2, -1, -1):
            take = vals[i] >= mx
            idx = jnp.where(take, i, idx)
            mx = jnp.where(take, vals[i], mx)
            sv = jnp.where(take, sc[i], sv)
        return idx, sv

    i1, s1 = first_argmax(v)
    v2 = [jnp.where(i1 == i, -jnp.inf, v[i]) for i in range(per)]
    i2, s2 = first_argmax(v2)
    denom = s1 + s2
    e0 = bestg * per + i1
    e1 = bestg * per + i2
    eid = lax.broadcasted_iota(I32, scores.shape, 0)
    oh0 = (eid == e0).astype(F32)
    oh1 = (eid == e1).astype(F32)
    oh = oh0 + oh1
    prefix = jnp.dot(oh.astype(BF16), tri_ref[...], preferred_element_type=F32) + carry_ref[:, :1]
    r0 = jnp.sum(oh0 * prefix, axis=0, keepdims=True)
    r1 = jnp.sum(oh1 * prefix, axis=0, keepdims=True)
    new_carry = carry_ref[:, :1] + jnp.sum(oh, axis=1, keepdims=True)
    carry_ref[...] = jnp.broadcast_to(new_carry, carry_ref.shape)
    cnt_ref[...] = jnp.broadcast_to(new_carry, cnt_ref.shape)
    ids_ref[...] = jnp.zeros(ids_ref.shape, I32)
    wts_ref[...] = jnp.zeros(wts_ref.shape, F32)
    for r, val in enumerate((e0, e1, r0.astype(I32), r1.astype(I32))):
        ids_ref[r:r + 1, :] = val
    wts_ref[0:1, :] = s1 / denom
    wts_ref[1:2, :] = s2 / denom


def _router(h2, w_router, router_bias):
    M, D = h2.shape
    n_exp = w_router.shape[1]
    bm = ROUTER_BLOCK
    tri = jnp.asarray(np.triu(np.ones((bm, bm), np.float32), k=1), BF16)
    wr_t = w_router.T.astype(BF16)
    rb = jnp.broadcast_to(router_bias.astype(F32)[:, None], (n_exp, LANES))
    return pl.pallas_call(
        functools.partial(_router_body, n_exp=n_exp),
        grid=(M // bm,),
        in_specs=[pl.BlockSpec((bm, D), lambda i: (i, 0)),
                  pl.BlockSpec((n_exp, D), lambda i: (0, 0)),
                  pl.BlockSpec((n_exp, LANES), lambda i: (0, 0)),
                  pl.BlockSpec((bm, bm), lambda i: (0, 0))],
        out_specs=[pl.BlockSpec((SUBLANES, bm), lambda i: (0, i)),
                   pl.BlockSpec((SUBLANES, bm), lambda i: (0, i)),
                   pl.BlockSpec((n_exp, LANES), lambda i: (0, 0))],
        out_shape=[jax.ShapeDtypeStruct((SUBLANES, M), I32), jax.ShapeDtypeStruct((SUBLANES, M), F32),
                   jax.ShapeDtypeStruct((n_exp, LANES), F32)],
        scratch_shapes=[pltpu.VMEM((n_exp, LANES), F32)],
        compiler_params=_cparams(1), name="moe_router")(h2, wr_t, rb, tri)


DISPATCH_ROWS = 256


def _dispatch_body(src_ref, h_hbm, xs_ref, sem):
    def copy(t):
        return pltpu.make_async_copy(h_hbm.at[pl.ds(src_ref[0, 0, t], 1)], xs_ref.at[pl.ds(t, 1)], sem)

    def start(t, c):
        copy(t).start()
        return c

    def wait(t, c):
        copy(t).wait()
        return c

    lax.fori_loop(0, DISPATCH_ROWS, start, 0)
    lax.fori_loop(0, DISPATCH_ROWS, wait, 0)


def _dispatch(h3, src, n_slots):
    slab = h3.shape[1:]
    return pl.pallas_call(
        _dispatch_body, grid=(n_slots // DISPATCH_ROWS,),
        in_specs=[pl.BlockSpec((1, 1, DISPATCH_ROWS), lambda i: (i, 0, 0), memory_space=pltpu.SMEM),
                  pl.BlockSpec(memory_space=pl.ANY)],
        out_specs=pl.BlockSpec((DISPATCH_ROWS,) + slab, lambda i: (i, 0, 0)),
        out_shape=jax.ShapeDtypeStruct((n_slots,) + slab, h3.dtype),
        scratch_shapes=[pltpu.SemaphoreType.DMA(())],
        compiler_params=_cparams(1), name="moe_dispatch")(src, h3)


def _expert_up_body(be_ref, nu_ref, x_ref, wg_ref, wu_ref, o_ref, wg_bf, wu_bf):
    i = pl.program_id(1)
    fresh = jnp.logical_or(i == 0, be_ref[i] != be_ref[jnp.maximum(i - 1, 0)])

    @pl.when(jnp.logical_and(fresh, i < nu_ref[0]))
    def _():
        wg_bf[...] = wg_ref[...].astype(BF16)
        wu_bf[...] = wu_ref[...].astype(BF16)

    @pl.when(i < nu_ref[0])
    def _():
        x = x_ref[...]
        g = jnp.dot(x, wg_bf[...], preferred_element_type=F32)
        u = jnp.dot(x, wu_bf[...], preferred_element_type=F32)
        o_ref[...] = (g * jax.nn.sigmoid(g) * u).astype(o_ref.dtype)

    @pl.when(i >= nu_ref[0])
    def _():
        o_ref[...] = jnp.zeros(o_ref.shape, o_ref.dtype)


def _expert_up(xs, w_gate, w_up, blk_expert, n_used, *, layer, bn):
    P, D = xs.shape
    de = w_gate.shape[-1]
    nblk = P // EXPERT_BLOCK

    def xmap(n, i, be, nu):
        return (jnp.minimum(i, nu[0] - 1), 0)

    def wmap(n, i, be, nu):
        return (layer, be[i], 0, n)

    return pl.pallas_call(
        _expert_up_body,
        grid_spec=pltpu.PrefetchScalarGridSpec(
            num_scalar_prefetch=2, grid=(de // bn, nblk),
            in_specs=[pl.BlockSpec((EXPERT_BLOCK, D), xmap),
                      pl.BlockSpec((None, None, D, bn), wmap),
                      pl.BlockSpec((None, None, D, bn), wmap)],
            out_specs=pl.BlockSpec((EXPERT_BLOCK, bn), lambda n, i, be, nu: (i, n)),
            scratch_shapes=[pltpu.VMEM((D, bn), BF16), pltpu.VMEM((D, bn), BF16)]),
        out_shape=jax.ShapeDtypeStruct((P, de), BF16),
        compiler_params=_cparams(2), name="moe_expert_up")(blk_expert, n_used, xs, w_gate, w_up)


def _expert_down_body(be_ref, nu_ref, x_ref, wd_ref, o_ref, wd_bf):
    i = pl.program_id(1)
    fresh = jnp.logical_or(i == 0, be_ref[i] != be_ref[jnp.maximum(i - 1, 0)])

    @pl.when(jnp.logical_and(fresh, i < nu_ref[0]))
    def _():
        wd_bf[...] = wd_ref[...].astype(BF16)

    @pl.when(i < nu_ref[0])
    def _():
        o_ref[...] = jnp.dot(x_ref[...], wd_bf[...], preferred_element_type=F32).astype(o_ref.dtype)

    @pl.when(i >= nu_ref[0])
    def _():
        o_ref[...] = jnp.zeros(o_ref.shape, o_ref.dtype)


def _expert_down(he, w_down, blk_expert, n_used, *, layer, bn):
    P, de = he.shape
    D = w_down.shape[-1]
    nblk = P // EXPERT_BLOCK
    return pl.pallas_call(
        _expert_down_body,
        grid_spec=pltpu.PrefetchScalarGridSpec(
            num_scalar_prefetch=2, grid=(D // bn, nblk),
            in_specs=[pl.BlockSpec((EXPERT_BLOCK, de), lambda n, i, be, nu: (jnp.minimum(i, nu[0] - 1), 0)),
                      pl.BlockSpec((None, None, de, bn), lambda n, i, be, nu: (layer, be[i], 0, n))],
            out_specs=pl.BlockSpec((EXPERT_BLOCK, bn), lambda n, i, be, nu: (i, n)),
            scratch_shapes=[pltpu.VMEM((de, bn), BF16)]),
        out_shape=jax.ShapeDtypeStruct((P, D), F32),
        compiler_params=_cparams(2), name="moe_expert_down")(blk_expert, n_used, he, w_down)


COMBINE_UNROLL = 8


def _combine_body(pos_ref, w_ref, o_hbm, y_ref, buf, sem):
    def copy(t, k):
        return pltpu.make_async_copy(o_hbm.at[pl.ds(pos_ref[0, k, t], 1)], buf.at[k, pl.ds(t, 1)], sem)

    def start(t, c):
        copy(t, 0).start()
        copy(t, 1).start()
        return c

    def wait(t, c):
        copy(t, 0).wait()
        copy(t, 1).wait()
        return c

    def mix(t, c):
        y_ref[t] = w_ref[0, 0, t] * buf[0, t] + w_ref[0, 1, t] * buf[1, t]
        return c

    lax.fori_loop(0, DISPATCH_ROWS, start, 0)
    lax.fori_loop(0, DISPATCH_ROWS, wait, 0)
    lax.fori_loop(0, DISPATCH_ROWS, mix, 0, unroll=COMBINE_UNROLL)


def _combine(o3, pos, wts, n_rows):
    slab = o3.shape[1:]
    smem = pl.BlockSpec((1, 2, DISPATCH_ROWS), lambda i: (i, 0, 0), memory_space=pltpu.SMEM)
    return pl.pallas_call(
        _combine_body, grid=(n_rows // DISPATCH_ROWS,),
        in_specs=[smem, smem, pl.BlockSpec(memory_space=pl.ANY)],
        out_specs=pl.BlockSpec((DISPATCH_ROWS,) + slab, lambda i: (i, 0, 0)),
        out_shape=jax.ShapeDtypeStruct((n_rows,) + slab, F32),
        scratch_shapes=[pltpu.VMEM((2, DISPATCH_ROWS) + slab, F32), pltpu.SemaphoreType.DMA(())],
        compiler_params=_cparams(1), name="moe_combine")(pos, wts, o3)


def _moe(h2, w_router, router_bias, w_gate, w_up, w_down, *, layer):
    M, D = h2.shape
    n_exp = w_router.shape[1]
    ids, wts, cnt = _router(h2, w_router, router_bias)
    counts = cnt[:, 0].astype(I32)
    padded = (counts + EXPERT_BLOCK - 1) // EXPERT_BLOCK * EXPERT_BLOCK
    ends = jnp.cumsum(padded)
    offs = ends - padded
    n_slots = (2 * M + n_exp * (EXPERT_BLOCK - 1)) // EXPERT_BLOCK * EXPERT_BLOCK
    nblk = n_slots // EXPERT_BLOCK
    pos0 = offs[ids[0]] + ids[2]
    pos1 = offs[ids[1]] + ids[3]
    pos = jnp.stack([pos0.reshape(-1, DISPATCH_ROWS), pos1.reshape(-1, DISPATCH_ROWS)], axis=1)
    wpair = jnp.stack([wts[0].reshape(-1, DISPATCH_ROWS), wts[1].reshape(-1, DISPATCH_ROWS)], axis=1)
    blk_start = jnp.arange(nblk, dtype=I32) * EXPERT_BLOCK
    blk_expert = jnp.minimum(jnp.sum((ends[None, :] <= blk_start[:, None]).astype(I32), axis=1), n_exp - 1)
    n_used = (ends[-1] // EXPERT_BLOCK).astype(I32).reshape(1)
    tok_ids = jnp.arange(M, dtype=I32)
    src = jnp.zeros((n_slots,), I32).at[jnp.concatenate([pos0, pos1])].set(jnp.concatenate([tok_ids, tok_ids]))
    xs = _dispatch(h2.reshape(M, D // LANES, LANES), src.reshape(-1, 1, DISPATCH_ROWS), n_slots).reshape(n_slots, D)
    he = _expert_up(xs, w_gate, w_up, blk_expert, n_used, layer=layer, bn=_pick(w_gate.shape[-1], (256, 128)))
    o = _expert_down(he, w_down, blk_expert, n_used, layer=layer, bn=_pick(D, (2048, 1024, 512)))
    return _combine(o.reshape(n_slots, D // LANES, LANES), pos, wpair, M).reshape(M, D)


class _Tok:
    def __init__(self, n_batch, seq, ctx):
        self.n_batch, self.seq, self.ctx = n_batch, seq, ctx
        self.t_all = seq + ctx
        self.lat_blocks = seq // ROW_BLOCK
        self.blocks_per_batch = self.t_all // ROW_BLOCK


def _rope_tables(seq, ctx):
    t = jnp.arange(seq, dtype=I32)
    row = (t // GRID_W).astype(F32)[:, None]
    col = (t % GRID_W).astype(F32)[:, None]

    def cs(rot_dim):
        n_freq = rot_dim // 4
        freqs = ROPE_THETA ** (-jnp.arange(n_freq, dtype=F32) / n_freq)
        ang = jnp.concatenate([row * freqs, col * freqs], axis=-1)
        cos = jnp.concatenate([jnp.cos(ang), jnp.ones((ctx, rot_dim // 2), F32)], axis=0)
        sin = jnp.concatenate([jnp.sin(ang), jnp.zeros((ctx, rot_dim // 2), F32)], axis=0)
        return cos, sin

    cos, sin = cs(HEAD_DIM)
    rope128 = (jnp.concatenate([cos, cos], axis=1), jnp.concatenate([-sin, sin], axis=1))
    cos, sin = cs(MLA_ROPE)
    z32 = jnp.zeros_like(sin)
    z64 = jnp.zeros((seq + ctx, LANES - MLA_ROPE), F32)
    rope64 = (jnp.concatenate([cos, cos, z64], axis=1), jnp.concatenate([sin, z32, z64], axis=1),
              jnp.concatenate([z32, sin, z64], axis=1))
    return rope128, rope64


def _q_rope_epilogue(acc, c_ref, slo_ref, shi_ref, *, scale):
    nope = acc[:, :MLA_NOPE] * scale
    rope = _rope64(acc[:, MLA_NOPE:], c_ref[...], slo_ref[...], shi_ref[...]) * scale
    return jnp.concatenate([nope, rope], axis=1)


def _kv_assemble_epilogue(acc, krr_ref):
    keys = jnp.concatenate([acc[:, :MLA_NOPE], krr_ref[...].astype(F32)], axis=1)
    vals = jnp.concatenate([acc[:, MLA_NOPE:], jnp.ones((acc.shape[0], LANES), F32)], axis=1)
    return keys, vals


def kernel(x, c, ctx, c_ctx, w_mod, b_mod, norm_mix, norm_ffn, w_in, mla_q_norm, mla_w_uq, mla_kv_norm,
           mla_w_ukv, gqa_q_norm, gqa_k_norm, na_rpb, lru_conv_w, lru_conv_b, lru_w_a, lru_b_a, lru_w_x,
           lru_b_x, lru_lambda, w_branch_gate, b_branch_gate, w_branch, w_out, w_router, router_bias,
           w_exp_gate, w_exp_up, w_exp_down, final_norm):
    B, S, D = x.shape
    CTX = ctx.shape[1]
    depth = w_in.shape[0]
    tok = _Tok(B, S, CTX)
    T = tok.t_all
    M = B * T
    bw = D // 4
    n_heads = bw // HEAD_DIM
    ql, kvl = mla_q_norm.shape[-1], mla_kv_norm.shape[-1]
    d_in = w_in.shape[-1]
    n_kv = (d_in - (ql + kvl + MLA_ROPE + bw + 3 * bw + 2 * bw)) // (2 * HEAD_DIM)
    q_per_kv = n_heads // n_kv
    n_lru_blocks = lru_w_a.shape[2]
    assert bw // n_lru_blocks == LANES and S % ROW_BLOCK == 0 and CTX == ROW_BLOCK and S % (GRID_W * NA_Q_ROWS) == 0
    mla_dq = 2 * LANES
    mla_scale = (MLA_NOPE + MLA_ROPE) ** -0.5 * LOG2E
    bm = _pick(M, (768, 512, 256))
    bm_tok = _pick(T, (768, 256))

    rope128, rope64 = _rope_tables(S, CTX)
    xa = jnp.concatenate([x, ctx], axis=1).reshape(M, D)
    cc = jnp.zeros((MOD_ROWS, D), F32).at[:B].set(c).at[B].set(c_ctx)
    silu_c = (cc * jax.nn.sigmoid(cc)).astype(BF16)

    o_mla = ql + kvl
    o_gqa = o_mla + MLA_ROPE
    w_gqa = bw + 2 * n_kv * HEAD_DIM
    o_na = o_gqa + w_gqa
    o_lru = o_na + 3 * bw

    y_ffn = mod_prev = None
    for layer in range(depth):
        need_ctx = layer < depth - 1
        modv = _matmul(silu_c, w_mod, w_prefix=(layer,), n_cols=N_MOD * D, bm=MOD_ROWS,
                       bn=_pick(N_MOD * D, (512, 256)), out_dtype=F32,
                       epilogue=lambda acc, b_ref: acc + b_ref[0],
                       extras=[(b_mod.reshape(depth, 1, N_MOD * D), (1, 1, _pick(N_MOD * D, (512, 256))),
                                lambda n, m, layer=layer: (layer, 0, n))],
                       name="mod_proj")
        modv = modv[:B + 1].reshape(B + 1, N_MOD, D)

        xa, h = _resid_norm(xa, y_ffn, mod_prev, modv, norm_mix[layer], gate_idx=5, shift_idx=0, scale_idx=1,
                            out_dtype=BF16, tok=tok, name="norm_mix")
        cqkv = _matmul(h, w_in, w_prefix=(layer,), n_cols=o_mla, bm=bm, bn=_pick(o_mla, (512, 384, 256, 128)),
                       out_dtype=F32, name="in_mla")
        w_kr = jnp.pad(w_in[layer, :, o_mla:o_gqa], ((0, 0), (0, LANES - MLA_ROPE)))
        kr = _matmul(h, w_kr, n_cols=LANES, bm=bm, bn=LANES, out_dtype=F32, name="in_kr")
        w_rest = w_in[layer, :, o_gqa:]
        p_gqa = _matmul(h, w_rest, n_cols=w_gqa, bm=bm, bn=_pick_bn(w_gqa, 0), out_dtype=F32, name="in_gqa")
        p_na = _matmul(h, w_rest, n_off=w_gqa, n_cols=3 * bw, bm=bm, bn=_pick_bn(3 * bw, w_gqa),
                       out_dtype=BF16, name="in_na")
        p_lru = _matmul(h, w_rest, n_off=w_gqa + 3 * bw, n_cols=2 * bw, bm=bm,
                        bn=_pick_bn(2 * bw, w_gqa + 3 * bw), out_dtype=F32, name="in_lru")

        qn, kvn, krr = _mla_prep(cqkv, kr, mla_q_norm[layer], mla_kv_norm[layer], rope64, tok)
        w_uq = jnp.pad(mla_w_uq[layer].reshape(ql, n_heads, MLA_NOPE + MLA_ROPE),
                       ((0, 0), (0, 0), (0, mla_dq - MLA_NOPE - MLA_ROPE))).reshape(ql, n_heads * mla_dq)
        tpb = T // bm_tok
        tab_spec = ((bm_tok, LANES), lambda n, m: (m % tpb, 0))
        q_mla = _matmul(qn, w_uq, n_cols=n_heads * mla_dq, bm=bm_tok, bn=mla_dq, out_dtype=BF16,
                        epilogue=functools.partial(_q_rope_epilogue, scale=mla_scale),
                        extras=[(t, *tab_spec) for t in rope64], name="mla_uq")
        k_mla, v_mla = _matmul(kvn, mla_w_ukv, w_prefix=(layer,), n_cols=n_heads * mla_dq, bm=bm, bn=mla_dq,
                               out_dtype=BF16, n_out=2, epilogue=_kv_assemble_epilogue,
                               extras=[(krr, (bm, LANES), lambda n, m: (m, 0))], name="mla_ukv")
        bq = _pick(S, (512, 256))
        bk = _pick(T, (768, 256))
        ctx_blk = S // ROW_BLOCK
        lat_q = dict(bq=bq, q_off=0, n_q=S // bq, bk=bk, k_off=0, n_k=T // bk, pipelined=True)
        q3, k3, v3 = (a.reshape(B, T, -1) for a in (q_mla, k_mla, v_mla))
        mla_args = dict(n_heads=n_heads, dq=mla_dq, dk=mla_dq, k_lane=lambda h: h, v_lane=lambda h: h,
                        v_has_ones=True)
        ya = _attention(q3, k3, v3, name="mla_attn", **lat_q, **mla_args)
        q_g, k_g, v_g = (a.reshape(B, T, -1) for a in
                         _gqa_prep(p_gqa, gqa_q_norm[layer], gqa_k_norm[layer], rope128, n_heads, n_kv, tok))
        gqa_args = dict(n_heads=n_heads, dq=HEAD_DIM, dk=HEAD_DIM, k_lane=lambda h: h // q_per_kv,
                        v_lane=lambda h: h // q_per_kv, v_has_ones=True)
        yb = _attention(q_g, k_g, v_g, name="gqa_attn", **lat_q, **gqa_args)
        na3 = p_na.reshape(B, T, 3 * bw)
        yc = _na_attention(na3, _na_bias_tables(na_rpb[layer], S // GRID_W), n_heads=n_heads, seq=S,
                           ctx_chunk=S // NA_CHUNK)
        if need_ctx:
            ctx_q = dict(bq=ROW_BLOCK, q_off=ctx_blk, n_q=1, bk=ROW_BLOCK, k_off=ctx_blk, n_k=1, pipelined=False)
            ya_c = _attention(q3, k3, v3, name="mla_attn_ctx", **ctx_q, **mla_args)
            yb_c = _attention(q_g, k_g, v_g, name="gqa_attn_ctx", **ctx_q, **gqa_args)
            yc_c = _attention(na3, na3, na3, name="na_attn_ctx", n_heads=n_heads, dq=HEAD_DIM, dk=HEAD_DIM,
                              k_lane=lambda h: n_heads + h, v_lane=lambda h: 2 * n_heads + h, v_has_ones=False,
                              s_scale=HEAD_DIM ** -0.5 * LOG2E, **ctx_q)
        else:
            ya_c = yb_c = yc_c = jnp.zeros((B, CTX, bw), BF16)
        ya, yb, yc = (jnp.concatenate([l_, c_], axis=1).reshape(M, bw)
                      for l_, c_ in ((ya, ya_c), (yb, yb_c), (yc, yc_c)))
        w_cat = jnp.concatenate([lru_w_a[layer, 0], lru_w_x[layer, 0], lru_w_a[layer, 1], lru_w_x[layer, 1]],
                                axis=-1)
        gates = _lru_gates(p_lru, lru_conv_w[layer], lru_conv_b[layer], w_cat, lru_b_a[layer], lru_b_x[layer],
                           lru_lambda[layer], tok)
        hf, hb = _lru_scan(*(g.reshape(B, T, n_lru_blocks, LANES) for g in gates), tok)
        yd = _lru_out(hf.reshape(M, bw), hb.reshape(M, bw), p_lru)

        bn_merge = _pick(D, (512, 256))
        bm_merge = _pick(M, (512, 256))
        merged = None
        for i_br, y_br in enumerate((ya, yb, yc, yd)):
            merged = _merge_branch(h, y_br, w_branch_gate, b_branch_gate, w_branch, merged, layer=layer,
                                   branch=i_br, bm=bm_merge, bn=bn_merge,
                                   out_dtype=BF16 if i_br == 3 else F32)
        y = _matmul(merged, w_out, w_prefix=(layer,), n_cols=D, bm=bm, bn=_pick(D, (512, 256)), out_dtype=F32,
                    name="out_proj")
        xa, h2 = _resid_norm(xa, y, modv, modv, norm_ffn[layer], gate_idx=2, shift_idx=3, scale_idx=4,
                             out_dtype=BF16, tok=tok, name="resid_norm_ffn")
        y_ffn = _moe(h2, w_router, router_bias, w_exp_gate, w_exp_up, w_exp_down, layer=layer)
        mod_prev = modv

    _, out = _resid_norm(xa, y_ffn, mod_prev, None, final_norm, gate_idx=5, shift_idx=None, scale_idx=None,
                         out_dtype=F32, tok=tok, name="final_norm")
    return out.reshape(B, T, D)[:, :S]
```

```python
import functools
import math

import numpy as np
import jax
import jax.numpy as jnp
from jax import lax
from jax.experimental import pallas as pl
from jax.experimental.pallas import tpu as pltpu

F32 = jnp.float32
BF16 = jnp.bfloat16
I32 = jnp.int32

HEAD_DIM = 128
GRID_W = 64
ROPE_THETA = 10000.0
NORM_EPS = 1e-6
MLA_NOPE = 128
MLA_ROPE = 64
NA_WIN_H = 8
NA_WIN_W = 16
LRU_CONV = 4
LRU_C = 8.0
N_GROUPS = 4
N_MOD = 6

LANES = 128
SUBLANES = 8
MOD_ROWS = 16
VMEM_LIMIT_BYTES = 58 * 1024 * 1024
ROW_BLOCK = 256
NEG_BIG = -1e30
LOG2E = math.log2(math.e)


def _pick(n, prefs):
    for p in prefs:
        if n % p == 0:
            return p
    return n


def _pick_bn(n_cols, n_off):
    return _pick(math.gcd(n_cols, n_off) if n_off else n_cols, (1024, 768, 512, 256, 128))


def _cparams(n_axes):
    return pltpu.CompilerParams(dimension_semantics=("arbitrary",) * n_axes,
                                vmem_limit_bytes=VMEM_LIMIT_BYTES)


def _mm_body(*refs, n_extra, n_out, epilogue, cache_w):
    a_ref, w_ref = refs[0], refs[1]
    extra = refs[2:2 + n_extra]
    o_refs = refs[2 + n_extra:2 + n_extra + n_out]
    if cache_w:
        wbf_ref = refs[2 + n_extra + n_out]

        @pl.when(pl.program_id(1) == 0)
        def _():
            wbf_ref[...] = w_ref[...].astype(BF16)

        w = wbf_ref[...]
    else:
        w = w_ref[...].astype(BF16)
    acc = jnp.dot(a_ref[...], w, preferred_element_type=F32)
    if epilogue is not None:
        acc = epilogue(acc, *extra)
    outs = acc if n_out > 1 else (acc,)
    for o_ref, val in zip(o_refs, outs):
        o_ref[...] = val.astype(o_ref.dtype)


def _matmul(a, w, *, n_cols, bm, bn, out_dtype, w_prefix=(), n_off=0, epilogue=None, extras=(), n_out=1, name):
    M, K = a.shape
    assert M % bm == 0 and n_cols % bn == 0 and n_off % bn == 0
    nb_off = n_off // bn
    grid = (n_cols // bn, M // bm)
    w_block = (None,) * len(w_prefix) + (K, bn)
    cache_w = (w.dtype != BF16) and grid[1] > 1
    w_mode = dict(pipeline_mode=pl.Buffered(1)) if cache_w else {}
    in_specs = [pl.BlockSpec((bm, K), lambda n, m: (m, 0)),
                pl.BlockSpec(w_block, lambda n, m: (*w_prefix, 0, n + nb_off), **w_mode)]
    args = [a, w]
    for arr, blk, imap in extras:
        in_specs.append(pl.BlockSpec(blk, imap))
        args.append(arr)
    scratch = [pltpu.VMEM((K, bn), BF16)] if cache_w else []
    out_spec = pl.BlockSpec((bm, bn), lambda n, m: (m, n))
    out_sds = jax.ShapeDtypeStruct((M, n_cols), out_dtype)
    res = pl.pallas_call(
        functools.partial(_mm_body, n_extra=len(extras), n_out=n_out, epilogue=epilogue, cache_w=cache_w),
        grid=grid, in_specs=in_specs, out_specs=[out_spec] * n_out, out_shape=[out_sds] * n_out,
        scratch_shapes=scratch, compiler_params=_cparams(2), name=name)(*args)
    return res if n_out > 1 else res[0]


def _rms(x, gain):
    return x * lax.rsqrt(jnp.mean(x * x, axis=-1, keepdims=True) + NORM_EPS) * gain


def _resid_norm_body(*refs, has_y, has_mod, gate_idx, shift_idx, scale_idx):
    it = iter(refs)
    x_ref = next(it)
    y_ref, modg_ref = (next(it), next(it)) if has_y else (None, None)
    modn_ref = next(it) if has_mod else None
    gain_ref = next(it)
    xo_ref = next(it) if has_y else None
    h_ref = next(it)
    x = x_ref[...]
    if has_y:
        x = x + modg_ref[0, gate_idx:gate_idx + 1, :] * y_ref[...]
        xo_ref[...] = x
    h = _rms(x, gain_ref[...])
    if has_mod:
        h = h * (1.0 + modn_ref[0, scale_idx:scale_idx + 1, :]) + modn_ref[0, shift_idx:shift_idx + 1, :]
    h_ref[...] = h.astype(h_ref.dtype)


def _mod_row_map(blocks_per_batch, lat_blocks, n_batch):
    def imap(i):
        b = i // blocks_per_batch
        j = i % blocks_per_batch
        return (jnp.where(j < lat_blocks, b, n_batch), 0, 0)
    return imap


def _resid_norm(x, y, mod_gate, mod_norm, gain, *, gate_idx, shift_idx, scale_idx, out_dtype, tok, name):
    M, D = x.shape
    nblk = M // ROW_BLOCK
    row = pl.BlockSpec((ROW_BLOCK, D), lambda i: (i, 0))
    mod_spec = pl.BlockSpec((1, N_MOD, D), _mod_row_map(tok.blocks_per_batch, tok.lat_blocks, tok.n_batch))
    in_specs, args = [row], [x]
    if y is not None:
        in_specs += [row, mod_spec]
        args += [y, mod_gate]
    if mod_norm is not None:
        in_specs.append(mod_spec)
        args.append(mod_norm)
    in_specs.append(pl.BlockSpec((1, D), lambda i: (0, 0)))
    args.append(gain.reshape(1, D))
    out_specs, out_shape = [], []
    if y is not None:
        out_specs.append(row)
        out_shape.append(jax.ShapeDtypeStruct((M, D), F32))
    out_specs.append(row)
    out_shape.append(jax.ShapeDtypeStruct((M, D), out_dtype))
    res = pl.pallas_call(
        functools.partial(_resid_norm_body, has_y=y is not None, has_mod=mod_norm is not None, gate_idx=gate_idx,
                          shift_idx=shift_idx, scale_idx=scale_idx),
        grid=(nblk,), in_specs=in_specs, out_specs=out_specs, out_shape=out_shape,
        compiler_params=_cparams(1), name=name)(*args)
    return res if y is not None else (x, res[0])


def _rope64(x, c, slo, shi):
    return x * c - pltpu.roll(x, 96, axis=1) * slo + pltpu.roll(x, 32, axis=1) * shi


def _rope128(x, c, ss):
    return x * c + pltpu.roll(x, 64, axis=1) * ss


def _mla_prep_body(cqkv_ref, kr_ref, qg_ref, kvg_ref, c_ref, slo_ref, shi_ref, qn_ref, kvn_ref, krr_ref, *, ql):
    cqkv = cqkv_ref[...]
    qn_ref[...] = _rms(cqkv[:, :ql], qg_ref[...]).astype(BF16)
    kvn_ref[...] = _rms(cqkv[:, ql:], kvg_ref[...]).astype(BF16)
    krr_ref[...] = _rope64(kr_ref[...], c_ref[...], slo_ref[...], shi_ref[...]).astype(BF16)


def _mla_prep(cqkv, kr, q_gain, kv_gain, rope64, tok):
    M, W = cqkv.shape
    ql, kvl = q_gain.shape[-1], kv_gain.shape[-1]
    bpb = tok.blocks_per_batch
    tab = pl.BlockSpec((ROW_BLOCK, LANES), lambda i: (i % bpb, 0))
    return pl.pallas_call(
        functools.partial(_mla_prep_body, ql=ql),
        grid=(M // ROW_BLOCK,),
        in_specs=[pl.BlockSpec((ROW_BLOCK, W), lambda i: (i, 0)),
                  pl.BlockSpec((ROW_BLOCK, LANES), lambda i: (i, 0)),
                  pl.BlockSpec((1, ql), lambda i: (0, 0)),
                  pl.BlockSpec((1, kvl), lambda i: (0, 0)), tab, tab, tab],
        out_specs=[pl.BlockSpec((ROW_BLOCK, ql), lambda i: (i, 0)),
                   pl.BlockSpec((ROW_BLOCK, kvl), lambda i: (i, 0)),
                   pl.BlockSpec((ROW_BLOCK, LANES), lambda i: (i, 0))],
        out_shape=[jax.ShapeDtypeStruct((M, ql), BF16), jax.ShapeDtypeStruct((M, kvl), BF16),
                   jax.ShapeDtypeStruct((M, LANES), BF16)],
        compiler_params=_cparams(1), name="mla_prep")(
            cqkv, kr, q_gain.reshape(1, ql), kv_gain.reshape(1, kvl), *rope64)


def _gqa_prep_body(p_ref, qg_ref, kg_ref, c_ref, ss_ref, q_ref, k_ref, v_ref, *, n_q, n_kv, scale):
    c, ss = c_ref[...], ss_ref[...]
    for h in range(n_q):
        xh = p_ref[:, h * HEAD_DIM:(h + 1) * HEAD_DIM]
        q_ref[:, h * HEAD_DIM:(h + 1) * HEAD_DIM] = (_rope128(_rms(xh, qg_ref[...]), c, ss) * scale).astype(BF16)
    off = n_q * HEAD_DIM
    for h in range(n_kv):
        xh = p_ref[:, off + h * HEAD_DIM:off + (h + 1) * HEAD_DIM]
        k_ref[:, h * HEAD_DIM:(h + 1) * HEAD_DIM] = _rope128(_rms(xh, kg_ref[...]), c, ss).astype(BF16)
    off += n_kv * HEAD_DIM
    for h in range(n_kv):
        v_ref[:, 2 * h * HEAD_DIM:(2 * h + 1) * HEAD_DIM] = p_ref[:, off + h * HEAD_DIM:off + (h + 1) * HEAD_DIM].astype(BF16)
        v_ref[:, (2 * h + 1) * HEAD_DIM:(2 * h + 2) * HEAD_DIM] = jnp.ones((ROW_BLOCK, HEAD_DIM), BF16)


def _gqa_prep(p, q_gain, k_gain, rope128, n_q, n_kv, tok):
    M, W = p.shape
    bpb = tok.blocks_per_batch
    tab = pl.BlockSpec((ROW_BLOCK, LANES), lambda i: (i % bpb, 0))
    wq, wk = n_q * HEAD_DIM, n_kv * HEAD_DIM
    return pl.pallas_call(
        functools.partial(_gqa_prep_body, n_q=n_q, n_kv=n_kv, scale=HEAD_DIM ** -0.5 * LOG2E),
        grid=(M // ROW_BLOCK,),
        in_specs=[pl.BlockSpec((ROW_BLOCK, W), lambda i: (i, 0)),
                  pl.BlockSpec((1, HEAD_DIM), lambda i: (0, 0)),
                  pl.BlockSpec((1, HEAD_DIM), lambda i: (0, 0)), tab, tab],
        out_specs=[pl.BlockSpec((ROW_BLOCK, wq), lambda i: (i, 0)),
                   pl.BlockSpec((ROW_BLOCK, wk), lambda i: (i, 0)),
                   pl.BlockSpec((ROW_BLOCK, 2 * wk), lambda i: (i, 0))],
        out_shape=[jax.ShapeDtypeStruct((M, wq), BF16), jax.ShapeDtypeStruct((M, wk), BF16),
                   jax.ShapeDtypeStruct((M, 2 * wk), BF16)],
        compiler_params=_cparams(1), name="gqa_prep")(
            p, q_gain.reshape(1, HEAD_DIM), k_gain.reshape(1, HEAD_DIM), *rope128)


def _lane_fold(x, op, init):
    for j in range(x.shape[1] // LANES):
        init = op(init, x[:, j * LANES:(j + 1) * LANES])
    return init


def _score_pass(q, k_ref, s_ref, *, n_k, bk, s_scale):
    m_lane = jnp.full((q.shape[0], LANES), -jnp.inf, F32)
    for c in range(n_k):
        s = lax.dot_general(q, k_ref[0, c * bk:(c + 1) * bk, :], (((1,), (1,)), ((), ())),
                            preferred_element_type=F32)
        if s_scale is not None:
            s = s * s_scale
        s_ref[c] = s
        m_lane = _lane_fold(s, jnp.maximum, m_lane)
    return jnp.max(m_lane, axis=1, keepdims=True)


def _value_pass(s_ref, m, v_ref, *, n_k, bk, v_has_ones):
    acc = jnp.zeros((s_ref.shape[1], HEAD_DIM + LANES), F32)
    for c in range(n_k):
        p = jnp.exp2(s_ref[c] - m)
        v1 = v_ref[0, c * bk:(c + 1) * bk, :]
        if not v_has_ones:
            v1 = jnp.concatenate([v1, jnp.ones((bk, LANES), BF16)], axis=-1)
        acc = acc + jnp.dot(p.astype(BF16), v1, preferred_element_type=F32)
    return acc[:, :HEAD_DIM] / acc[:, HEAD_DIM:HEAD_DIM + 1]


def _attn_body(q_ref, k_ref, v_ref, o_ref, s_sc, *, v_has_ones, n_k, bk, s_scale):
    m = _score_pass(q_ref[0], k_ref, s_sc, n_k=n_k, bk=bk, s_scale=s_scale)
    o_ref[0] = _value_pass(s_sc, m, v_ref, n_k=n_k, bk=bk, v_has_ones=v_has_ones).astype(o_ref.dtype)


def _attn_pipelined_body(q_ref, k_ref, v_ref, o_ref, sa, sb, ma, mb, *, n_k, bk, s_scale):
    i = pl.program_id(2)

    @pl.when(i == 0)
    def _():
        sb[...] = jnp.zeros(sb.shape, F32)
        mb[...] = jnp.zeros(mb.shape, F32)

    def step(s_w, m_w, s_r, m_r):
        m = _score_pass(q_ref[0], k_ref, s_w, n_k=n_k, bk=bk, s_scale=s_scale)
        m_w[...] = jnp.broadcast_to(m, m_w.shape)
        o_ref[0] = _value_pass(s_r, m_r[:, :1], v_ref, n_k=n_k, bk=bk, v_has_ones=True).astype(o_ref.dtype)

    @pl.when(i % 2 == 0)
    def _():
        step(sa, ma, sb, mb)

    @pl.when(i % 2 == 1)
    def _():
        step(sb, mb, sa, ma)


def _attention(q, k, v, *, n_heads, dq, dk, k_lane, v_lane, v_has_ones, bq, q_off, n_q, bk, k_off, n_k, name,
               pipelined, s_scale=None):
    B = q.shape[0]
    tk = n_k * bk
    dv = HEAD_DIM + LANES if v_has_ones else HEAD_DIM
    out_shape = jax.ShapeDtypeStruct((B, n_q * bq, n_heads * HEAD_DIM), BF16)
    if not pipelined:
        return pl.pallas_call(
            functools.partial(_attn_body, v_has_ones=v_has_ones, n_k=n_k, bk=bk, s_scale=s_scale),
            grid=(B, n_heads, n_q),
            in_specs=[pl.BlockSpec((1, bq, dq), lambda b, h, i: (b, i + q_off, h)),
                      pl.BlockSpec((1, tk, dk), lambda b, h, i: (b, k_off, k_lane(h))),
                      pl.BlockSpec((1, tk, dv), lambda b, h, i: (b, k_off, v_lane(h)))],
            out_specs=pl.BlockSpec((1, bq, HEAD_DIM), lambda b, h, i: (b, i, h)),
            out_shape=out_shape, scratch_shapes=[pltpu.VMEM((n_k, bq, bk), F32)],
            compiler_params=_cparams(3), name=name)(q, k, v)
    assert v_has_ones
    resident = dict(pipeline_mode=pl.Buffered(1))
    return pl.pallas_call(
        functools.partial(_attn_pipelined_body, n_k=n_k, bk=bk, s_scale=s_scale),
        grid=(B, n_heads, n_q + 1),
        in_specs=[pl.BlockSpec((1, bq, dq), lambda b, h, i: (b, jnp.minimum(i, n_q - 1) + q_off, h)),
                  pl.BlockSpec((1, tk, dk), lambda b, h, i: (b, k_off, k_lane(h)), **resident),
                  pl.BlockSpec((1, tk, dv), lambda b, h, i: (b, k_off, v_lane(h)), **resident)],
        out_specs=pl.BlockSpec((1, bq, HEAD_DIM), lambda b, h, i: (b, jnp.maximum(i - 1, 0), h)),
        out_shape=out_shape,
        scratch_shapes=[pltpu.VMEM((n_k, bq, bk), F32), pltpu.VMEM((n_k, bq, bk), F32),
                        pltpu.VMEM((bq, LANES), F32), pltpu.VMEM((bq, LANES), F32)],
        compiler_params=_cparams(3), name=name)(q, k, v)


NA_Q_ROWS = 8
NA_K_ROWS = 16
NA_CHUNK = 256
NA_N_CHUNKS = NA_K_ROWS * GRID_W // NA_CHUNK


def _na_body(*refs, scale):
    q_ref = refs[0]
    k_refs = refs[1:1 + NA_N_CHUNKS + 1]
    v_refs = refs[2 + NA_N_CHUNKS:3 + 2 * NA_N_CHUNKS]
    bias_ref = refs[3 + 2 * NA_N_CHUNKS]
    o_ref = refs[4 + 2 * NA_N_CHUNKS]
    q = q_ref[0]
    parts = [lax.dot_general(q, kr[0], (((1,), (1,)), ((), ())), preferred_element_type=F32) * scale
             for kr in k_refs]
    s_loc = jnp.concatenate(parts[:NA_N_CHUNKS], axis=1) + bias_ref[0, 0]
    s = jnp.concatenate([s_loc, parts[NA_N_CHUNKS]], axis=1)
    m = jnp.max(s, axis=1, keepdims=True)
    p = jnp.exp(s - m)
    l = jnp.sum(p, axis=1, keepdims=True)
    pb = p.astype(BF16)
    acc = None
    for j, vr in enumerate(v_refs):
        t = jnp.dot(pb[:, j * NA_CHUNK:(j + 1) * NA_CHUNK], vr[0], preferred_element_type=F32)
        acc = t if acc is None else acc + t
    o_ref[0] = (acc / l).astype(o_ref.dtype)


def _na_bias_tables(rpb, rows):
    n_groups = rows // NA_Q_ROWS
    n_h = rpb.shape[0]
    n_dr, n_dc = 2 * NA_WIN_H - 1, 2 * NA_WIN_W - 1
    exact = lax.Precision.HIGHEST
    qc = np.arange(GRID_W)[:, None]
    kc = np.arange(GRID_W)[None, :]
    cs = np.clip(qc - NA_WIN_W // 2, 0, GRID_W - NA_WIN_W)
    col_valid = (kc >= cs) & (kc < cs + NA_WIN_W)
    dc = np.clip(kc - qc + (NA_WIN_W - 1), 0, n_dc - 1)
    oh_c = (np.arange(n_dc)[:, None, None] == dc[None]).astype(np.float32).reshape(n_dc, GRID_W * GRID_W)
    cols = jnp.einsum('hrd,dx->hrx', rpb.astype(F32), jnp.asarray(oh_c), precision=exact)
    tabs = []
    j = np.arange(NA_Q_ROWS)[:, None]
    i = np.arange(NA_K_ROWS)[None, :]
    for g in (0, 1, n_groups - 1):
        qr = NA_Q_ROWS * g + j
        kr = np.clip(NA_Q_ROWS * g - NA_WIN_H // 2, 0, rows - NA_K_ROWS) + i
        rs = np.clip(qr - NA_WIN_H // 2, 0, rows - NA_WIN_H)
        row_valid = (kr >= rs) & (kr < rs + NA_WIN_H)
        dr = np.clip(kr - qr + (NA_WIN_H - 1), 0, n_dr - 1)
        oh_r = (np.arange(n_dr)[None, None, :] == dr[:, :, None]).astype(np.float32)
        oh_r = oh_r.reshape(NA_Q_ROWS * NA_K_ROWS, n_dr)
        t = jnp.einsum('pr,hrx->hpx', jnp.asarray(oh_r), cols, precision=exact)
        t = t.reshape(n_h, NA_Q_ROWS, NA_K_ROWS, GRID_W, GRID_W)
        valid = row_valid[:, :, None, None] & col_valid[None, None]
        t = jnp.where(valid[None], t, NEG_BIG).transpose(0, 1, 3, 2, 4)
        tabs.append(t.reshape(n_h, NA_Q_ROWS * GRID_W, NA_K_ROWS * GRID_W))
    return jnp.stack(tabs)


def _na_attention(qkv, bias_tab, *, n_heads, seq, ctx_chunk):
    B = qkv.shape[0]
    rows = seq // GRID_W
    n_groups = rows // NA_Q_ROWS
    bq = NA_Q_ROWS * GRID_W
    max_cb = seq // NA_CHUNK - NA_N_CHUNKS

    def cb(g):
        return jnp.clip(2 * g - 1, 0, max_cb)

    in_specs = [pl.BlockSpec((1, bq, HEAD_DIM), lambda b, g, h: (b, g, h))]
    for part in (1, 2):
        for c in range(NA_N_CHUNKS):
            in_specs.append(pl.BlockSpec((1, NA_CHUNK, HEAD_DIM),
                                         lambda b, g, h, c=c, part=part: (b, cb(g) + c, part * n_heads + h)))
        in_specs.append(pl.BlockSpec((1, NA_CHUNK, HEAD_DIM),
                                     lambda b, g, h, part=part: (b, ctx_chunk, part * n_heads + h)))
    in_specs.append(pl.BlockSpec(
        (1, 1, bq, NA_K_ROWS * GRID_W),
        lambda b, g, h: (jnp.where(g == 0, 0, jnp.where(g == n_groups - 1, 2, 1)), h, 0, 0)))
    n_in = 2 * (NA_N_CHUNKS + 1)
    return pl.pallas_call(
        functools.partial(_na_body, scale=HEAD_DIM ** -0.5),
        grid=(B, n_groups, n_heads), in_specs=in_specs,
        out_specs=pl.BlockSpec((1, bq, HEAD_DIM), lambda b, g, h: (b, g, h)),
        out_shape=jax.ShapeDtypeStruct((B, seq, n_heads * HEAD_DIM), BF16),
        compiler_params=_cparams(3), name="na_attention")(qkv, *([qkv] * n_in), bias_tab)


def _softplus(z):
    return jnp.maximum(z, 0.0) + jnp.log(1.0 + jnp.exp(-jnp.abs(z)))


def _lru_gates_body(x_ref, hp_ref, hn_ref, cw_ref, cb_ref, w_ref, ba_ref, bx_ref, lam_ref,
                    af_ref, bf_ref, ab_ref, bb_ref, xs_ref, *, bpb, lat_blocks, n_blocks_lru):
    i = pl.program_id(0)
    j = i % bpb
    first = jnp.logical_or(j == 0, j == lat_blocks)
    last = jnp.logical_or(j == lat_blocks - 1, j == bpb - 1)
    x = x_ref[...]
    xs_ref[SUBLANES:SUBLANES + ROW_BLOCK, :] = x
    xs_ref[0:SUBLANES, :] = jnp.where(first, 0.0, hp_ref[...])
    xs_ref[SUBLANES + ROW_BLOCK:, :] = jnp.where(last, 0.0, hn_ref[...])
    xc = cb_ref[...] + cw_ref[2:3, :] * x
    for tap, off in ((0, -2), (1, -1), (3, 1)):
        xc = xc + cw_ref[tap:tap + 1, :] * xs_ref[SUBLANES + off:SUBLANES + off + ROW_BLOCK, :]
    xcb = xc.astype(BF16)
    zs = [jnp.dot(xcb[:, n * LANES:(n + 1) * LANES], w_ref[n].astype(BF16), preferred_element_type=F32)
          for n in range(n_blocks_lru)]
    for d, (a_ref, b_ref) in enumerate(((af_ref, bf_ref), (ab_ref, bb_ref))):
        za = jnp.concatenate([z[:, (2 * d) * LANES:(2 * d + 1) * LANES] for z in zs], axis=1)
        zx = jnp.concatenate([z[:, (2 * d + 1) * LANES:(2 * d + 2) * LANES] for z in zs], axis=1)
        r = jax.nn.sigmoid(za + ba_ref[d:d + 1, :])
        ig = jax.nn.sigmoid(zx + bx_ref[d:d + 1, :])
        log_a = (-LRU_C) * r * _softplus(-lam_ref[d:d + 1, :])
        a = jnp.exp(log_a)
        a_ref[...] = a
        b_ref[...] = jnp.sqrt(1.0 - jnp.exp(2.0 * log_a)) * (ig * xc)


def _lru_gates(p_lru, conv_w, conv_b, w_cat, b_a, b_x, lam, tok):
    M = p_lru.shape[0]
    W = conv_b.shape[-1]
    nb = W // LANES
    rb8 = ROW_BLOCK // SUBLANES
    n_tiles8 = M // SUBLANES
    row = pl.BlockSpec((ROW_BLOCK, W), lambda i: (i, 0))
    vec2 = pl.BlockSpec((2, W), lambda i: (0, 0))
    return pl.pallas_call(
        functools.partial(_lru_gates_body, bpb=tok.blocks_per_batch, lat_blocks=tok.lat_blocks, n_blocks_lru=nb),
        grid=(M // ROW_BLOCK,),
        in_specs=[row,
                  pl.BlockSpec((SUBLANES, W), lambda i: (jnp.maximum(i * rb8 - 1, 0), 0)),
                  pl.BlockSpec((SUBLANES, W), lambda i: (jnp.minimum((i + 1) * rb8, n_tiles8 - 1), 0)),
                  pl.BlockSpec((LRU_CONV, W), lambda i: (0, 0)),
                  pl.BlockSpec((1, W), lambda i: (0, 0)),
                  pl.BlockSpec((nb, LANES, 4 * LANES), lambda i: (0, 0, 0)),
                  vec2, vec2, vec2],
        out_specs=[row] * 4,
        out_shape=[jax.ShapeDtypeStruct((M, W), F32)] * 4,
        scratch_shapes=[pltpu.VMEM((ROW_BLOCK + 2 * SUBLANES, W), F32)],
        compiler_params=_cparams(1), name="lru_gates")(
            p_lru, p_lru, p_lru, conv_w, conv_b.reshape(1, W), w_cat, b_a, b_x, lam)


LRU_UNROLL = 8


def _lru_scan_body(af_ref, bf_ref, ab_ref, bb_ref, hf_ref, hb_ref, carry_ref):
    @pl.when(pl.program_id(1) == 0)
    def _():
        carry_ref[...] = jnp.zeros(carry_ref.shape, F32)

    n = af_ref.shape[1]

    def step(t, hs):
        hf, hb = hs
        tb = n - 1 - t
        hf = af_ref[0, t] * hf + bf_ref[0, t]
        hb = ab_ref[0, tb] * hb + bb_ref[0, tb]
        hf_ref[0, t] = hf
        hb_ref[0, tb] = hb
        return hf, hb

    hf, hb = lax.fori_loop(0, n, step, (carry_ref[0], carry_ref[1]), unroll=LRU_UNROLL)
    carry_ref[0] = hf
    carry_ref[1] = hb


def _lru_scan(a_f, b_f, a_b, b_b, tok):
    B, T, R, _ = a_f.shape
    lat = tok.lat_blocks
    n_chunks = tok.blocks_per_batch

    def fwd(b, k):
        return (b, jnp.where(k == 0, lat, k - 1), 0, 0)

    def bwd(b, k):
        return (b, jnp.where(k == 0, lat, lat - k), 0, 0)

    blk = (1, ROW_BLOCK, R, LANES)
    return pl.pallas_call(
        _lru_scan_body, grid=(B, n_chunks),
        in_specs=[pl.BlockSpec(blk, fwd), pl.BlockSpec(blk, fwd), pl.BlockSpec(blk, bwd), pl.BlockSpec(blk, bwd)],
        out_specs=[pl.BlockSpec(blk, fwd), pl.BlockSpec(blk, bwd)],
        out_shape=[jax.ShapeDtypeStruct(a_f.shape, F32)] * 2,
        scratch_shapes=[pltpu.VMEM((2, R, LANES), F32)],
        compiler_params=_cparams(2), name="lru_scan")(a_f, b_f, a_b, b_b)


def _gelu_tanh(x):
    return 0.5 * x * (1.0 + jnp.tanh(math.sqrt(2.0 / math.pi) * (x + 0.044715 * (x * x * x))))


def _lru_out_body(hf_ref, hb_ref, g_ref, y_ref):
    y_ref[...] = ((hf_ref[...] + hb_ref[...]) * _gelu_tanh(g_ref[...])).astype(y_ref.dtype)


def _lru_out(hf, hb, p_lru):
    M, W = hf.shape
    row = pl.BlockSpec((ROW_BLOCK, W), lambda i: (i, 0))
    return pl.pallas_call(
        _lru_out_body, grid=(M // ROW_BLOCK,),
        in_specs=[row, row, pl.BlockSpec((ROW_BLOCK, W), lambda i: (i, 1))],
        out_specs=row, out_shape=jax.ShapeDtypeStruct((M, W), BF16),
        compiler_params=_cparams(1), name="lru_out")(hf, hb, p_lru)


def _merge_body(*refs, has_prev):
    if has_prev:
        h_ref, y_ref, wg_ref, wb_ref, bg_ref, prev_ref, o_ref = refs
    else:
        h_ref, y_ref, wg_ref, wb_ref, bg_ref, o_ref = refs
    gate = jax.nn.sigmoid(jnp.dot(h_ref[...], wg_ref[...], preferred_element_type=F32) + bg_ref[0])
    term = gate * jnp.dot(y_ref[...], wb_ref[...], preferred_element_type=F32)
    if has_prev:
        term = term + prev_ref[...]
    o_ref[...] = term.astype(o_ref.dtype)


def _merge_branch(h, y, w_gate, b_gate, w_branch, prev, *, layer, branch, bm, bn, out_dtype):
    M, D = h.shape
    kb = y.shape[1]
    in_specs = [pl.BlockSpec((bm, D), lambda n, m: (m, 0)),
                pl.BlockSpec((bm, kb), lambda n, m: (m, 0)),
                pl.BlockSpec((None, None, D, bn), lambda n, m: (layer, branch, 0, n)),
                pl.BlockSpec((None, None, kb, bn), lambda n, m: (layer, branch, 0, n)),
                pl.BlockSpec((None, 1, 1, bn), lambda n, m: (layer, branch, 0, n))]
    args = [h, y, w_gate, w_branch, b_gate.reshape(b_gate.shape[0], b_gate.shape[1], 1, D)]
    if prev is not None:
        in_specs.append(pl.BlockSpec((bm, bn), lambda n, m: (m, n)))
        args.append(prev)
    return pl.pallas_call(
        functools.partial(_merge_body, has_prev=prev is not None),
        grid=(D // bn, M // bm), in_specs=in_specs,
        out_specs=pl.BlockSpec((bm, bn), lambda n, m: (m, n)),
        out_shape=jax.ShapeDtypeStruct((M, D), out_dtype),
        compiler_params=_cparams(2), name=f"merge_branch{branch}")(*args)


ROUTER_BLOCK = 512
EXPERT_BLOCK = 512


def _router_body(h_ref, wr_ref, rb_ref, tri_ref, ids_ref, wts_ref, cnt_ref, carry_ref, *, n_exp):
    @pl.when(pl.program_id(0) == 0)
    def _():
        carry_ref[...] = jnp.zeros(carry_ref.shape, F32)

    per = n_exp // N_GROUPS
    logits = lax.dot_general(wr_ref[...], h_ref[...], (((1,), (1,)), ((), ())), preferred_element_type=F32)
    scores = jax.nn.sigmoid(logits)
    sel = scores + rb_ref[:, :1]
    sel_r = [sel[e:e + 1, :] for e in range(n_exp)]
    sc_r = [scores[e:e + 1, :] for e in range(n_exp)]
    gs = []
    for g in range(N_GROUPS):
        rows = sel_r[g * per:(g + 1) * per]
        best = None
        for a in range(per):
            for b in range(a + 1, per):
                ps = rows[a] + rows[b]
                best = ps if best is None else jnp.maximum(best, ps)
        gs.append(best)
    bestg = jnp.full(gs[0].shape, N_GROUPS - 1, I32)
    run = gs[N_GROUPS - 1]
    for g in range(N_GROUPS - 2, -1, -1):
        take = gs[g] >= run
        bestg = jnp.where(take, g, bestg)
        run = jnp.where(take, gs[g], run)
    v = [sel_r[i] for i in range(per)]
    sc = [sc_r[i] for i in range(per)]
    for g in range(1, N_GROUPS):
        isg = bestg == g
        v = [jnp.where(isg, sel_r[g * per + i], v[i]) for i in range(per)]
        sc = [jnp.where(isg, sc_r[g * per + i], sc[i]) for i in range(per)]

    def first_argmax(vals):
        idx = jnp.full(vals[0].shape, per - 1, I32)
        mx = vals[per - 1]
        sv = sc[per - 1]
        for i in range(per - 2, -1, -1):
            take = vals[i] >= mx
            idx = jnp.where(take, i, idx)
            mx = jnp.where(take, vals[i], mx)
            sv = jnp.where(take, sc[i], sv)
        return idx, sv

    i1, s1 = first_argmax(v)
    v2 = [jnp.where(i1 == i, -jnp.inf, v[i]) for i in range(per)]
    i2, s2 = first_argmax(v2)
    denom = s1 + s2
    e0 = bestg * per + i1
    e1 = bestg * per + i2
    eid = lax.broadcasted_iota(I32, scores.shape, 0)
    oh0 = (eid == e0).astype(F32)
    oh1 = (eid == e1).astype(F32)
    oh = oh0 + oh1
    prefix = jnp.dot(oh.astype(BF16), tri_ref[...], preferred_element_type=F32) + carry_ref[:, :1]
    r0 = jnp.sum(oh0 * prefix, axis=0, keepdims=True)
    r1 = jnp.sum(oh1 * prefix, axis=0, keepdims=True)
    new_carry = carry_ref[:, :1] + jnp.sum(oh, axis=1, keepdims=True)
    carry_ref[...] = jnp.broadcast_to(new_carry, carry_ref.shape)
    cnt_ref[...] = jnp.broadcast_to(new_carry, cnt_ref.shape)
    ids_ref[...] = jnp.zeros(ids_ref.shape, I32)
    wts_ref[...] = jnp.zeros(wts_ref.shape, F32)
    for r, val in enumerate((e0, e1, r0.astype(I32), r1.astype(I32))):
        ids_ref[r:r + 1, :] = val
    wts_ref[0:1, :] = s1 / denom
    wts_ref[1:2, :] = s2 / denom


def _router(h2, w_router, router_bias):
    M, D = h2.shape
    n_exp = w_router.shape[1]
    bm = ROUTER_BLOCK
    tri = jnp.asarray(np.triu(np.ones((bm, bm), np.float32), k=1), BF16)
    wr_t = w_router.T.astype(BF16)
    rb = jnp.broadcast_to(router_bias.astype(F32)[:, None], (n_exp, LANES))
    return pl.pallas_call(
        functools.partial(_router_body, n_exp=n_exp),
        grid=(M // bm,),
        in_specs=[pl.BlockSpec((bm, D), lambda i: (i, 0)),
                  pl.BlockSpec((n_exp, D), lambda i: (0, 0)),
                  pl.BlockSpec((n_exp, LANES), lambda i: (0, 0)),
                  pl.BlockSpec((bm, bm), lambda i: (0, 0))],
        out_specs=[pl.BlockSpec((SUBLANES, bm), lambda i: (0, i)),
                   pl.BlockSpec((SUBLANES, bm), lambda i: (0, i)),
                   pl.BlockSpec((n_exp, LANES), lambda i: (0, 0))],
        out_shape=[jax.ShapeDtypeStruct((SUBLANES, M), I32), jax.ShapeDtypeStruct((SUBLANES, M), F32),
                   jax.ShapeDtypeStruct((n_exp, LANES), F32)],
        scratch_shapes=[pltpu.VMEM((n_exp, LANES), F32)],
        compiler_params=_cparams(1), name="moe_router")(h2, wr_t, rb, tri)


DISPATCH_ROWS = 256


def _dispatch_body(n_live_ref, src_ref, h_hbm, xs_ref, sem):
    def copy(t):
        return pltpu.make_async_copy(h_hbm.at[pl.ds(src_ref[0, 0, t], 1)], xs_ref.at[pl.ds(t, 1)], sem)

    def start(t, c):
        copy(t).start()
        return c

    def wait(t, c):
        copy(t).wait()
        return c

    live = pl.program_id(0) * DISPATCH_ROWS < n_live_ref[0]

    @pl.when(live)
    def _():
        lax.fori_loop(0, DISPATCH_ROWS, start, 0)
        lax.fori_loop(0, DISPATCH_ROWS, wait, 0)

    @pl.when(jnp.logical_not(live))
    def _():
        xs_ref[...] = jnp.zeros(xs_ref.shape, xs_ref.dtype)


def _dispatch(h3, src, n_live, n_slots):
    slab = h3.shape[1:]
    return pl.pallas_call(
        _dispatch_body,
        grid_spec=pltpu.PrefetchScalarGridSpec(
            num_scalar_prefetch=1, grid=(n_slots // DISPATCH_ROWS,),
            in_specs=[pl.BlockSpec((1, 1, DISPATCH_ROWS), lambda i, nl: (i, 0, 0), memory_space=pltpu.SMEM),
                      pl.BlockSpec(memory_space=pl.ANY)],
            out_specs=pl.BlockSpec((DISPATCH_ROWS,) + slab, lambda i, nl: (i, 0, 0)),
            scratch_shapes=[pltpu.SemaphoreType.DMA(())]),
        out_shape=jax.ShapeDtypeStruct((n_slots,) + slab, h3.dtype),
        compiler_params=_cparams(1), name="moe_dispatch")(n_live, src, h3)


def _expert_up_body(be_ref, nu_ref, x_ref, wg_ref, wu_ref, o_ref, wg_bf, wu_bf):
    i = pl.program_id(1)
    fresh = jnp.logical_or(i == 0, be_ref[i] != be_ref[jnp.maximum(i - 1, 0)])

    @pl.when(jnp.logical_and(fresh, i < nu_ref[0]))
    def _():
        wg_bf[...] = wg_ref[...].astype(BF16)
        wu_bf[...] = wu_ref[...].astype(BF16)

    @pl.when(i < nu_ref[0])
    def _():
        x = x_ref[...]
        g = jnp.dot(x, wg_bf[...], preferred_element_type=F32)
        u = jnp.dot(x, wu_bf[...], preferred_element_type=F32)
        o_ref[...] = (g * jax.nn.sigmoid(g) * u).astype(o_ref.dtype)

    @pl.when(i >= nu_ref[0])
    def _():
        o_ref[...] = jnp.zeros(o_ref.shape, o_ref.dtype)


def _expert_up(xs, w_gate, w_up, blk_expert, n_used, *, layer, bn):
    P, D = xs.shape
    de = w_gate.shape[-1]
    nblk = P // EXPERT_BLOCK

    def xmap(n, i, be, nu):
        return (jnp.minimum(i, nu[0] - 1), 0)

    def wmap(n, i, be, nu):
        return (layer, be[i], 0, n)

    return pl.pallas_call(
        _expert_up_body,
        grid_spec=pltpu.PrefetchScalarGridSpec(
            num_scalar_prefetch=2, grid=(de // bn, nblk),
            in_specs=[pl.BlockSpec((EXPERT_BLOCK, D), xmap),
                      pl.BlockSpec((None, None, D, bn), wmap),
                      pl.BlockSpec((None, None, D, bn), wmap)],
            out_specs=pl.BlockSpec((EXPERT_BLOCK, bn), lambda n, i, be, nu: (i, n)),
            scratch_shapes=[pltpu.VMEM((D, bn), BF16), pltpu.VMEM((D, bn), BF16)]),
        out_shape=jax.ShapeDtypeStruct((P, de), BF16),
        compiler_params=_cparams(2), name="moe_expert_up")(blk_expert, n_used, xs, w_gate, w_up)


def _expert_down_body(be_ref, nu_ref, x_ref, wd_ref, o_ref, wd_bf):
    i = pl.program_id(1)
    fresh = jnp.logical_or(i == 0, be_ref[i] != be_ref[jnp.maximum(i - 1, 0)])

    @pl.when(jnp.logical_and(fresh, i < nu_ref[0]))
    def _():
        wd_bf[...] = wd_ref[...].astype(BF16)

    @pl.when(i < nu_ref[0])
    def _():
        o_ref[...] = jnp.dot(x_ref[...], wd_bf[...], preferred_element_type=F32).astype(o_ref.dtype)

    @pl.when(i >= nu_ref[0])
    def _():
        o_ref[...] = jnp.zeros(o_ref.shape, o_ref.dtype)


def _expert_down(he, w_down, blk_expert, n_used, *, layer, bn):
    P, de = he.shape
    D = w_down.shape[-1]
    nblk = P // EXPERT_BLOCK
    return pl.pallas_call(
        _expert_down_body,
        grid_spec=pltpu.PrefetchScalarGridSpec(
            num_scalar_prefetch=2, grid=(D // bn, nblk),
            in_specs=[pl.BlockSpec((EXPERT_BLOCK, de), lambda n, i, be, nu: (jnp.minimum(i, nu[0] - 1), 0)),
                      pl.BlockSpec((None, None, de, bn), lambda n, i, be, nu: (layer, be[i], 0, n))],
            out_specs=pl.BlockSpec((EXPERT_BLOCK, bn), lambda n, i, be, nu: (i, n)),
            scratch_shapes=[pltpu.VMEM((de, bn), BF16)]),
        out_shape=jax.ShapeDtypeStruct((P, D), F32),
        compiler_params=_cparams(2), name="moe_expert_down")(blk_expert, n_used, he, w_down)


COMBINE_UNROLL = 8


def _combine_body(pos_ref, w_ref, o_hbm, y_ref, buf, sem):
    def copy(t, k):
        return pltpu.make_async_copy(o_hbm.at[pl.ds(pos_ref[0, k, t], 1)], buf.at[k, pl.ds(t, 1)], sem)

    def start(t, c):
        copy(t, 0).start()
        copy(t, 1).start()
        return c

    def wait(t, c):
        copy(t, 0).wait()
        copy(t, 1).wait()
        return c

    def mix(t, c):
        y_ref[t] = w_ref[0, 0, t] * buf[0, t] + w_ref[0, 1, t] * buf[1, t]
        return c

    lax.fori_loop(0, DISPATCH_ROWS, start, 0)
    lax.fori_loop(0, DISPATCH_ROWS, wait, 0)
    lax.fori_loop(0, DISPATCH_ROWS, mix, 0, unroll=COMBINE_UNROLL)


def _combine(o3, pos, wts, n_rows):
    slab = o3.shape[1:]
    smem = pl.BlockSpec((1, 2, DISPATCH_ROWS), lambda i: (i, 0, 0), memory_space=pltpu.SMEM)
    return pl.pallas_call(
        _combine_body, grid=(n_rows // DISPATCH_ROWS,),
        in_specs=[smem, smem, pl.BlockSpec(memory_space=pl.ANY)],
        out_specs=pl.BlockSpec((DISPATCH_ROWS,) + slab, lambda i: (i, 0, 0)),
        out_shape=jax.ShapeDtypeStruct((n_rows,) + slab, F32),
        scratch_shapes=[pltpu.VMEM((2, DISPATCH_ROWS) + slab, F32), pltpu.SemaphoreType.DMA(())],
        compiler_params=_cparams(1), name="moe_combine")(pos, wts, o3)


def _moe(h2, w_router, router_bias, w_gate, w_up, w_down, *, layer):
    M, D = h2.shape
    n_exp = w_router.shape[1]
    ids, wts, cnt = _router(h2, w_router, router_bias)
    counts = cnt[:, 0].astype(I32)
    padded = (counts + EXPERT_BLOCK - 1) // EXPERT_BLOCK * EXPERT_BLOCK
    ends = jnp.cumsum(padded)
    offs = ends - padded
    n_slots = (2 * M + n_exp * (EXPERT_BLOCK - 1)) // EXPERT_BLOCK * EXPERT_BLOCK
    nblk = n_slots // EXPERT_BLOCK
    pos0 = offs[ids[0]] + ids[2]
    pos1 = offs[ids[1]] + ids[3]
    pos = jnp.stack([pos0.reshape(-1, DISPATCH_ROWS), pos1.reshape(-1, DISPATCH_ROWS)], axis=1)
    wpair = jnp.stack([wts[0].reshape(-1, DISPATCH_ROWS), wts[1].reshape(-1, DISPATCH_ROWS)], axis=1)
    blk_start = jnp.arange(nblk, dtype=I32) * EXPERT_BLOCK
    blk_expert = jnp.minimum(jnp.sum((ends[None, :] <= blk_start[:, None]).astype(I32), axis=1), n_exp - 1)
    n_used = (ends[-1] // EXPERT_BLOCK).astype(I32).reshape(1)
    tok_ids = jnp.arange(M, dtype=I32)
    src = jnp.zeros((n_slots,), I32).at[jnp.concatenate([pos0, pos1])].set(jnp.concatenate([tok_ids, tok_ids]))
    xs = _dispatch(h2.reshape(M, D // LANES, LANES), src.reshape(-1, 1, DISPATCH_ROWS),
                   ends[-1].astype(I32).reshape(1), n_slots).reshape(n_slots, D)
    he = _expert_up(xs, w_gate, w_up, blk_expert, n_used, layer=layer, bn=_pick(w_gate.shape[-1], (256, 128)))
    o = _expert_down(he, w_down, blk_expert, n_used, layer=layer, bn=_pick(D, (2048, 1024, 512)))
    return _combine(o.reshape(n_slots, D // LANES, LANES), pos, wpair, M).reshape(M, D)


class _Tok:
    def __init__(self, n_batch, seq, ctx):
        self.n_batch, self.seq, self.ctx = n_batch, seq, ctx
        self.t_all = seq + ctx
        self.lat_blocks = seq // ROW_BLOCK
        self.blocks_per_batch = self.t_all // ROW_BLOCK


def _rope_tables(seq, ctx):
    t = jnp.arange(seq, dtype=I32)
    row = (t // GRID_W).astype(F32)[:, None]
    col = (t % GRID_W).astype(F32)[:, None]

    def cs(rot_dim):
        n_freq = rot_dim // 4
        freqs = ROPE_THETA ** (-jnp.arange(n_freq, dtype=F32) / n_freq)
        ang = jnp.concatenate([row * freqs, col * freqs], axis=-1)
        cos = jnp.concatenate([jnp.cos(ang), jnp.ones((ctx, rot_dim // 2), F32)], axis=0)
        sin = jnp.concatenate([jnp.sin(ang), jnp.zeros((ctx, rot_dim // 2), F32)], axis=0)
        return cos, sin

    cos, sin = cs(HEAD_DIM)
    rope128 = (jnp.concatenate([cos, cos], axis=1), jnp.concatenate([-sin, sin], axis=1))
    cos, sin = cs(MLA_ROPE)
    z32 = jnp.zeros_like(sin)
    z64 = jnp.zeros((seq + ctx, LANES - MLA_ROPE), F32)
    rope64 = (jnp.concatenate([cos, cos, z64], axis=1), jnp.concatenate([sin, z32, z64], axis=1),
              jnp.concatenate([z32, sin, z64], axis=1))
    return rope128, rope64


def _q_rope_epilogue(acc, c_ref, slo_ref, shi_ref, *, scale):
    nope = acc[:, :MLA_NOPE] * scale
    rope = _rope64(acc[:, MLA_NOPE:], c_ref[...], slo_ref[...], shi_ref[...]) * scale
    return jnp.concatenate([nope, rope], axis=1)


def _kv_assemble_epilogue(acc, krr_ref):
    keys = jnp.concatenate([acc[:, :MLA_NOPE], krr_ref[...].astype(F32)], axis=1)
    vals = jnp.concatenate([acc[:, MLA_NOPE:], jnp.ones((acc.shape[0], LANES), F32)], axis=1)
    return keys, vals


def kernel(x, c, ctx, c_ctx, w_mod, b_mod, norm_mix, norm_ffn, w_in, mla_q_norm, mla_w_uq, mla_kv_norm,
           mla_w_ukv, gqa_q_norm, gqa_k_norm, na_rpb, lru_conv_w, lru_conv_b, lru_w_a, lru_b_a, lru_w_x,
           lru_b_x, lru_lambda, w_branch_gate, b_branch_gate, w_branch, w_out, w_router, router_bias,
           w_exp_gate, w_exp_up, w_exp_down, final_norm):
    B, S, D = x.shape
    CTX = ctx.shape[1]
    depth = w_in.shape[0]
    tok = _Tok(B, S, CTX)
    T = tok.t_all
    M = B * T
    bw = D // 4
    n_heads = bw // HEAD_DIM
    ql, kvl = mla_q_norm.shape[-1], mla_kv_norm.shape[-1]
    d_in = w_in.shape[-1]
    n_kv = (d_in - (ql + kvl + MLA_ROPE + bw + 3 * bw + 2 * bw)) // (2 * HEAD_DIM)
    q_per_kv = n_heads // n_kv
    n_lru_blocks = lru_w_a.shape[2]
    assert bw // n_lru_blocks == LANES and S % ROW_BLOCK == 0 and CTX == ROW_BLOCK and S % (GRID_W * NA_Q_ROWS) == 0
    mla_dq = 2 * LANES
    mla_scale = (MLA_NOPE + MLA_ROPE) ** -0.5 * LOG2E
    bm = _pick(M, (768, 512, 256))
    bm_tok = _pick(T, (768, 256))

    rope128, rope64 = _rope_tables(S, CTX)
    xa = jnp.concatenate([x, ctx], axis=1).reshape(M, D)
    cc = jnp.zeros((MOD_ROWS, D), F32).at[:B].set(c).at[B].set(c_ctx)
    silu_c = (cc * jax.nn.sigmoid(cc)).astype(BF16)

    o_mla = ql + kvl
    o_gqa = o_mla + MLA_ROPE
    w_gqa = bw + 2 * n_kv * HEAD_DIM
    o_na = o_gqa + w_gqa
    o_lru = o_na + 3 * bw

    bn_mod = _pick(N_MOD * D, (512, 256))
    mods = []
    for layer in range(depth):
        modv = _matmul(silu_c, w_mod, w_prefix=(layer,), n_cols=N_MOD * D, bm=MOD_ROWS, bn=bn_mod, out_dtype=F32,
                       epilogue=lambda acc, b_ref: acc + b_ref[0],
                       extras=[(b_mod.reshape(depth, 1, N_MOD * D), (1, 1, bn_mod),
                                lambda n, m, layer=layer: (layer, 0, n))],
                       name="mod_proj")
        mods.append(modv[:B + 1].reshape(B + 1, N_MOD, D))
    wg_bf, wb_bf = w_branch_gate.astype(BF16), w_branch.astype(BF16)

    y_ffn = mod_prev = None
    for layer in range(depth):
        need_ctx = layer < depth - 1
        modv = mods[layer]
        xa, h = _resid_norm(xa, y_ffn, mod_prev, modv, norm_mix[layer], gate_idx=5, shift_idx=0, scale_idx=1,
                            out_dtype=BF16, tok=tok, name="norm_mix")
        cqkv = _matmul(h, w_in, w_prefix=(layer,), n_cols=o_mla, bm=bm, bn=_pick_bn(o_mla, 0), out_dtype=F32,
                       name="in_mla")
        w_kr = jnp.pad(w_in[layer, :, o_mla:o_gqa], ((0, 0), (0, LANES - MLA_ROPE)))
        kr = _matmul(h, w_kr, n_cols=LANES, bm=bm, bn=LANES, out_dtype=F32, name="in_kr")
        w_rest = w_in[layer, :, o_gqa:o_lru]
        p_gqa = _matmul(h, w_rest, n_cols=w_gqa, bm=bm, bn=_pick_bn(w_gqa, 0), out_dtype=F32, name="in_gqa")
        p_na = _matmul(h, w_rest, n_off=w_gqa, n_cols=3 * bw, bm=bm, bn=_pick_bn(3 * bw, w_gqa),
                       out_dtype=BF16, name="in_na")
        p_lru = _matmul(h, w_in[layer, :, o_lru:], n_cols=2 * bw, bm=bm, bn=_pick_bn(2 * bw, 0), out_dtype=F32,
                        name="in_lru")

        qn, kvn, krr = _mla_prep(cqkv, kr, mla_q_norm[layer], mla_kv_norm[layer], rope64, tok)
        w_uq = jnp.pad(mla_w_uq[layer].reshape(ql, n_heads, MLA_NOPE + MLA_ROPE),
                       ((0, 0), (0, 0), (0, mla_dq - MLA_NOPE - MLA_ROPE))).reshape(ql, n_heads * mla_dq)
        tpb = T // bm_tok
        tab_spec = ((bm_tok, LANES), lambda n, m: (m % tpb, 0))
        q_mla = _matmul(qn, w_uq, n_cols=n_heads * mla_dq, bm=bm_tok, bn=mla_dq, out_dtype=BF16,
                        epilogue=functools.partial(_q_rope_epilogue, scale=mla_scale),
                        extras=[(t, *tab_spec) for t in rope64], name="mla_uq")
        k_mla, v_mla = _matmul(kvn, mla_w_ukv, w_prefix=(layer,), n_cols=n_heads * mla_dq, bm=bm, bn=mla_dq,
                               out_dtype=BF16, n_out=2, epilogue=_kv_assemble_epilogue,
                               extras=[(krr, (bm, LANES), lambda n, m: (m, 0))], name="mla_ukv")
        bq = _pick(S, (512, 256))
        bk = _pick(T, (768, 256))
        ctx_blk = S // ROW_BLOCK
        lat_q = dict(bq=bq, q_off=0, n_q=S // bq, bk=bk, k_off=0, n_k=T // bk, pipelined=True)
        q3, k3, v3 = (a.reshape(B, T, -1) for a in (q_mla, k_mla, v_mla))
        mla_args = dict(n_heads=n_heads, dq=mla_dq, dk=mla_dq, k_lane=lambda h: h, v_lane=lambda h: h,
                        v_has_ones=True)
        ya = _attention(q3, k3, v3, name="mla_attn", **lat_q, **mla_args)
        q_g, k_g, v_g = (a.reshape(B, T, -1) for a in
                         _gqa_prep(p_gqa, gqa_q_norm[layer], gqa_k_norm[layer], rope128, n_heads, n_kv, tok))
        gqa_args = dict(n_heads=n_heads, dq=HEAD_DIM, dk=HEAD_DIM, k_lane=lambda h: h // q_per_kv,
                        v_lane=lambda h: h // q_per_kv, v_has_ones=True)
        yb = _attention(q_g, k_g, v_g, name="gqa_attn", **lat_q, **gqa_args)
        na3 = p_na.reshape(B, T, 3 * bw)
        yc = _na_attention(na3, _na_bias_tables(na_rpb[layer], S // GRID_W), n_heads=n_heads, seq=S,
                           ctx_chunk=S // NA_CHUNK)
        if need_ctx:
            ctx_q = dict(bq=ROW_BLOCK, q_off=ctx_blk, n_q=1, bk=ROW_BLOCK, k_off=ctx_blk, n_k=1, pipelined=False)
            ya_c = _attention(q3, k3, v3, name="mla_attn_ctx", **ctx_q, **mla_args)
            yb_c = _attention(q_g, k_g, v_g, name="gqa_attn_ctx", **ctx_q, **gqa_args)
            yc_c = _attention(na3, na3, na3, name="na_attn_ctx", n_heads=n_heads, dq=HEAD_DIM, dk=HEAD_DIM,
                              k_lane=lambda h: n_heads + h, v_lane=lambda h: 2 * n_heads + h, v_has_ones=False,
                              s_scale=HEAD_DIM ** -0.5 * LOG2E, **ctx_q)
        else:
            ya_c = yb_c = yc_c = jnp.zeros((B, CTX, bw), BF16)
        ya, yb, yc = (jnp.concatenate([l_, c_], axis=1).reshape(M, bw)
                      for l_, c_ in ((ya, ya_c), (yb, yb_c), (yc, yc_c)))
        w_cat = jnp.concatenate([lru_w_a[layer, 0], lru_w_x[layer, 0], lru_w_a[layer, 1], lru_w_x[layer, 1]],
                                axis=-1)
        gates = _lru_gates(p_lru, lru_conv_w[layer], lru_conv_b[layer], w_cat, lru_b_a[layer], lru_b_x[layer],
                           lru_lambda[layer], tok)
        hf, hb = _lru_scan(*(g.reshape(B, T, n_lru_blocks, LANES) for g in gates), tok)
        yd = _lru_out(hf.reshape(M, bw), hb.reshape(M, bw), p_lru)

        bn_merge = _pick(D, (1024, 512, 256))
        bm_merge = _pick(M, (512, 256))
        merged = None
        for i_br, y_br in enumerate((ya, yb, yc, yd)):
            merged = _merge_branch(h, y_br, wg_bf, b_branch_gate, wb_bf, merged, layer=layer,
                                   branch=i_br, bm=bm_merge, bn=bn_merge,
                                   out_dtype=BF16 if i_br == 3 else F32)
        y = _matmul(merged, w_out, w_prefix=(layer,), n_cols=D, bm=bm, bn=_pick_bn(D, 0), out_dtype=F32,
                    name="out_proj")
        xa, h2 = _resid_norm(xa, y, modv, modv, norm_ffn[layer], gate_idx=2, shift_idx=3, scale_idx=4,
                             out_dtype=BF16, tok=tok, name="resid_norm_ffn")
        y_ffn = _moe(h2, w_router, router_bias, w_exp_gate, w_exp_up, w_exp_down, layer=layer)
        mod_prev = modv

    _, out = _resid_norm(xa, y_ffn, mod_prev, None, final_norm, gate_idx=5, shift_idx=None, scale_idx=None,
                         out_dtype=F32, tok=tok, name="final_norm")
    return out.reshape(B, T, D)[:, :S]
```

```python
import functools
import math

import numpy as np
import jax
import jax.numpy as jnp
from jax import lax
from jax.experimental import pallas as pl
from jax.experimental.pallas import tpu as pltpu

F32 = jnp.float32
BF16 = jnp.bfloat16
I32 = jnp.int32

HEAD_DIM = 128
GRID_W = 64
ROPE_THETA = 10000.0
NORM_EPS = 1e-6
MLA_NOPE = 128
MLA_ROPE = 64
NA_WIN_H = 8
NA_WIN_W = 16
LRU_CONV = 4
LRU_C = 8.0
N_GROUPS = 4
N_MOD = 6

LANES = 128
SUBLANES = 8
MOD_ROWS = 16
VMEM_LIMIT_BYTES = 58 * 1024 * 1024
ROW_BLOCK = 256
NEG_BIG = -1e30
LOG2E = math.log2(math.e)


def _pick(n, prefs):
    for p in prefs:
        if n % p == 0:
            return p
    return n


def _pick_bn(n_cols, n_off):
    return _pick(math.gcd(n_cols, n_off) if n_off else n_cols, (1024, 768, 512, 256, 128))


def _cparams(n_axes):
    return pltpu.CompilerParams(dimension_semantics=("arbitrary",) * n_axes,
                                vmem_limit_bytes=VMEM_LIMIT_BYTES)


def _mm_body(*refs, n_extra, n_out, epilogue, cache_w):
    a_ref, w_ref = refs[0], refs[1]
    extra = refs[2:2 + n_extra]
    o_refs = refs[2 + n_extra:2 + n_extra + n_out]
    if cache_w:
        wbf_ref = refs[2 + n_extra + n_out]

        @pl.when(pl.program_id(1) == 0)
        def _():
            wbf_ref[...] = w_ref[...].astype(BF16)

        w = wbf_ref[...]
    else:
        w = w_ref[...].astype(BF16)
    acc = jnp.dot(a_ref[...], w, preferred_element_type=F32)
    if epilogue is not None:
        acc = epilogue(acc, *extra)
    outs = acc if n_out > 1 else (acc,)
    for o_ref, val in zip(o_refs, outs):
        o_ref[...] = val.astype(o_ref.dtype)


def _matmul(a, w, *, n_cols, bm, bn, out_dtype, w_prefix=(), n_off=0, epilogue=None, extras=(), n_out=1, name):
    M, K = a.shape
    assert M % bm == 0 and n_cols % bn == 0 and n_off % bn == 0
    nb_off = n_off // bn
    grid = (n_cols // bn, M // bm)
    w_block = (None,) * len(w_prefix) + (K, bn)
    cache_w = (w.dtype != BF16) and grid[1] > 1
    w_mode = dict(pipeline_mode=pl.Buffered(1)) if cache_w else {}
    in_specs = [pl.BlockSpec((bm, K), lambda n, m: (m, 0)),
                pl.BlockSpec(w_block, lambda n, m: (*w_prefix, 0, n + nb_off), **w_mode)]
    args = [a, w]
    for arr, blk, imap in extras:
        in_specs.append(pl.BlockSpec(blk, imap))
        args.append(arr)
    scratch = [pltpu.VMEM((K, bn), BF16)] if cache_w else []
    out_spec = pl.BlockSpec((bm, bn), lambda n, m: (m, n))
    out_sds = jax.ShapeDtypeStruct((M, n_cols), out_dtype)
    res = pl.pallas_call(
        functools.partial(_mm_body, n_extra=len(extras), n_out=n_out, epilogue=epilogue, cache_w=cache_w),
        grid=grid, in_specs=in_specs, out_specs=[out_spec] * n_out, out_shape=[out_sds] * n_out,
        scratch_shapes=scratch, compiler_params=_cparams(2), name=name)(*args)
    return res if n_out > 1 else res[0]


def _rms(x, gain):
    return x * lax.rsqrt(jnp.mean(x * x, axis=-1, keepdims=True) + NORM_EPS) * gain


def _resid_norm_body(*refs, has_y, has_mod, gate_idx, shift_idx, scale_idx):
    it = iter(refs)
    x_ref = next(it)
    y_ref, modg_ref = (next(it), next(it)) if has_y else (None, None)
    modn_ref = next(it) if has_mod else None
    gain_ref = next(it)
    xo_ref = next(it) if has_y else None
    h_ref = next(it)
    x = x_ref[...]
    if has_y:
        x = x + modg_ref[0, gate_idx:gate_idx + 1, :] * y_ref[...]
        xo_ref[...] = x
    h = _rms(x, gain_ref[...])
    if has_mod:
        h = h * (1.0 + modn_ref[0, scale_idx:scale_idx + 1, :]) + modn_ref[0, shift_idx:shift_idx + 1, :]
    h_ref[...] = h.astype(h_ref.dtype)


def _mod_row_map(blocks_per_batch, lat_blocks, n_batch):
    def imap(i):
        b = i // blocks_per_batch
        j = i % blocks_per_batch
        return (jnp.where(j < lat_blocks, b, n_batch), 0, 0)
    return imap


def _resid_norm(x, y, mod_gate, mod_norm, gain, *, gate_idx, shift_idx, scale_idx, out_dtype, tok, name):
    M, D = x.shape
    nblk = M // ROW_BLOCK
    row = pl.BlockSpec((ROW_BLOCK, D), lambda i: (i, 0))
    mod_spec = pl.BlockSpec((1, N_MOD, D), _mod_row_map(tok.blocks_per_batch, tok.lat_blocks, tok.n_batch))
    in_specs, args = [row], [x]
    if y is not None:
        in_specs += [row, mod_spec]
        args += [y, mod_gate]
    if mod_norm is not None:
        in_specs.append(mod_spec)
        args.append(mod_norm)
    in_specs.append(pl.BlockSpec((1, D), lambda i: (0, 0)))
    args.append(gain.reshape(1, D))
    out_specs, out_shape = [], []
    if y is not None:
        out_specs.append(row)
        out_shape.append(jax.ShapeDtypeStruct((M, D), F32))
    out_specs.append(row)
    out_shape.append(jax.ShapeDtypeStruct((M, D), out_dtype))
    res = pl.pallas_call(
        functools.partial(_resid_norm_body, has_y=y is not None, has_mod=mod_norm is not None, gate_idx=gate_idx,
                          shift_idx=shift_idx, scale_idx=scale_idx),
        grid=(nblk,), in_specs=in_specs, out_specs=out_specs, out_shape=out_shape,
        compiler_params=_cparams(1), name=name)(*args)
    return res if y is not None else (x, res[0])


def _rope64(x, c, slo, shi):
    return x * c - pltpu.roll(x, 96, axis=1) * slo + pltpu.roll(x, 32, axis=1) * shi


def _rope128(x, c, ss):
    return x * c + pltpu.roll(x, 64, axis=1) * ss


def _mla_prep_body(cqkv_ref, kr_ref, qg_ref, kvg_ref, c_ref, slo_ref, shi_ref, qn_ref, kvn_ref, krr_ref, *, ql):
    cqkv = cqkv_ref[...]
    qn_ref[...] = _rms(cqkv[:, :ql], qg_ref[...]).astype(BF16)
    kvn_ref[...] = _rms(cqkv[:, ql:], kvg_ref[...]).astype(BF16)
    krr_ref[...] = _rope64(kr_ref[...], c_ref[...], slo_ref[...], shi_ref[...]).astype(BF16)


def _mla_prep(cqkv, kr, q_gain, kv_gain, rope64, tok):
    M, W = cqkv.shape
    ql, kvl = q_gain.shape[-1], kv_gain.shape[-1]
    bpb = tok.blocks_per_batch
    tab = pl.BlockSpec((ROW_BLOCK, LANES), lambda i: (i % bpb, 0))
    return pl.pallas_call(
        functools.partial(_mla_prep_body, ql=ql),
        grid=(M // ROW_BLOCK,),
        in_specs=[pl.BlockSpec((ROW_BLOCK, W), lambda i: (i, 0)),
                  pl.BlockSpec((ROW_BLOCK, LANES), lambda i: (i, 0)),
                  pl.BlockSpec((1, ql), lambda i: (0, 0)),
                  pl.BlockSpec((1, kvl), lambda i: (0, 0)), tab, tab, tab],
        out_specs=[pl.BlockSpec((ROW_BLOCK, ql), lambda i: (i, 0)),
                   pl.BlockSpec((ROW_BLOCK, kvl), lambda i: (i, 0)),
                   pl.BlockSpec((ROW_BLOCK, LANES), lambda i: (i, 0))],
        out_shape=[jax.ShapeDtypeStruct((M, ql), BF16), jax.ShapeDtypeStruct((M, kvl), BF16),
                   jax.ShapeDtypeStruct((M, LANES), BF16)],
        compiler_params=_cparams(1), name="mla_prep")(
            cqkv, kr, q_gain.reshape(1, ql), kv_gain.reshape(1, kvl), *rope64)


def _gqa_prep_body(p_ref, qg_ref, kg_ref, c_ref, ss_ref, q_ref, k_ref, v_ref, *, n_q, n_kv, scale):
    c, ss = c_ref[...], ss_ref[...]
    for h in range(n_q):
        xh = p_ref[:, h * HEAD_DIM:(h + 1) * HEAD_DIM]
        q_ref[:, h * HEAD_DIM:(h + 1) * HEAD_DIM] = (_rope128(_rms(xh, qg_ref[...]), c, ss) * scale).astype(BF16)
    off = n_q * HEAD_DIM
    for h in range(n_kv):
        xh = p_ref[:, off + h * HEAD_DIM:off + (h + 1) * HEAD_DIM]
        k_ref[:, h * HEAD_DIM:(h + 1) * HEAD_DIM] = _rope128(_rms(xh, kg_ref[...]), c, ss).astype(BF16)
    off += n_kv * HEAD_DIM
    for h in range(n_kv):
        v_ref[:, 2 * h * HEAD_DIM:(2 * h + 1) * HEAD_DIM] = p_ref[:, off + h * HEAD_DIM:off + (h + 1) * HEAD_DIM].astype(BF16)
        v_ref[:, (2 * h + 1) * HEAD_DIM:(2 * h + 2) * HEAD_DIM] = jnp.ones((ROW_BLOCK, HEAD_DIM), BF16)


def _gqa_prep(p, q_gain, k_gain, rope128, n_q, n_kv, tok):
    M, W = p.shape
    bpb = tok.blocks_per_batch
    tab = pl.BlockSpec((ROW_BLOCK, LANES), lambda i: (i % bpb, 0))
    wq, wk = n_q * HEAD_DIM, n_kv * HEAD_DIM
    return pl.pallas_call(
        functools.partial(_gqa_prep_body, n_q=n_q, n_kv=n_kv, scale=HEAD_DIM ** -0.5 * LOG2E),
        grid=(M // ROW_BLOCK,),
        in_specs=[pl.BlockSpec((ROW_BLOCK, W), lambda i: (i, 0)),
                  pl.BlockSpec((1, HEAD_DIM), lambda i: (0, 0)),
                  pl.BlockSpec((1, HEAD_DIM), lambda i: (0, 0)), tab, tab],
        out_specs=[pl.BlockSpec((ROW_BLOCK, wq), lambda i: (i, 0)),
                   pl.BlockSpec((ROW_BLOCK, wk), lambda i: (i, 0)),
                   pl.BlockSpec((ROW_BLOCK, 2 * wk), lambda i: (i, 0))],
        out_shape=[jax.ShapeDtypeStruct((M, wq), BF16), jax.ShapeDtypeStruct((M, wk), BF16),
                   jax.ShapeDtypeStruct((M, 2 * wk), BF16)],
        compiler_params=_cparams(1), name="gqa_prep")(
            p, q_gain.reshape(1, HEAD_DIM), k_gain.reshape(1, HEAD_DIM), *rope128)


def _lane_fold(x, op, init):
    for j in range(x.shape[1] // LANES):
        init = op(init, x[:, j * LANES:(j + 1) * LANES])
    return init


def _score_pass(q, k_ref, s_ref, *, n_k, bk, s_scale):
    m_lane = jnp.full((q.shape[0], LANES), -jnp.inf, F32)
    for c in range(n_k):
        s = lax.dot_general(q, k_ref[0, c * bk:(c + 1) * bk, :], (((1,), (1,)), ((), ())),
                            preferred_element_type=F32)
        if s_scale is not None:
            s = s * s_scale
        s_ref[c] = s
        m_lane = _lane_fold(s, jnp.maximum, m_lane)
    return jnp.max(m_lane, axis=1, keepdims=True)


def _value_pass(s_ref, m, v_ref, *, n_k, bk, v_has_ones):
    acc = jnp.zeros((s_ref.shape[1], HEAD_DIM + LANES), F32)
    for c in range(n_k):
        p = jnp.exp2(s_ref[c] - m)
        v1 = v_ref[0, c * bk:(c + 1) * bk, :]
        if not v_has_ones:
            v1 = jnp.concatenate([v1, jnp.ones((bk, LANES), BF16)], axis=-1)
        acc = acc + jnp.dot(p.astype(BF16), v1, preferred_element_type=F32)
    return acc[:, :HEAD_DIM] / acc[:, HEAD_DIM:HEAD_DIM + 1]


def _attn_body(q_ref, k_ref, v_ref, o_ref, s_sc, *, v_has_ones, n_k, bk, s_scale):
    m = _score_pass(q_ref[0], k_ref, s_sc, n_k=n_k, bk=bk, s_scale=s_scale)
    o_ref[0] = _value_pass(s_sc, m, v_ref, n_k=n_k, bk=bk, v_has_ones=v_has_ones).astype(o_ref.dtype)


def _attn_pipelined_body(q_ref, k_ref, v_ref, o_ref, sa, sb, ma, mb, *, n_k, bk, s_scale):
    i = pl.program_id(1)

    @pl.when(i == 0)
    def _():
        sb[...] = jnp.zeros(sb.shape, F32)
        mb[...] = jnp.zeros(mb.shape, F32)

    def step(s_w, m_w, s_r, m_r):
        m = _score_pass(q_ref[0], k_ref, s_w, n_k=n_k, bk=bk, s_scale=s_scale)
        m_w[...] = jnp.broadcast_to(m, m_w.shape)
        o_ref[0] = _value_pass(s_r, m_r[:, :1], v_ref, n_k=n_k, bk=bk, v_has_ones=True).astype(o_ref.dtype)

    @pl.when(i % 2 == 0)
    def _():
        step(sa, ma, sb, mb)

    @pl.when(i % 2 == 1)
    def _():
        step(sb, mb, sa, ma)


def _attention(q, k, v, *, n_heads, dq, dk, k_lane, v_lane, v_has_ones, bq, q_off, n_q, bk, k_off, n_k, name,
               pipelined, s_scale=None):
    B = q.shape[0]
    tk = n_k * bk
    dv = HEAD_DIM + LANES if v_has_ones else HEAD_DIM
    out_shape = jax.ShapeDtypeStruct((B, n_q * bq, n_heads * HEAD_DIM), BF16)
    if not pipelined:
        return pl.pallas_call(
            functools.partial(_attn_body, v_has_ones=v_has_ones, n_k=n_k, bk=bk, s_scale=s_scale),
            grid=(B, n_heads, n_q),
            in_specs=[pl.BlockSpec((1, bq, dq), lambda b, h, i: (b, i + q_off, h)),
                      pl.BlockSpec((1, tk, dk), lambda b, h, i: (b, k_off, k_lane(h))),
                      pl.BlockSpec((1, tk, dv), lambda b, h, i: (b, k_off, v_lane(h)))],
            out_specs=pl.BlockSpec((1, bq, HEAD_DIM), lambda b, h, i: (b, i, h)),
            out_shape=out_shape, scratch_shapes=[pltpu.VMEM((n_k, bq, bk), F32)],
            compiler_params=_cparams(3), name=name)(q, k, v)
    assert v_has_ones
    resident = dict(pipeline_mode=pl.Buffered(1))
    n_steps = n_heads * n_q

    def scored(t):
        t = jnp.minimum(t, n_steps - 1)
        return t // n_q, t % n_q

    def finished(t):
        t = jnp.maximum(t - 1, 0)
        return t // n_q, t % n_q

    return pl.pallas_call(
        functools.partial(_attn_pipelined_body, n_k=n_k, bk=bk, s_scale=s_scale),
        grid=(B, n_steps + 1),
        in_specs=[pl.BlockSpec((1, bq, dq), lambda b, t: (b, scored(t)[1] + q_off, scored(t)[0])),
                  pl.BlockSpec((1, tk, dk), lambda b, t: (b, k_off, k_lane(scored(t)[0])), **resident),
                  pl.BlockSpec((1, tk, dv), lambda b, t: (b, k_off, v_lane(finished(t)[0])), **resident)],
        out_specs=pl.BlockSpec((1, bq, HEAD_DIM), lambda b, t: (b, finished(t)[1], finished(t)[0])),
        out_shape=out_shape,
        scratch_shapes=[pltpu.VMEM((n_k, bq, bk), F32), pltpu.VMEM((n_k, bq, bk), F32),
                        pltpu.VMEM((bq, LANES), F32), pltpu.VMEM((bq, LANES), F32)],
        compiler_params=_cparams(2), name=name)(q, k, v)


NA_Q_ROWS = 8
NA_K_ROWS = 16
NA_CHUNK = 256
NA_N_CHUNKS = NA_K_ROWS * GRID_W // NA_CHUNK
NA_HEADS_PER_STEP = 2


def _na_body(*refs, scale):
    q_ref = refs[0]
    k_refs = refs[1:1 + NA_N_CHUNKS + 1]
    v_refs = refs[2 + NA_N_CHUNKS:3 + 2 * NA_N_CHUNKS]
    bias_ref = refs[3 + 2 * NA_N_CHUNKS]
    o_ref = refs[4 + 2 * NA_N_CHUNKS]
    for hh in range(NA_HEADS_PER_STEP):
        lanes = slice(hh * HEAD_DIM, (hh + 1) * HEAD_DIM)
        q = q_ref[0, :, lanes]
        parts = [lax.dot_general(q, kr[0, :, lanes], (((1,), (1,)), ((), ())), preferred_element_type=F32) * scale
                 for kr in k_refs]
        s_loc = jnp.concatenate(parts[:NA_N_CHUNKS], axis=1) + bias_ref[0, hh]
        s = jnp.concatenate([s_loc, parts[NA_N_CHUNKS]], axis=1)
        m = jnp.max(s, axis=1, keepdims=True)
        p = jnp.exp(s - m)
        l = jnp.sum(p, axis=1, keepdims=True)
        pb = p.astype(BF16)
        acc = None
        for j, vr in enumerate(v_refs):
            t = jnp.dot(pb[:, j * NA_CHUNK:(j + 1) * NA_CHUNK], vr[0, :, lanes], preferred_element_type=F32)
            acc = t if acc is None else acc + t
        o_ref[0, :, lanes] = (acc / l).astype(o_ref.dtype)


def _na_bias_tables(rpb, rows):
    n_groups = rows // NA_Q_ROWS
    n_h = rpb.shape[0]
    n_dr, n_dc = 2 * NA_WIN_H - 1, 2 * NA_WIN_W - 1
    exact = lax.Precision.HIGHEST
    qc = np.arange(GRID_W)[:, None]
    kc = np.arange(GRID_W)[None, :]
    cs = np.clip(qc - NA_WIN_W // 2, 0, GRID_W - NA_WIN_W)
    col_valid = (kc >= cs) & (kc < cs + NA_WIN_W)
    dc = np.clip(kc - qc + (NA_WIN_W - 1), 0, n_dc - 1)
    oh_c = (np.arange(n_dc)[:, None, None] == dc[None]).astype(np.float32).reshape(n_dc, GRID_W * GRID_W)
    cols = jnp.einsum('hrd,dx->hrx', rpb.astype(F32), jnp.asarray(oh_c), precision=exact)
    cols = jnp.where(col_valid.reshape(-1)[None, None], cols, NEG_BIG).reshape(n_h, n_dr, 1, GRID_W, 1, GRID_W)
    tabs = []
    j = np.arange(NA_Q_ROWS)[:, None]
    i = np.arange(NA_K_ROWS)[None, :]
    for g in (0, 1, n_groups - 1):
        qr = NA_Q_ROWS * g + j
        kr = np.clip(NA_Q_ROWS * g - NA_WIN_H // 2, 0, rows - NA_K_ROWS) + i
        rs = np.clip(qr - NA_WIN_H // 2, 0, rows - NA_WIN_H)
        row_valid = (kr >= rs) & (kr < rs + NA_WIN_H)
        dr = kr - qr + (NA_WIN_H - 1)
        t = jnp.full((n_h, NA_Q_ROWS, GRID_W, NA_K_ROWS, GRID_W), NEG_BIG, F32)
        for r in range(n_dr):
            pick = (row_valid & (dr == r))[None, :, None, :, None]
            if pick.any():
                t = jnp.where(pick, cols[:, r], t)
        tabs.append(t.reshape(n_h, NA_Q_ROWS * GRID_W, NA_K_ROWS * GRID_W))
    return jnp.stack(tabs)


def _na_attention(qkv, bias_tab, *, n_heads, seq, ctx_chunk):
    B = qkv.shape[0]
    rows = seq // GRID_W
    n_groups = rows // NA_Q_ROWS
    bq = NA_Q_ROWS * GRID_W
    max_cb = seq // NA_CHUNK - NA_N_CHUNKS

    def cb(g):
        return jnp.clip(2 * g - 1, 0, max_cb)

    hps = NA_HEADS_PER_STEP
    hw = hps * HEAD_DIM
    n_hblk = n_heads // hps
    in_specs = [pl.BlockSpec((1, bq, hw), lambda b, g, h: (b, g, h))]
    for part in (1, 2):
        for c in range(NA_N_CHUNKS):
            in_specs.append(pl.BlockSpec((1, NA_CHUNK, hw),
                                         lambda b, g, h, c=c, part=part: (b, cb(g) + c, part * n_hblk + h)))
        in_specs.append(pl.BlockSpec((1, NA_CHUNK, hw),
                                     lambda b, g, h, part=part: (b, ctx_chunk, part * n_hblk + h)))
    in_specs.append(pl.BlockSpec(
        (1, hps, bq, NA_K_ROWS * GRID_W),
        lambda b, g, h: (jnp.where(g == 0, 0, jnp.where(g == n_groups - 1, 2, 1)), h, 0, 0)))
    n_in = 2 * (NA_N_CHUNKS + 1)
    return pl.pallas_call(
        functools.partial(_na_body, scale=HEAD_DIM ** -0.5),
        grid=(B, n_groups, n_hblk), in_specs=in_specs,
        out_specs=pl.BlockSpec((1, bq, hw), lambda b, g, h: (b, g, h)),
        out_shape=jax.ShapeDtypeStruct((B, seq, n_heads * HEAD_DIM), BF16),
        compiler_params=_cparams(3), name="na_attention")(qkv, *([qkv] * n_in), bias_tab)


def _softplus(z):
    return jnp.maximum(z, 0.0) + jnp.log(1.0 + jnp.exp(-jnp.abs(z)))


def _lru_gates_body(x_ref, hp_ref, hn_ref, cw_ref, cb_ref, w_ref, ba_ref, bx_ref, lam_ref,
                    af_ref, bf_ref, ab_ref, bb_ref, xs_ref, *, bpb, lat_blocks, n_blocks_lru):
    i = pl.program_id(0)
    j = i % bpb
    first = jnp.logical_or(j == 0, j == lat_blocks)
    last = jnp.logical_or(j == lat_blocks - 1, j == bpb - 1)
    x = x_ref[...]
    xs_ref[SUBLANES:SUBLANES + ROW_BLOCK, :] = x
    xs_ref[0:SUBLANES, :] = jnp.where(first, 0.0, hp_ref[...])
    xs_ref[SUBLANES + ROW_BLOCK:, :] = jnp.where(last, 0.0, hn_ref[...])
    xc = cb_ref[...] + cw_ref[2:3, :] * x
    for tap, off in ((0, -2), (1, -1), (3, 1)):
        xc = xc + cw_ref[tap:tap + 1, :] * xs_ref[SUBLANES + off:SUBLANES + off + ROW_BLOCK, :]
    xcb = xc.astype(BF16)
    zs = [jnp.dot(xcb[:, n * LANES:(n + 1) * LANES], w_ref[n].astype(BF16), preferred_element_type=F32)
          for n in range(n_blocks_lru)]
    for d, (a_ref, b_ref) in enumerate(((af_ref, bf_ref), (ab_ref, bb_ref))):
        za = jnp.concatenate([z[:, (2 * d) * LANES:(2 * d + 1) * LANES] for z in zs], axis=1)
        zx = jnp.concatenate([z[:, (2 * d + 1) * LANES:(2 * d + 2) * LANES] for z in zs], axis=1)
        r = jax.nn.sigmoid(za + ba_ref[d:d + 1, :])
        ig = jax.nn.sigmoid(zx + bx_ref[d:d + 1, :])
        log_a = (-LRU_C) * r * _softplus(-lam_ref[d:d + 1, :])
        a = jnp.exp(log_a)
        a_ref[...] = a
        b_ref[...] = jnp.sqrt(1.0 - jnp.exp(2.0 * log_a)) * (ig * xc)


def _lru_gates(p_lru, conv_w, conv_b, w_cat, b_a, b_x, lam, tok):
    M = p_lru.shape[0]
    W = conv_b.shape[-1]
    nb = W // LANES
    rb8 = ROW_BLOCK // SUBLANES
    n_tiles8 = M // SUBLANES
    row = pl.BlockSpec((ROW_BLOCK, W), lambda i: (i, 0))
    vec2 = pl.BlockSpec((2, W), lambda i: (0, 0))
    return pl.pallas_call(
        functools.partial(_lru_gates_body, bpb=tok.blocks_per_batch, lat_blocks=tok.lat_blocks, n_blocks_lru=nb),
        grid=(M // ROW_BLOCK,),
        in_specs=[row,
                  pl.BlockSpec((SUBLANES, W), lambda i: (jnp.maximum(i * rb8 - 1, 0), 0)),
                  pl.BlockSpec((SUBLANES, W), lambda i: (jnp.minimum((i + 1) * rb8, n_tiles8 - 1), 0)),
                  pl.BlockSpec((LRU_CONV, W), lambda i: (0, 0)),
                  pl.BlockSpec((1, W), lambda i: (0, 0)),
                  pl.BlockSpec((nb, LANES, 4 * LANES), lambda i: (0, 0, 0)),
                  vec2, vec2, vec2],
        out_specs=[row] * 4,
        out_shape=[jax.ShapeDtypeStruct((M, W), F32)] * 4,
        scratch_shapes=[pltpu.VMEM((ROW_BLOCK + 2 * SUBLANES, W), F32)],
        compiler_params=_cparams(1), name="lru_gates")(
            p_lru, p_lru, p_lru, conv_w, conv_b.reshape(1, W), w_cat, b_a, b_x, lam)


LRU_UNROLL = 8


def _lru_scan_body(af_ref, bf_ref, ab_ref, bb_ref, hf_ref, hb_ref, carry_ref):
    @pl.when(pl.program_id(1) == 0)
    def _():
        carry_ref[...] = jnp.zeros(carry_ref.shape, F32)

    n = af_ref.shape[1]

    def step(t, hs):
        hf, hb = hs
        tb = n - 1 - t
        hf = af_ref[0, t] * hf + bf_ref[0, t]
        hb = ab_ref[0, tb] * hb + bb_ref[0, tb]
        hf_ref[0, t] = hf
        hb_ref[0, tb] = hb
        return hf, hb

    hf, hb = lax.fori_loop(0, n, step, (carry_ref[0], carry_ref[1]), unroll=LRU_UNROLL)
    carry_ref[0] = hf
    carry_ref[1] = hb


def _lru_scan(a_f, b_f, a_b, b_b, tok):
    B, T, R, _ = a_f.shape
    lat = tok.lat_blocks
    n_chunks = tok.blocks_per_batch

    def fwd(b, k):
        return (b, jnp.where(k == 0, lat, k - 1), 0, 0)

    def bwd(b, k):
        return (b, jnp.where(k == 0, lat, lat - k), 0, 0)

    blk = (1, ROW_BLOCK, R, LANES)
    return pl.pallas_call(
        _lru_scan_body, grid=(B, n_chunks),
        in_specs=[pl.BlockSpec(blk, fwd), pl.BlockSpec(blk, fwd), pl.BlockSpec(blk, bwd), pl.BlockSpec(blk, bwd)],
        out_specs=[pl.BlockSpec(blk, fwd), pl.BlockSpec(blk, bwd)],
        out_shape=[jax.ShapeDtypeStruct(a_f.shape, F32)] * 2,
        scratch_shapes=[pltpu.VMEM((2, R, LANES), F32)],
        compiler_params=_cparams(2), name="lru_scan")(a_f, b_f, a_b, b_b)


def _gelu_tanh(x):
    return 0.5 * x * (1.0 + jnp.tanh(math.sqrt(2.0 / math.pi) * (x + 0.044715 * (x * x * x))))


def _lru_out_body(hf_ref, hb_ref, g_ref, y_ref):
    y_ref[...] = ((hf_ref[...] + hb_ref[...]) * _gelu_tanh(g_ref[...])).astype(y_ref.dtype)


def _lru_out(hf, hb, p_lru):
    M, W = hf.shape
    row = pl.BlockSpec((ROW_BLOCK, W), lambda i: (i, 0))
    return pl.pallas_call(
        _lru_out_body, grid=(M // ROW_BLOCK,),
        in_specs=[row, row, pl.BlockSpec((ROW_BLOCK, W), lambda i: (i, 1))],
        out_specs=row, out_shape=jax.ShapeDtypeStruct((M, W), BF16),
        compiler_params=_cparams(1), name="lru_out")(hf, hb, p_lru)


def _merge_body(*refs, has_prev):
    if has_prev:
        h_ref, y_ref, wg_ref, wb_ref, bg_ref, prev_ref, o_ref = refs
    else:
        h_ref, y_ref, wg_ref, wb_ref, bg_ref, o_ref = refs
    gate = jax.nn.sigmoid(jnp.dot(h_ref[...], wg_ref[...], preferred_element_type=F32) + bg_ref[0])
    term = gate * jnp.dot(y_ref[...], wb_ref[...], preferred_element_type=F32)
    if has_prev:
        term = term + prev_ref[...]
    o_ref[...] = term.astype(o_ref.dtype)


def _merge_branch(h, y, w_gate, b_gate, w_branch, prev, *, layer, branch, bm, bn, out_dtype):
    M, D = h.shape
    kb = y.shape[1]
    in_specs = [pl.BlockSpec((bm, D), lambda n, m: (m, 0)),
                pl.BlockSpec((bm, kb), lambda n, m: (m, 0)),
                pl.BlockSpec((None, None, D, bn), lambda n, m: (layer, branch, 0, n)),
                pl.BlockSpec((None, None, kb, bn), lambda n, m: (layer, branch, 0, n)),
                pl.BlockSpec((None, 1, 1, bn), lambda n, m: (layer, branch, 0, n))]
    args = [h, y, w_gate, w_branch, b_gate.reshape(b_gate.shape[0], b_gate.shape[1], 1, D)]
    if prev is not None:
        in_specs.append(pl.BlockSpec((bm, bn), lambda n, m: (m, n)))
        args.append(prev)
    return pl.pallas_call(
        functools.partial(_merge_body, has_prev=prev is not None),
        grid=(D // bn, M // bm), in_specs=in_specs,
        out_specs=pl.BlockSpec((bm, bn), lambda n, m: (m, n)),
        out_shape=jax.ShapeDtypeStruct((M, D), out_dtype),
        compiler_params=_cparams(2), name=f"merge_branch{branch}")(*args)


ROUTER_BLOCK = 512
EXPERT_BLOCK = 512


def _router_body(h_ref, wr_ref, rb_ref, tri_ref, ids_ref, wts_ref, cnt_ref, carry_ref, *, n_exp):
    @pl.when(pl.program_id(0) == 0)
    def _():
        carry_ref[...] = jnp.zeros(carry_ref.shape, F32)

    per = n_exp // N_GROUPS
    logits = lax.dot_general(wr_ref[...], h_ref[...], (((1,), (1,)), ((), ())), preferred_element_type=F32)
    scores = jax.nn.sigmoid(logits)
    sel = scores + rb_ref[:, :1]
    sel_r = [sel[e:e + 1, :] for e in range(n_exp)]
    sc_r = [scores[e:e + 1, :] for e in range(n_exp)]
    gs = []
    for g in range(N_GROUPS):
        rows = sel_r[g * per:(g + 1) * per]
        best = None
        for a in range(per):
            for b in range(a + 1, per):
                ps = rows[a] + rows[b]
                best = ps if best is None else jnp.maximum(best, ps)
        gs.append(best)
    bestg = jnp.full(gs[0].shape, N_GROUPS - 1, I32)
    run = gs[N_GROUPS - 1]
    for g in range(N_GROUPS - 2, -1, -1):
        take = gs[g] >= run
        bestg = jnp.where(take, g, bestg)
        run = jnp.where(take, gs[g], run)
    v = [sel_r[i] for i in range(per)]
    sc = [sc_r[i] for i in range(per)]
    for g in range(1, N_GROUPS):
        isg = bestg == g
        v = [jnp.where(isg, sel_r[g * per + i], v[i]) for i in range(per)]
        sc = [jnp.where(isg, sc_r[g * per + i], sc[i]) for i in range(per)]

    def first_argmax(vals):
        idx = jnp.full(vals[0].shape, per - 1, I32)
        mx = vals[per - 1]
        sv = sc[per - 1]
        for i in range(per - 2, -1, -1):
            take = vals[i] >= mx
            idx = jnp.where(take, i, idx)
            mx = jnp.where(take, vals[i], mx)
            sv = jnp.where(take, sc[i], sv)
        return idx, sv

    i1, s1 = first_argmax(v)
    v2 = [jnp.where(i1 == i, -jnp.inf, v[i]) for i in range(per)]
    i2, s2 = first_argmax(v2)
    denom = s1 + s2
    e0 = bestg * per + i1
    e1 = bestg * per + i2
    eid = lax.broadcasted_iota(I32, scores.shape, 0)
    oh0 = (eid == e0).astype(F32)
    oh1 = (eid == e1).astype(F32)
    oh = oh0 + oh1
    prefix = jnp.dot(oh.astype(BF16), tri_ref[...], preferred_element_type=F32) + carry_ref[:, :1]
    r0 = jnp.sum(oh0 * prefix, axis=0, keepdims=True)
    r1 = jnp.sum(oh1 * prefix, axis=0, keepdims=True)
    new_carry = carry_ref[:, :1] + jnp.sum(oh, axis=1, keepdims=True)
    carry_ref[...] = jnp.broadcast_to(new_carry, carry_ref.shape)
    cnt_ref[...] = jnp.broadcast_to(new_carry, cnt_ref.shape)
    ids_ref[...] = jnp.zeros(ids_ref.shape, I32)
    wts_ref[...] = jnp.zeros(wts_ref.shape, F32)
    for r, val in enumerate((e0, e1, r0.astype(I32), r1.astype(I32))):
        ids_ref[r:r + 1, :] = val
    wts_ref[0:1, :] = s1 / denom
    wts_ref[1:2, :] = s2 / denom


def _router(h2, w_router, router_bias):
    M, D = h2.shape
    n_exp = w_router.shape[1]
    bm = ROUTER_BLOCK
    tri = jnp.asarray(np.triu(np.ones((bm, bm), np.float32), k=1), BF16)
    wr_t = w_router.T.astype(BF16)
    rb = jnp.broadcast_to(router_bias.astype(F32)[:, None], (n_exp, LANES))
    return pl.pallas_call(
        functools.partial(_router_body, n_exp=n_exp),
        grid=(M // bm,),
        in_specs=[pl.BlockSpec((bm, D), lambda i: (i, 0)),
                  pl.BlockSpec((n_exp, D), lambda i: (0, 0)),
                  pl.BlockSpec((n_exp, LANES), lambda i: (0, 0)),
                  pl.BlockSpec((bm, bm), lambda i: (0, 0))],
        out_specs=[pl.BlockSpec((SUBLANES, bm), lambda i: (0, i)),
                   pl.BlockSpec((SUBLANES, bm), lambda i: (0, i)),
                   pl.BlockSpec((n_exp, LANES), lambda i: (0, 0))],
        out_shape=[jax.ShapeDtypeStruct((SUBLANES, M), I32), jax.ShapeDtypeStruct((SUBLANES, M), F32),
                   jax.ShapeDtypeStruct((n_exp, LANES), F32)],
        scratch_shapes=[pltpu.VMEM((n_exp, LANES), F32)],
        compiler_params=_cparams(1), name="moe_router")(h2, wr_t, rb, tri)


DISPATCH_ROWS = 256


def _dispatch_body(n_live_ref, src_ref, h_hbm, xs_ref, sem):
    def copy(t):
        return pltpu.make_async_copy(h_hbm.at[pl.ds(src_ref[0, 0, t], 1)], xs_ref.at[pl.ds(t, 1)], sem)

    def start(t, c):
        copy(t).start()
        return c

    def wait(t, c):
        copy(t).wait()
        return c

    live = pl.program_id(0) * DISPATCH_ROWS < n_live_ref[0]

    @pl.when(live)
    def _():
        lax.fori_loop(0, DISPATCH_ROWS, start, 0)
        lax.fori_loop(0, DISPATCH_ROWS, wait, 0)

    @pl.when(jnp.logical_not(live))
    def _():
        xs_ref[...] = jnp.zeros(xs_ref.shape, xs_ref.dtype)


def _dispatch(h3, src, n_live, n_slots):
    slab = h3.shape[1:]
    return pl.pallas_call(
        _dispatch_body,
        grid_spec=pltpu.PrefetchScalarGridSpec(
            num_scalar_prefetch=1, grid=(n_slots // DISPATCH_ROWS,),
            in_specs=[pl.BlockSpec((1, 1, DISPATCH_ROWS), lambda i, nl: (i, 0, 0), memory_space=pltpu.SMEM),
                      pl.BlockSpec(memory_space=pl.ANY)],
            out_specs=pl.BlockSpec((DISPATCH_ROWS,) + slab, lambda i, nl: (i, 0, 0)),
            scratch_shapes=[pltpu.SemaphoreType.DMA(())]),
        out_shape=jax.ShapeDtypeStruct((n_slots,) + slab, h3.dtype),
        compiler_params=_cparams(1), name="moe_dispatch")(n_live, src, h3)


def _expert_up_body(be_ref, nu_ref, x_ref, wg_ref, wu_ref, o_ref, wg_bf, wu_bf):
    i = pl.program_id(1)
    fresh = jnp.logical_or(i == 0, be_ref[i] != be_ref[jnp.maximum(i - 1, 0)])

    @pl.when(jnp.logical_and(fresh, i < nu_ref[0]))
    def _():
        wg_bf[...] = wg_ref[...].astype(BF16)
        wu_bf[...] = wu_ref[...].astype(BF16)

    @pl.when(i < nu_ref[0])
    def _():
        x = x_ref[...]
        g = jnp.dot(x, wg_bf[...], preferred_element_type=F32)
        u = jnp.dot(x, wu_bf[...], preferred_element_type=F32)
        o_ref[...] = (g * jax.nn.sigmoid(g) * u).astype(o_ref.dtype)

    @pl.when(i >= nu_ref[0])
    def _():
        o_ref[...] = jnp.zeros(o_ref.shape, o_ref.dtype)


def _expert_up(xs, w_gate, w_up, blk_expert, n_used, *, layer, bn):
    P, D = xs.shape
    de = w_gate.shape[-1]
    nblk = P // EXPERT_BLOCK

    def xmap(n, i, be, nu):
        return (jnp.minimum(i, nu[0] - 1), 0)

    def wmap(n, i, be, nu):
        return (layer, be[i], 0, n)

    return pl.pallas_call(
        _expert_up_body,
        grid_spec=pltpu.PrefetchScalarGridSpec(
            num_scalar_prefetch=2, grid=(de // bn, nblk),
            in_specs=[pl.BlockSpec((EXPERT_BLOCK, D), xmap),
                      pl.BlockSpec((None, None, D, bn), wmap),
                      pl.BlockSpec((None, None, D, bn), wmap)],
            out_specs=pl.BlockSpec((EXPERT_BLOCK, bn), lambda n, i, be, nu: (i, n)),
            scratch_shapes=[pltpu.VMEM((D, bn), BF16), pltpu.VMEM((D, bn), BF16)]),
        out_shape=jax.ShapeDtypeStruct((P, de), BF16),
        compiler_params=_cparams(2), name="moe_expert_up")(blk_expert, n_used, xs, w_gate, w_up)


def _expert_down_body(be_ref, nu_ref, x_ref, wd_ref, o_ref, wd_bf):
    i = pl.program_id(1)
    fresh = jnp.logical_or(i == 0, be_ref[i] != be_ref[jnp.maximum(i - 1, 0)])

    @pl.when(jnp.logical_and(fresh, i < nu_ref[0]))
    def _():
        wd_bf[...] = wd_ref[...].astype(BF16)

    @pl.when(i < nu_ref[0])
    def _():
        o_ref[...] = jnp.dot(x_ref[...], wd_bf[...], preferred_element_type=F32).astype(o_ref.dtype)

    @pl.when(i >= nu_ref[0])
    def _():
        o_ref[...] = jnp.zeros(o_ref.shape, o_ref.dtype)


def _expert_down(he, w_down, blk_expert, n_used, *, layer, bn):
    P, de = he.shape
    D = w_down.shape[-1]
    nblk = P // EXPERT_BLOCK
    return pl.pallas_call(
        _expert_down_body,
        grid_spec=pltpu.PrefetchScalarGridSpec(
            num_scalar_prefetch=2, grid=(D // bn, nblk),
            in_specs=[pl.BlockSpec((EXPERT_BLOCK, de), lambda n, i, be, nu: (jnp.minimum(i, nu[0] - 1), 0)),
                      pl.BlockSpec((None, None, de, bn), lambda n, i, be, nu: (layer, be[i], 0, n))],
            out_specs=pl.BlockSpec((EXPERT_BLOCK, bn), lambda n, i, be, nu: (i, n)),
            scratch_shapes=[pltpu.VMEM((de, bn), BF16)]),
        out_shape=jax.ShapeDtypeStruct((P, D), F32),
        compiler_params=_cparams(2), name="moe_expert_down")(blk_expert, n_used, he, w_down)


COMBINE_UNROLL = 8


def _combine_body(pos_ref, w_ref, o_hbm, y_ref, buf, sem):
    def copy(t, k):
        return pltpu.make_async_copy(o_hbm.at[pl.ds(pos_ref[0, k, t], 1)], buf.at[k, pl.ds(t, 1)], sem)

    def start(t, c):
        copy(t, 0).start()
        copy(t, 1).start()
        return c

    def wait(t, c):
        copy(t, 0).wait()
        copy(t, 1).wait()
        return c

    def mix(t, c):
        y_ref[t] = w_ref[0, 0, t] * buf[0, t] + w_ref[0, 1, t] * buf[1, t]
        return c

    lax.fori_loop(0, DISPATCH_ROWS, start, 0)
    lax.fori_loop(0, DISPATCH_ROWS, wait, 0)
    lax.fori_loop(0, DISPATCH_ROWS, mix, 0, unroll=COMBINE_UNROLL)


def _combine(o3, pos, wts, n_rows):
    slab = o3.shape[1:]
    smem = pl.BlockSpec((1, 2, DISPATCH_ROWS), lambda i: (i, 0, 0), memory_space=pltpu.SMEM)
    return pl.pallas_call(
        _combine_body, grid=(n_rows // DISPATCH_ROWS,),
        in_specs=[smem, smem, pl.BlockSpec(memory_space=pl.ANY)],
        out_specs=pl.BlockSpec((DISPATCH_ROWS,) + slab, lambda i: (i, 0, 0)),
        out_shape=jax.ShapeDtypeStruct((n_rows,) + slab, F32),
        scratch_shapes=[pltpu.VMEM((2, DISPATCH_ROWS) + slab, F32), pltpu.SemaphoreType.DMA(())],
        compiler_params=_cparams(1), name="moe_combine")(pos, wts, o3)


def _moe(h2, w_router, router_bias, w_gate, w_up, w_down, *, layer):
    M, D = h2.shape
    n_exp = w_router.shape[1]
    ids, wts, cnt = _router(h2, w_router, router_bias)
    counts = cnt[:, 0].astype(I32)
    padded = (counts + EXPERT_BLOCK - 1) // EXPERT_BLOCK * EXPERT_BLOCK
    ends = jnp.cumsum(padded)
    offs = ends - padded
    n_slots = (2 * M + n_exp * (EXPERT_BLOCK - 1)) // EXPERT_BLOCK * EXPERT_BLOCK
    nblk = n_slots // EXPERT_BLOCK
    pos0 = offs[ids[0]] + ids[2]
    pos1 = offs[ids[1]] + ids[3]
    pos = jnp.stack([pos0.reshape(-1, DISPATCH_ROWS), pos1.reshape(-1, DISPATCH_ROWS)], axis=1)
    wpair = jnp.stack([wts[0].reshape(-1, DISPATCH_ROWS), wts[1].reshape(-1, DISPATCH_ROWS)], axis=1)
    blk_start = jnp.arange(nblk, dtype=I32) * EXPERT_BLOCK
    blk_expert = jnp.minimum(jnp.sum((ends[None, :] <= blk_start[:, None]).astype(I32), axis=1), n_exp - 1)
    n_used = (ends[-1] // EXPERT_BLOCK).astype(I32).reshape(1)
    tok_ids = jnp.arange(M, dtype=I32)
    src = jnp.zeros((n_slots,), I32).at[jnp.concatenate([pos0, pos1])].set(jnp.concatenate([tok_ids, tok_ids]))
    xs = _dispatch(h2.reshape(M, D // LANES, LANES), src.reshape(-1, 1, DISPATCH_ROWS),
                   ends[-1].astype(I32).reshape(1), n_slots).reshape(n_slots, D)
    he = _expert_up(xs, w_gate, w_up, blk_expert, n_used, layer=layer, bn=_pick(w_gate.shape[-1], (256, 128)))
    o = _expert_down(he, w_down, blk_expert, n_used, layer=layer, bn=_pick(D, (2048, 1024, 512)))
    return _combine(o.reshape(n_slots, D // LANES, LANES), pos, wpair, M).reshape(M, D)


class _Tok:
    def __init__(self, n_batch, seq, ctx):
        self.n_batch, self.seq, self.ctx = n_batch, seq, ctx
        self.t_all = seq + ctx
        self.lat_blocks = seq // ROW_BLOCK
        self.blocks_per_batch = self.t_all // ROW_BLOCK


def _rope_tables(seq, ctx):
    t = jnp.arange(seq, dtype=I32)
    row = (t // GRID_W).astype(F32)[:, None]
    col = (t % GRID_W).astype(F32)[:, None]

    def cs(rot_dim):
        n_freq = rot_dim // 4
        freqs = ROPE_THETA ** (-jnp.arange(n_freq, dtype=F32) / n_freq)
        ang = jnp.concatenate([row * freqs, col * freqs], axis=-1)
        cos = jnp.concatenate([jnp.cos(ang), jnp.ones((ctx, rot_dim // 2), F32)], axis=0)
        sin = jnp.concatenate([jnp.sin(ang), jnp.zeros((ctx, rot_dim // 2), F32)], axis=0)
        return cos, sin

    cos, sin = cs(HEAD_DIM)
    rope128 = (jnp.concatenate([cos, cos], axis=1), jnp.concatenate([-sin, sin], axis=1))
    cos, sin = cs(MLA_ROPE)
    z32 = jnp.zeros_like(sin)
    z64 = jnp.zeros((seq + ctx, LANES - MLA_ROPE), F32)
    rope64 = (jnp.concatenate([cos, cos, z64], axis=1), jnp.concatenate([sin, z32, z64], axis=1),
              jnp.concatenate([z32, sin, z64], axis=1))
    return rope128, rope64


def _q_rope_epilogue(acc, c_ref, slo_ref, shi_ref, *, scale):
    nope = acc[:, :MLA_NOPE] * scale
    rope = _rope64(acc[:, MLA_NOPE:], c_ref[...], slo_ref[...], shi_ref[...]) * scale
    return jnp.concatenate([nope, rope], axis=1)


def _kv_assemble_epilogue(acc, krr_ref):
    keys = jnp.concatenate([acc[:, :MLA_NOPE], krr_ref[...].astype(F32)], axis=1)
    vals = jnp.concatenate([acc[:, MLA_NOPE:], jnp.ones((acc.shape[0], LANES), F32)], axis=1)
    return keys, vals


def kernel(x, c, ctx, c_ctx, w_mod, b_mod, norm_mix, norm_ffn, w_in, mla_q_norm, mla_w_uq, mla_kv_norm,
           mla_w_ukv, gqa_q_norm, gqa_k_norm, na_rpb, lru_conv_w, lru_conv_b, lru_w_a, lru_b_a, lru_w_x,
           lru_b_x, lru_lambda, w_branch_gate, b_branch_gate, w_branch, w_out, w_router, router_bias,
           w_exp_gate, w_exp_up, w_exp_down, final_norm):
    B, S, D = x.shape
    CTX = ctx.shape[1]
    depth = w_in.shape[0]
    tok = _Tok(B, S, CTX)
    T = tok.t_all
    M = B * T
    bw = D // 4
    n_heads = bw // HEAD_DIM
    ql, kvl = mla_q_norm.shape[-1], mla_kv_norm.shape[-1]
    d_in = w_in.shape[-1]
    n_kv = (d_in - (ql + kvl + MLA_ROPE + bw + 3 * bw + 2 * bw)) // (2 * HEAD_DIM)
    q_per_kv = n_heads // n_kv
    n_lru_blocks = lru_w_a.shape[2]
    assert bw // n_lru_blocks == LANES and S % ROW_BLOCK == 0 and CTX == ROW_BLOCK and S % (GRID_W * NA_Q_ROWS) == 0
    mla_dq = 2 * LANES
    mla_scale = (MLA_NOPE + MLA_ROPE) ** -0.5 * LOG2E
    bm = _pick(M, (768, 512, 256))
    bm_tok = _pick(T, (768, 256))

    rope128, rope64 = _rope_tables(S, CTX)
    xa = jnp.concatenate([x, ctx], axis=1).reshape(M, D)
    cc = jnp.zeros((MOD_ROWS, D), F32).at[:B].set(c).at[B].set(c_ctx)
    silu_c = (cc * jax.nn.sigmoid(cc)).astype(BF16)

    o_mla = ql + kvl
    o_gqa = o_mla + MLA_ROPE
    w_gqa = bw + 2 * n_kv * HEAD_DIM
    o_na = o_gqa + w_gqa
    o_lru = o_na + 3 * bw

    bn_mod = _pick(N_MOD * D, (512, 256))
    mods = []
    for layer in range(depth):
        modv = _matmul(silu_c, w_mod, w_prefix=(layer,), n_cols=N_MOD * D, bm=MOD_ROWS, bn=bn_mod, out_dtype=F32,
                       epilogue=lambda acc, b_ref: acc + b_ref[0],
                       extras=[(b_mod.reshape(depth, 1, N_MOD * D), (1, 1, bn_mod),
                                lambda n, m, layer=layer: (layer, 0, n))],
                       name="mod_proj")
        mods.append(modv[:B + 1].reshape(B + 1, N_MOD, D))
    wg_bf, wb_bf = w_branch_gate.astype(BF16), w_branch.astype(BF16)

    y_ffn = mod_prev = None
    for layer in range(depth):
        need_ctx = layer < depth - 1
        modv = mods[layer]
        xa, h = _resid_norm(xa, y_ffn, mod_prev, modv, norm_mix[layer], gate_idx=5, shift_idx=0, scale_idx=1,
                            out_dtype=BF16, tok=tok, name="norm_mix")
        cqkv = _matmul(h, w_in, w_prefix=(layer,), n_cols=o_mla, bm=bm, bn=_pick_bn(o_mla, 0), out_dtype=F32,
                       name="in_mla")
        w_kr = jnp.pad(w_in[layer, :, o_mla:o_gqa], ((0, 0), (0, LANES - MLA_ROPE)))
        kr = _matmul(h, w_kr, n_cols=LANES, bm=bm, bn=LANES, out_dtype=F32, name="in_kr")
        w_rest = w_in[layer, :, o_gqa:o_lru]
        p_gqa = _matmul(h, w_rest, n_cols=w_gqa, bm=bm, bn=_pick_bn(w_gqa, 0), out_dtype=F32, name="in_gqa")
        p_na = _matmul(h, w_rest, n_off=w_gqa, n_cols=3 * bw, bm=bm, bn=_pick_bn(3 * bw, w_gqa),
                       out_dtype=BF16, name="in_na")
        p_lru = _matmul(h, w_in[layer, :, o_lru:], n_cols=2 * bw, bm=bm, bn=_pick_bn(2 * bw, 0), out_dtype=F32,
                        name="in_lru")

        qn, kvn, krr = _mla_prep(cqkv, kr, mla_q_norm[layer], mla_kv_norm[layer], rope64, tok)
        w_uq = jnp.pad(mla_w_uq[layer].reshape(ql, n_heads, MLA_NOPE + MLA_ROPE),
                       ((0, 0), (0, 0), (0, mla_dq - MLA_NOPE - MLA_ROPE))).reshape(ql, n_heads * mla_dq)
        tpb = T // bm_tok
        tab_spec = ((bm_tok, LANES), lambda n, m: (m % tpb, 0))
        q_mla = _matmul(qn, w_uq, n_cols=n_heads * mla_dq, bm=bm_tok, bn=mla_dq, out_dtype=BF16,
                        epilogue=functools.partial(_q_rope_epilogue, scale=mla_scale),
                        extras=[(t, *tab_spec) for t in rope64], name="mla_uq")
        k_mla, v_mla = _matmul(kvn, mla_w_ukv, w_prefix=(layer,), n_cols=n_heads * mla_dq, bm=bm, bn=mla_dq,
                               out_dtype=BF16, n_out=2, epilogue=_kv_assemble_epilogue,
                               extras=[(krr, (bm, LANES), lambda n, m: (m, 0))], name="mla_ukv")
        bq = _pick(S, (512, 256))
        bk = _pick(T, (768, 256))
        ctx_blk = S // ROW_BLOCK
        lat_q = dict(bq=bq, q_off=0, n_q=S // bq, bk=bk, k_off=0, n_k=T // bk, pipelined=True)
        q3, k3, v3 = (a.reshape(B, T, -1) for a in (q_mla, k_mla, v_mla))
        mla_args = dict(n_heads=n_heads, dq=mla_dq, dk=mla_dq, k_lane=lambda h: h, v_lane=lambda h: h,
                        v_has_ones=True)
        ya = _attention(q3, k3, v3, name="mla_attn", **lat_q, **mla_args)
        q_g, k_g, v_g = (a.reshape(B, T, -1) for a in
                         _gqa_prep(p_gqa, gqa_q_norm[layer], gqa_k_norm[layer], rope128, n_heads, n_kv, tok))
        gqa_args = dict(n_heads=n_heads, dq=HEAD_DIM, dk=HEAD_DIM, k_lane=lambda h: h // q_per_kv,
                        v_lane=lambda h: h // q_per_kv, v_has_ones=True)
        yb = _attention(q_g, k_g, v_g, name="gqa_attn", **lat_q, **gqa_args)
        na3 = p_na.reshape(B, T, 3 * bw)
        yc = _na_attention(na3, _na_bias_tables(na_rpb[layer], S // GRID_W), n_heads=n_heads, seq=S,
                           ctx_chunk=S // NA_CHUNK)
        if need_ctx:
            ctx_q = dict(bq=ROW_BLOCK, q_off=ctx_blk, n_q=1, bk=ROW_BLOCK, k_off=ctx_blk, n_k=1, pipelined=False)
            ya_c = _attention(q3, k3, v3, name="mla_attn_ctx", **ctx_q, **mla_args)
            yb_c = _attention(q_g, k_g, v_g, name="gqa_attn_ctx", **ctx_q, **gqa_args)
            yc_c = _attention(na3, na3, na3, name="na_attn_ctx", n_heads=n_heads, dq=HEAD_DIM, dk=HEAD_DIM,
                              k_lane=lambda h: n_heads + h, v_lane=lambda h: 2 * n_heads + h, v_has_ones=False,
                              s_scale=HEAD_DIM ** -0.5 * LOG2E, **ctx_q)
        else:
            ya_c = yb_c = yc_c = jnp.zeros((B, CTX, bw), BF16)
        ya, yb, yc = (jnp.concatenate([l_, c_], axis=1).reshape(M, bw)
                      for l_, c_ in ((ya, ya_c), (yb, yb_c), (yc, yc_c)))
        w_cat = jnp.concatenate([lru_w_a[layer, 0], lru_w_x[layer, 0], lru_w_a[layer, 1], lru_w_x[layer, 1]],
                                axis=-1)
        gates = _lru_gates(p_lru, lru_conv_w[layer], lru_conv_b[layer], w_cat, lru_b_a[layer], lru_b_x[layer],
                           lru_lambda[layer], tok)
        hf, hb = _lru_scan(*(g.reshape(B, T, n_lru_blocks, LANES) for g in gates), tok)
        yd = _lru_out(hf.reshape(M, bw), hb.reshape(M, bw), p_lru)

        bn_merge = _pick(D, (1024, 512, 256))
        bm_merge = _pick(M, (512, 256))
        merged = None
        for i_br, y_br in enumerate((ya, yb, yc, yd)):
            merged = _merge_branch(h, y_br, wg_bf, b_branch_gate, wb_bf, merged, layer=layer,
                                   branch=i_br, bm=bm_merge, bn=bn_merge,
                                   out_dtype=BF16 if i_br == 3 else F32)
        y = _matmul(merged, w_out, w_prefix=(layer,), n_cols=D, bm=bm, bn=_pick_bn(D, 0), out_dtype=F32,
                    name="out_proj")
        xa, h2 = _resid_norm(xa, y, modv, modv, norm_ffn[layer], gate_idx=2, shift_idx=3, scale_idx=4,
                             out_dtype=BF16, tok=tok, name="resid_norm_ffn")
        y_ffn = _moe(h2, w_router, router_bias, w_exp_gate, w_exp_up, w_exp_down, layer=layer)
        mod_prev = modv

    _, out = _resid_norm(xa, y_ffn, mod_prev, None, final_norm, gate_idx=5, shift_idx=None, scale_idx=None,
                         out_dtype=F32, tok=tok, name="final_norm")
    return out.reshape(B, T, D)[:, :S]
```

```python
import functools
import math

import numpy as np
import jax
import jax.numpy as jnp
from jax import lax
from jax.experimental import pallas as pl
from jax.experimental.pallas import tpu as pltpu

F32 = jnp.float32
BF16 = jnp.bfloat16
I32 = jnp.int32

HEAD_DIM = 128
GRID_W = 64
ROPE_THETA = 10000.0
NORM_EPS = 1e-6
MLA_NOPE = 128
MLA_ROPE = 64
NA_WIN_H = 8
NA_WIN_W = 16
LRU_CONV = 4
LRU_C = 8.0
N_GROUPS = 4
N_MOD = 6

LANES = 128
SUBLANES = 8
MOD_ROWS = 16
VMEM_LIMIT_BYTES = 58 * 1024 * 1024
ROW_BLOCK = 256
NEG_BIG = -1e30
LOG2E = math.log2(math.e)


def _pick(n, prefs):
    for p in prefs:
        if n % p == 0:
            return p
    return n


def _pick_bn(n_cols, n_off):
    return _pick(math.gcd(n_cols, n_off) if n_off else n_cols, (1024, 768, 512, 256, 128))


def _cparams(n_axes):
    return pltpu.CompilerParams(dimension_semantics=("arbitrary",) * n_axes,
                                vmem_limit_bytes=VMEM_LIMIT_BYTES)


def _mm_body(*refs, n_extra, n_out, epilogue, cache_w):
    a_ref, w_ref = refs[0], refs[1]
    extra = refs[2:2 + n_extra]
    o_refs = refs[2 + n_extra:2 + n_extra + n_out]
    if cache_w:
        wbf_ref = refs[2 + n_extra + n_out]

        @pl.when(pl.program_id(1) == 0)
        def _():
            wbf_ref[...] = w_ref[...].astype(BF16)

        w = wbf_ref[...]
    else:
        w = w_ref[...].astype(BF16)
    acc = jnp.dot(a_ref[...], w, preferred_element_type=F32)
    if epilogue is not None:
        acc = epilogue(acc, *extra)
    outs = acc if n_out > 1 else (acc,)
    for o_ref, val in zip(o_refs, outs):
        o_ref[...] = val.astype(o_ref.dtype)


def _matmul(a, w, *, n_cols, bm, bn, out_dtype, w_prefix=(), n_off=0, epilogue=None, extras=(), n_out=1, name):
    M, K = a.shape
    assert M % bm == 0 and n_cols % bn == 0 and n_off % bn == 0
    nb_off = n_off // bn
    grid = (n_cols // bn, M // bm)
    w_block = (None,) * len(w_prefix) + (K, bn)
    cache_w = (w.dtype != BF16) and grid[1] > 1
    w_mode = dict(pipeline_mode=pl.Buffered(1)) if cache_w else {}
    in_specs = [pl.BlockSpec((bm, K), lambda n, m: (m, 0)),
                pl.BlockSpec(w_block, lambda n, m: (*w_prefix, 0, n + nb_off), **w_mode)]
    args = [a, w]
    for arr, blk, imap in extras:
        in_specs.append(pl.BlockSpec(blk, imap))
        args.append(arr)
    scratch = [pltpu.VMEM((K, bn), BF16)] if cache_w else []
    out_spec = pl.BlockSpec((bm, bn), lambda n, m: (m, n))
    out_sds = jax.ShapeDtypeStruct((M, n_cols), out_dtype)
    res = pl.pallas_call(
        functools.partial(_mm_body, n_extra=len(extras), n_out=n_out, epilogue=epilogue, cache_w=cache_w),
        grid=grid, in_specs=in_specs, out_specs=[out_spec] * n_out, out_shape=[out_sds] * n_out,
        scratch_shapes=scratch, compiler_params=_cparams(2), name=name)(*args)
    return res if n_out > 1 else res[0]


def _rms(x, gain):
    return x * lax.rsqrt(jnp.mean(x * x, axis=-1, keepdims=True) + NORM_EPS) * gain


def _resid_norm_body(*refs, has_y, has_mod, gate_idx, shift_idx, scale_idx):
    it = iter(refs)
    x_ref = next(it)
    y_ref, modg_ref = (next(it), next(it)) if has_y else (None, None)
    modn_ref = next(it) if has_mod else None
    gain_ref = next(it)
    xo_ref = next(it) if has_y else None
    h_ref = next(it)
    x = x_ref[...]
    if has_y:
        x = x + modg_ref[0, gate_idx:gate_idx + 1, :] * y_ref[...]
        xo_ref[...] = x
    h = _rms(x, gain_ref[...])
    if has_mod:
        h = h * (1.0 + modn_ref[0, scale_idx:scale_idx + 1, :]) + modn_ref[0, shift_idx:shift_idx + 1, :]
    h_ref[...] = h.astype(h_ref.dtype)


def _mod_row_map(blocks_per_batch, lat_blocks, n_batch):
    def imap(i):
        b = i // blocks_per_batch
        j = i % blocks_per_batch
        return (jnp.where(j < lat_blocks, b, n_batch), 0, 0)
    return imap


def _resid_norm(x, y, mod_gate, mod_norm, gain, *, gate_idx, shift_idx, scale_idx, out_dtype, tok, name):
    M, D = x.shape
    nblk = M // ROW_BLOCK
    row = pl.BlockSpec((ROW_BLOCK, D), lambda i: (i, 0))
    mod_spec = pl.BlockSpec((1, N_MOD, D), _mod_row_map(tok.blocks_per_batch, tok.lat_blocks, tok.n_batch))
    in_specs, args = [row], [x]
    if y is not None:
        in_specs += [row, mod_spec]
        args += [y, mod_gate]
    if mod_norm is not None:
        in_specs.append(mod_spec)
        args.append(mod_norm)
    in_specs.append(pl.BlockSpec((1, D), lambda i: (0, 0)))
    args.append(gain.reshape(1, D))
    out_specs, out_shape = [], []
    if y is not None:
        out_specs.append(row)
        out_shape.append(jax.ShapeDtypeStruct((M, D), F32))
    out_specs.append(row)
    out_shape.append(jax.ShapeDtypeStruct((M, D), out_dtype))
    res = pl.pallas_call(
        functools.partial(_resid_norm_body, has_y=y is not None, has_mod=mod_norm is not None, gate_idx=gate_idx,
                          shift_idx=shift_idx, scale_idx=scale_idx),
        grid=(nblk,), in_specs=in_specs, out_specs=out_specs, out_shape=out_shape,
        compiler_params=_cparams(1), name=name)(*args)
    return res if y is not None else (x, res[0])


def _rope64(x, c, slo, shi):
    return x * c - pltpu.roll(x, 96, axis=1) * slo + pltpu.roll(x, 32, axis=1) * shi


def _rope128(x, c, ss):
    return x * c + pltpu.roll(x, 64, axis=1) * ss


def _mla_prep_body(cqkv_ref, kr_ref, qg_ref, kvg_ref, c_ref, slo_ref, shi_ref, qn_ref, kvn_ref, krr_ref, *, ql):
    cqkv = cqkv_ref[...]
    qn_ref[...] = _rms(cqkv[:, :ql], qg_ref[...]).astype(BF16)
    kvn_ref[...] = _rms(cqkv[:, ql:], kvg_ref[...]).astype(BF16)
    krr_ref[...] = _rope64(kr_ref[...], c_ref[...], slo_ref[...], shi_ref[...]).astype(BF16)


def _mla_prep(cqkv, kr, q_gain, kv_gain, rope64, tok):
    M, W = cqkv.shape
    ql, kvl = q_gain.shape[-1], kv_gain.shape[-1]
    bpb = tok.blocks_per_batch
    tab = pl.BlockSpec((ROW_BLOCK, LANES), lambda i: (i % bpb, 0))
    return pl.pallas_call(
        functools.partial(_mla_prep_body, ql=ql),
        grid=(M // ROW_BLOCK,),
        in_specs=[pl.BlockSpec((ROW_BLOCK, W), lambda i: (i, 0)),
                  pl.BlockSpec((ROW_BLOCK, LANES), lambda i: (i, 0)),
                  pl.BlockSpec((1, ql), lambda i: (0, 0)),
                  pl.BlockSpec((1, kvl), lambda i: (0, 0)), tab, tab, tab],
        out_specs=[pl.BlockSpec((ROW_BLOCK, ql), lambda i: (i, 0)),
                   pl.BlockSpec((ROW_BLOCK, kvl), lambda i: (i, 0)),
                   pl.BlockSpec((ROW_BLOCK, LANES), lambda i: (i, 0))],
        out_shape=[jax.ShapeDtypeStruct((M, ql), BF16), jax.ShapeDtypeStruct((M, kvl), BF16),
                   jax.ShapeDtypeStruct((M, LANES), BF16)],
        compiler_params=_cparams(1), name="mla_prep")(
            cqkv, kr, q_gain.reshape(1, ql), kv_gain.reshape(1, kvl), *rope64)


def _gqa_prep_body(p_ref, qg_ref, kg_ref, c_ref, ss_ref, q_ref, k_ref, v_ref, *, n_q, n_kv, scale):
    c, ss = c_ref[...], ss_ref[...]
    for h in range(n_q):
        xh = p_ref[:, h * HEAD_DIM:(h + 1) * HEAD_DIM]
        q_ref[:, h * HEAD_DIM:(h + 1) * HEAD_DIM] = (_rope128(_rms(xh, qg_ref[...]), c, ss) * scale).astype(BF16)
    off = n_q * HEAD_DIM
    for h in range(n_kv):
        xh = p_ref[:, off + h * HEAD_DIM:off + (h + 1) * HEAD_DIM]
        k_ref[:, h * HEAD_DIM:(h + 1) * HEAD_DIM] = _rope128(_rms(xh, kg_ref[...]), c, ss).astype(BF16)
    off += n_kv * HEAD_DIM
    for h in range(n_kv):
        v_ref[:, 2 * h * HEAD_DIM:(2 * h + 1) * HEAD_DIM] = p_ref[:, off + h * HEAD_DIM:off + (h + 1) * HEAD_DIM].astype(BF16)
        v_ref[:, (2 * h + 1) * HEAD_DIM:(2 * h + 2) * HEAD_DIM] = jnp.ones((ROW_BLOCK, HEAD_DIM), BF16)


def _gqa_prep(p, q_gain, k_gain, rope128, n_q, n_kv, tok):
    M, W = p.shape
    bpb = tok.blocks_per_batch
    tab = pl.BlockSpec((ROW_BLOCK, LANES), lambda i: (i % bpb, 0))
    wq, wk = n_q * HEAD_DIM, n_kv * HEAD_DIM
    return pl.pallas_call(
        functools.partial(_gqa_prep_body, n_q=n_q, n_kv=n_kv, scale=HEAD_DIM ** -0.5 * LOG2E),
        grid=(M // ROW_BLOCK,),
        in_specs=[pl.BlockSpec((ROW_BLOCK, W), lambda i: (i, 0)),
                  pl.BlockSpec((1, HEAD_DIM), lambda i: (0, 0)),
                  pl.BlockSpec((1, HEAD_DIM), lambda i: (0, 0)), tab, tab],
        out_specs=[pl.BlockSpec((ROW_BLOCK, wq), lambda i: (i, 0)),
                   pl.BlockSpec((ROW_BLOCK, wk), lambda i: (i, 0)),
                   pl.BlockSpec((ROW_BLOCK, 2 * wk), lambda i: (i, 0))],
        out_shape=[jax.ShapeDtypeStruct((M, wq), BF16), jax.ShapeDtypeStruct((M, wk), BF16),
                   jax.ShapeDtypeStruct((M, 2 * wk), BF16)],
        compiler_params=_cparams(1), name="gqa_prep")(
            p, q_gain.reshape(1, HEAD_DIM), k_gain.reshape(1, HEAD_DIM), *rope128)


def _lane_fold(x, op, init):
    for j in range(x.shape[1] // LANES):
        init = op(init, x[:, j * LANES:(j + 1) * LANES])
    return init


def _score_pass(q, k_ref, s_ref, *, n_k, bk, s_scale):
    m_lane = jnp.full((q.shape[0], LANES), -jnp.inf, F32)
    for c in range(n_k):
        s = lax.dot_general(q, k_ref[0, c * bk:(c + 1) * bk, :], (((1,), (1,)), ((), ())),
                            preferred_element_type=F32)
        if s_scale is not None:
            s = s * s_scale
        s_ref[c] = s
        m_lane = _lane_fold(s, jnp.maximum, m_lane)
    return jnp.max(m_lane, axis=1, keepdims=True)


def _value_pass(s_ref, m, v_ref, *, n_k, bk, v_has_ones):
    acc = jnp.zeros((s_ref.shape[1], HEAD_DIM + LANES), F32)
    for c in range(n_k):
        p = jnp.exp2(s_ref[c] - m)
        v1 = v_ref[0, c * bk:(c + 1) * bk, :]
        if not v_has_ones:
            v1 = jnp.concatenate([v1, jnp.ones((bk, LANES), BF16)], axis=-1)
        acc = acc + jnp.dot(p.astype(BF16), v1, preferred_element_type=F32)
    return acc[:, :HEAD_DIM] / acc[:, HEAD_DIM:HEAD_DIM + 1]


def _attn_body(q_ref, k_ref, v_ref, o_ref, s_sc, *, v_has_ones, n_k, bk, s_scale):
    m = _score_pass(q_ref[0], k_ref, s_sc, n_k=n_k, bk=bk, s_scale=s_scale)
    o_ref[0] = _value_pass(s_sc, m, v_ref, n_k=n_k, bk=bk, v_has_ones=v_has_ones).astype(o_ref.dtype)


def _attn_pipelined_body(q_ref, k_ref, v_ref, o_ref, sa, sb, ma, mb, *, n_k, bk, s_scale):
    i = pl.program_id(1)

    @pl.when(i == 0)
    def _():
        sb[...] = jnp.zeros(sb.shape, F32)
        mb[...] = jnp.zeros(mb.shape, F32)

    def step(s_w, m_w, s_r, m_r):
        m = _score_pass(q_ref[0], k_ref, s_w, n_k=n_k, bk=bk, s_scale=s_scale)
        m_w[...] = jnp.broadcast_to(m, m_w.shape)
        o_ref[0] = _value_pass(s_r, m_r[:, :1], v_ref, n_k=n_k, bk=bk, v_has_ones=True).astype(o_ref.dtype)

    @pl.when(i % 2 == 0)
    def _():
        step(sa, ma, sb, mb)

    @pl.when(i % 2 == 1)
    def _():
        step(sb, mb, sa, ma)


def _attention(q, k, v, *, n_heads, dq, dk, k_lane, v_lane, v_has_ones, bq, q_off, n_q, bk, k_off, n_k, name,
               pipelined, s_scale=None):
    B = q.shape[0]
    tk = n_k * bk
    dv = HEAD_DIM + LANES if v_has_ones else HEAD_DIM
    out_shape = jax.ShapeDtypeStruct((B, n_q * bq, n_heads * HEAD_DIM), BF16)
    if not pipelined:
        return pl.pallas_call(
            functools.partial(_attn_body, v_has_ones=v_has_ones, n_k=n_k, bk=bk, s_scale=s_scale),
            grid=(B, n_heads, n_q),
            in_specs=[pl.BlockSpec((1, bq, dq), lambda b, h, i: (b, i + q_off, h)),
                      pl.BlockSpec((1, tk, dk), lambda b, h, i: (b, k_off, k_lane(h))),
                      pl.BlockSpec((1, tk, dv), lambda b, h, i: (b, k_off, v_lane(h)))],
            out_specs=pl.BlockSpec((1, bq, HEAD_DIM), lambda b, h, i: (b, i, h)),
            out_shape=out_shape, scratch_shapes=[pltpu.VMEM((n_k, bq, bk), F32)],
            compiler_params=_cparams(3), name=name)(q, k, v)
    assert v_has_ones
    resident = dict(pipeline_mode=pl.Buffered(1))
    n_steps = n_heads * n_q

    def scored(t):
        t = jnp.minimum(t, n_steps - 1)
        return t // n_q, t % n_q

    def finished(t):
        t = jnp.maximum(t - 1, 0)
        return t // n_q, t % n_q

    return pl.pallas_call(
        functools.partial(_attn_pipelined_body, n_k=n_k, bk=bk, s_scale=s_scale),
        grid=(B, n_steps + 1),
        in_specs=[pl.BlockSpec((1, bq, dq), lambda b, t: (b, scored(t)[1] + q_off, scored(t)[0])),
                  pl.BlockSpec((1, tk, dk), lambda b, t: (b, k_off, k_lane(scored(t)[0])), **resident),
                  pl.BlockSpec((1, tk, dv), lambda b, t: (b, k_off, v_lane(finished(t)[0])), **resident)],
        out_specs=pl.BlockSpec((1, bq, HEAD_DIM), lambda b, t: (b, finished(t)[1], finished(t)[0])),
        out_shape=out_shape,
        scratch_shapes=[pltpu.VMEM((n_k, bq, bk), F32), pltpu.VMEM((n_k, bq, bk), F32),
                        pltpu.VMEM((bq, LANES), F32), pltpu.VMEM((bq, LANES), F32)],
        compiler_params=_cparams(2), name=name)(q, k, v)


NA_Q_ROWS = 8
NA_K_ROWS = 16
NA_CHUNK = 256
NA_N_CHUNKS = NA_K_ROWS * GRID_W // NA_CHUNK
NA_HEADS_PER_STEP = 2


def _na_body(*refs, scale):
    q_ref = refs[0]
    k_refs = refs[1:1 + NA_N_CHUNKS + 1]
    v_refs = refs[2 + NA_N_CHUNKS:3 + 2 * NA_N_CHUNKS]
    bias_ref = refs[3 + 2 * NA_N_CHUNKS]
    o_ref = refs[4 + 2 * NA_N_CHUNKS]
    for hh in range(NA_HEADS_PER_STEP):
        lanes = slice(hh * HEAD_DIM, (hh + 1) * HEAD_DIM)
        q = q_ref[0, :, lanes]
        parts = [lax.dot_general(q, kr[0, :, lanes], (((1,), (1,)), ((), ())), preferred_element_type=F32) * scale
                 for kr in k_refs]
        s_loc = jnp.concatenate(parts[:NA_N_CHUNKS], axis=1) + bias_ref[0, hh]
        s = jnp.concatenate([s_loc, parts[NA_N_CHUNKS]], axis=1)
        m = jnp.max(s, axis=1, keepdims=True)
        p = jnp.exp(s - m)
        l = jnp.sum(p, axis=1, keepdims=True)
        pb = p.astype(BF16)
        acc = None
        for j, vr in enumerate(v_refs):
            t = jnp.dot(pb[:, j * NA_CHUNK:(j + 1) * NA_CHUNK], vr[0, :, lanes], preferred_element_type=F32)
            acc = t if acc is None else acc + t
        o_ref[0, :, lanes] = (acc / l).astype(o_ref.dtype)


def _na_bias_tables(rpb, rows):
    n_groups = rows // NA_Q_ROWS
    n_h = rpb.shape[0]
    n_dr, n_dc = 2 * NA_WIN_H - 1, 2 * NA_WIN_W - 1
    exact = lax.Precision.HIGHEST
    qc = np.arange(GRID_W)[:, None]
    kc = np.arange(GRID_W)[None, :]
    cs = np.clip(qc - NA_WIN_W // 2, 0, GRID_W - NA_WIN_W)
    col_valid = (kc >= cs) & (kc < cs + NA_WIN_W)
    dc = np.clip(kc - qc + (NA_WIN_W - 1), 0, n_dc - 1)
    oh_c = (np.arange(n_dc)[:, None, None] == dc[None]).astype(np.float32).reshape(n_dc, GRID_W * GRID_W)
    cols = jnp.einsum('hrd,dx->hrx', rpb.astype(F32), jnp.asarray(oh_c), precision=exact)
    tabs = []
    j = np.arange(NA_Q_ROWS)[:, None]
    i = np.arange(NA_K_ROWS)[None, :]
    for g in (0, 1, n_groups - 1):
        qr = NA_Q_ROWS * g + j
        kr = np.clip(NA_Q_ROWS * g - NA_WIN_H // 2, 0, rows - NA_K_ROWS) + i
        rs = np.clip(qr - NA_WIN_H // 2, 0, rows - NA_WIN_H)
        row_valid = (kr >= rs) & (kr < rs + NA_WIN_H)
        dr = np.clip(kr - qr + (NA_WIN_H - 1), 0, n_dr - 1)
        oh_r = (np.arange(n_dr)[None, None, :] == dr[:, :, None]).astype(np.float32)
        oh_r = oh_r.reshape(NA_Q_ROWS * NA_K_ROWS, n_dr)
        t = jnp.einsum('pr,hrx->hpx', jnp.asarray(oh_r), cols, precision=exact)
        t = t.reshape(n_h, NA_Q_ROWS, NA_K_ROWS, GRID_W, GRID_W)
        valid = row_valid[:, :, None, None] & col_valid[None, None]
        t = jnp.where(valid[None], t, NEG_BIG).transpose(0, 1, 3, 2, 4)
        tabs.append(t.reshape(n_h, NA_Q_ROWS * GRID_W, NA_K_ROWS * GRID_W))
    return jnp.stack(tabs)


def _na_attention(qkv, bias_tab, *, n_heads, seq, ctx_chunk):
    B = qkv.shape[0]
    rows = seq // GRID_W
    n_groups = rows // NA_Q_ROWS
    bq = NA_Q_ROWS * GRID_W
    max_cb = seq // NA_CHUNK - NA_N_CHUNKS

    def cb(g):
        return jnp.clip(2 * g - 1, 0, max_cb)

    hps = NA_HEADS_PER_STEP
    hw = hps * HEAD_DIM
    n_hblk = n_heads // hps
    in_specs = [pl.BlockSpec((1, bq, hw), lambda b, g, h: (b, g, h))]
    for part in (1, 2):
        for c in range(NA_N_CHUNKS):
            in_specs.append(pl.BlockSpec((1, NA_CHUNK, hw),
                                         lambda b, g, h, c=c, part=part: (b, cb(g) + c, part * n_hblk + h)))
        in_specs.append(pl.BlockSpec((1, NA_CHUNK, hw),
                                     lambda b, g, h, part=part: (b, ctx_chunk, part * n_hblk + h)))
    in_specs.append(pl.BlockSpec(
        (1, hps, bq, NA_K_ROWS * GRID_W),
        lambda b, g, h: (jnp.where(g == 0, 0, jnp.where(g == n_groups - 1, 2, 1)), h, 0, 0)))
    n_in = 2 * (NA_N_CHUNKS + 1)
    return pl.pallas_call(
        functools.partial(_na_body, scale=HEAD_DIM ** -0.5),
        grid=(B, n_groups, n_hblk), in_specs=in_specs,
        out_specs=pl.BlockSpec((1, bq, hw), lambda b, g, h: (b, g, h)),
        out_shape=jax.ShapeDtypeStruct((B, seq, n_heads * HEAD_DIM), BF16),
        compiler_params=_cparams(3), name="na_attention")(qkv, *([qkv] * n_in), bias_tab)


def _softplus(z):
    return jnp.maximum(z, 0.0) + jnp.log(1.0 + jnp.exp(-jnp.abs(z)))


def _lru_gates_body(x_ref, hp_ref, hn_ref, cw_ref, cb_ref, w_ref, ba_ref, bx_ref, lam_ref,
                    af_ref, bf_ref, ab_ref, bb_ref, xs_ref, *, bpb, lat_blocks, n_blocks_lru):
    i = pl.program_id(0)
    j = i % bpb
    first = jnp.logical_or(j == 0, j == lat_blocks)
    last = jnp.logical_or(j == lat_blocks - 1, j == bpb - 1)
    x = x_ref[...]
    xs_ref[SUBLANES:SUBLANES + ROW_BLOCK, :] = x
    xs_ref[0:SUBLANES, :] = jnp.where(first, 0.0, hp_ref[...])
    xs_ref[SUBLANES + ROW_BLOCK:, :] = jnp.where(last, 0.0, hn_ref[...])
    xc = cb_ref[...] + cw_ref[2:3, :] * x
    for tap, off in ((0, -2), (1, -1), (3, 1)):
        xc = xc + cw_ref[tap:tap + 1, :] * xs_ref[SUBLANES + off:SUBLANES + off + ROW_BLOCK, :]
    xcb = xc.astype(BF16)
    zs = [jnp.dot(xcb[:, n * LANES:(n + 1) * LANES], w_ref[n].astype(BF16), preferred_element_type=F32)
          for n in range(n_blocks_lru)]
    for d, (a_ref, b_ref) in enumerate(((af_ref, bf_ref), (ab_ref, bb_ref))):
        za = jnp.concatenate([z[:, (2 * d) * LANES:(2 * d + 1) * LANES] for z in zs], axis=1)
        zx = jnp.concatenate([z[:, (2 * d + 1) * LANES:(2 * d + 2) * LANES] for z in zs], axis=1)
        r = jax.nn.sigmoid(za + ba_ref[d:d + 1, :])
        ig = jax.nn.sigmoid(zx + bx_ref[d:d + 1, :])
        log_a = (-LRU_C) * r * _softplus(-lam_ref[d:d + 1, :])
        a = jnp.exp(log_a)
        a_ref[...] = a
        b_ref[...] = jnp.sqrt(1.0 - jnp.exp(2.0 * log_a)) * (ig * xc)


def _lru_gates(p_lru, conv_w, conv_b, w_cat, b_a, b_x, lam, tok):
    M = p_lru.shape[0]
    W = conv_b.shape[-1]
    nb = W // LANES
    rb8 = ROW_BLOCK // SUBLANES
    n_tiles8 = M // SUBLANES
    row = pl.BlockSpec((ROW_BLOCK, W), lambda i: (i, 0))
    vec2 = pl.BlockSpec((2, W), lambda i: (0, 0))
    return pl.pallas_call(
        functools.partial(_lru_gates_body, bpb=tok.blocks_per_batch, lat_blocks=tok.lat_blocks, n_blocks_lru=nb),
        grid=(M // ROW_BLOCK,),
        in_specs=[row,
                  pl.BlockSpec((SUBLANES, W), lambda i: (jnp.maximum(i * rb8 - 1, 0), 0)),
                  pl.BlockSpec((SUBLANES, W), lambda i: (jnp.minimum((i + 1) * rb8, n_tiles8 - 1), 0)),
                  pl.BlockSpec((LRU_CONV, W), lambda i: (0, 0)),
                  pl.BlockSpec((1, W), lambda i: (0, 0)),
                  pl.BlockSpec((nb, LANES, 4 * LANES), lambda i: (0, 0, 0)),
                  vec2, vec2, vec2],
        out_specs=[row] * 4,
        out_shape=[jax.ShapeDtypeStruct((M, W), F32)] * 4,
        scratch_shapes=[pltpu.VMEM((ROW_BLOCK + 2 * SUBLANES, W), F32)],
        compiler_params=_cparams(1), name="lru_gates")(
            p_lru, p_lru, p_lru, conv_w, conv_b.reshape(1, W), w_cat, b_a, b_x, lam)


LRU_UNROLL = 8


def _lru_scan_body(af_ref, bf_ref, ab_ref, bb_ref, hf_ref, hb_ref, carry_ref):
    @pl.when(pl.program_id(1) == 0)
    def _():
        carry_ref[...] = jnp.zeros(carry_ref.shape, F32)

    n = af_ref.shape[1]

    def step(t, hs):
        hf, hb = hs
        tb = n - 1 - t
        hf = af_ref[0, t] * hf + bf_ref[0, t]
        hb = ab_ref[0, tb] * hb + bb_ref[0, tb]
        hf_ref[0, t] = hf
        hb_ref[0, tb] = hb
        return hf, hb

    hf, hb = lax.fori_loop(0, n, step, (carry_ref[0], carry_ref[1]), unroll=LRU_UNROLL)
    carry_ref[0] = hf
    carry_ref[1] = hb


def _lru_scan(a_f, b_f, a_b, b_b, tok):
    B, T, R, _ = a_f.shape
    lat = tok.lat_blocks
    n_chunks = tok.blocks_per_batch

    def fwd(b, k):
        return (b, jnp.where(k == 0, lat, k - 1), 0, 0)

    def bwd(b, k):
        return (b, jnp.where(k == 0, lat, lat - k), 0, 0)

    blk = (1, ROW_BLOCK, R, LANES)
    return pl.pallas_call(
        _lru_scan_body, grid=(B, n_chunks),
        in_specs=[pl.BlockSpec(blk, fwd), pl.BlockSpec(blk, fwd), pl.BlockSpec(blk, bwd), pl.BlockSpec(blk, bwd)],
        out_specs=[pl.BlockSpec(blk, fwd), pl.BlockSpec(blk, bwd)],
        out_shape=[jax.ShapeDtypeStruct(a_f.shape, F32)] * 2,
        scratch_shapes=[pltpu.VMEM((2, R, LANES), F32)],
        compiler_params=_cparams(2), name="lru_scan")(a_f, b_f, a_b, b_b)


def _gelu_tanh(x):
    return 0.5 * x * (1.0 + jnp.tanh(math.sqrt(2.0 / math.pi) * (x + 0.044715 * (x * x * x))))


def _lru_out_body(hf_ref, hb_ref, g_ref, y_ref):
    y_ref[...] = ((hf_ref[...] + hb_ref[...]) * _gelu_tanh(g_ref[...])).astype(y_ref.dtype)


def _lru_out(hf, hb, p_lru):
    M, W = hf.shape
    row = pl.BlockSpec((ROW_BLOCK, W), lambda i: (i, 0))
    return pl.pallas_call(
        _lru_out_body, grid=(M // ROW_BLOCK,),
        in_specs=[row, row, pl.BlockSpec((ROW_BLOCK, W), lambda i: (i, 1))],
        out_specs=row, out_shape=jax.ShapeDtypeStruct((M, W), BF16),
        compiler_params=_cparams(1), name="lru_out")(hf, hb, p_lru)


def _merge_body(*refs, has_prev):
    if has_prev:
        h_ref, y_ref, wg_ref, wb_ref, bg_ref, prev_ref, o_ref = refs
    else:
        h_ref, y_ref, wg_ref, wb_ref, bg_ref, o_ref = refs
    gate = jax.nn.sigmoid(jnp.dot(h_ref[...], wg_ref[...], preferred_element_type=F32) + bg_ref[0])
    term = gate * jnp.dot(y_ref[...], wb_ref[...], preferred_element_type=F32)
    if has_prev:
        term = term + prev_ref[...]
    o_ref[...] = term.astype(o_ref.dtype)


def _merge_branch(h, y, w_gate, b_gate, w_branch, prev, *, layer, branch, bm, bn, out_dtype):
    M, D = h.shape
    kb = y.shape[1]
    in_specs = [pl.BlockSpec((bm, D), lambda n, m: (m, 0)),
                pl.BlockSpec((bm, kb), lambda n, m: (m, 0)),
                pl.BlockSpec((None, None, D, bn), lambda n, m: (layer, branch, 0, n)),
                pl.BlockSpec((None, None, kb, bn), lambda n, m: (layer, branch, 0, n)),
                pl.BlockSpec((None, 1, 1, bn), lambda n, m: (layer, branch, 0, n))]
    args = [h, y, w_gate, w_branch, b_gate.reshape(b_gate.shape[0], b_gate.shape[1], 1, D)]
    if prev is not None:
        in_specs.append(pl.BlockSpec((bm, bn), lambda n, m: (m, n)))
        args.append(prev)
    return pl.pallas_call(
        functools.partial(_merge_body, has_prev=prev is not None),
        grid=(D // bn, M // bm), in_specs=in_specs,
        out_specs=pl.BlockSpec((bm, bn), lambda n, m: (m, n)),
        out_shape=jax.ShapeDtypeStruct((M, D), out_dtype),
        compiler_params=_cparams(2), name=f"merge_branch{branch}")(*args)


ROUTER_BLOCK = 512
EXPERT_BLOCK = 512


def _router_body(h_ref, wr_ref, rb_ref, tri_ref, ids_ref, wts_ref, cnt_ref, carry_ref, *, n_exp):
    @pl.when(pl.program_id(0) == 0)
    def _():
        carry_ref[...] = jnp.zeros(carry_ref.shape, F32)

    per = n_exp // N_GROUPS
    logits = lax.dot_general(wr_ref[...], h_ref[...], (((1,), (1,)), ((), ())), preferred_element_type=F32)
    scores = jax.nn.sigmoid(logits)
    sel = scores + rb_ref[:, :1]
    sel_r = [sel[e:e + 1, :] for e in range(n_exp)]
    sc_r = [scores[e:e + 1, :] for e in range(n_exp)]
    gs = []
    for g in range(N_GROUPS):
        rows = sel_r[g * per:(g + 1) * per]
        best = None
        for a in range(per):
            for b in range(a + 1, per):
                ps = rows[a] + rows[b]
                best = ps if best is None else jnp.maximum(best, ps)
        gs.append(best)
    bestg = jnp.full(gs[0].shape, N_GROUPS - 1, I32)
    run = gs[N_GROUPS - 1]
    for g in range(N_GROUPS - 2, -1, -1):
        take = gs[g] >= run
        bestg = jnp.where(take, g, bestg)
        run = jnp.where(take, gs[g], run)
    v = [sel_r[i] for i in range(per)]
    sc = [sc_r[i] for i in range(per)]
    for g in range(1, N_GROUPS):
        isg = bestg == g
        v = [jnp.where(isg, sel_r[g * per + i], v[i]) for i in range(per)]
        sc = [jnp.where(isg, sc_r[g * per + i], sc[i]) for i in range(per)]

    def first_argmax(vals):
        idx = jnp.full(vals[0].shape, per - 1, I32)
        mx = vals[per - 1]
        sv = sc[per - 1]
        for i in range(per - 2, -1, -1):
            take = vals[i] >= mx
            idx = jnp.where(take, i, idx)
            mx = jnp.where(take, vals[i], mx)
            sv = jnp.where(take, sc[i], sv)
        return idx, sv

    i1, s1 = first_argmax(v)
    v2 = [jnp.where(i1 == i, -jnp.inf, v[i]) for i in range(per)]
    i2, s2 = first_argmax(v2)
    denom = s1 + s2
    e0 = bestg * per + i1
    e1 = bestg * per + i2
    eid = lax.broadcasted_iota(I32, scores.shape, 0)
    oh0 = (eid == e0).astype(F32)
    oh1 = (eid == e1).astype(F32)
    oh = oh0 + oh1
    prefix = jnp.dot(oh.astype(BF16), tri_ref[...], preferred_element_type=F32) + carry_ref[:, :1]
    r0 = jnp.sum(oh0 * prefix, axis=0, keepdims=True)
    r1 = jnp.sum(oh1 * prefix, axis=0, keepdims=True)
    new_carry = carry_ref[:, :1] + jnp.sum(oh, axis=1, keepdims=True)
    carry_ref[...] = jnp.broadcast_to(new_carry, carry_ref.shape)
    cnt_ref[...] = jnp.broadcast_to(new_carry, cnt_ref.shape)
    ids_ref[...] = jnp.zeros(ids_ref.shape, I32)
    wts_ref[...] = jnp.zeros(wts_ref.shape, F32)
    for r, val in enumerate((e0, e1, r0.astype(I32), r1.astype(I32))):
        ids_ref[r:r + 1, :] = val
    wts_ref[0:1, :] = s1 / denom
    wts_ref[1:2, :] = s2 / denom


def _router(h2, w_router, router_bias):
    M, D = h2.shape
    n_exp = w_router.shape[1]
    bm = ROUTER_BLOCK
    tri = jnp.asarray(np.triu(np.ones((bm, bm), np.float32), k=1), BF16)
    wr_t = w_router.T.astype(BF16)
    rb = jnp.broadcast_to(router_bias.astype(F32)[:, None], (n_exp, LANES))
    return pl.pallas_call(
        functools.partial(_router_body, n_exp=n_exp),
        grid=(M // bm,),
        in_specs=[pl.BlockSpec((bm, D), lambda i: (i, 0)),
                  pl.BlockSpec((n_exp, D), lambda i: (0, 0)),
                  pl.BlockSpec((n_exp, LANES), lambda i: (0, 0)),
                  pl.BlockSpec((bm, bm), lambda i: (0, 0))],
        out_specs=[pl.BlockSpec((SUBLANES, bm), lambda i: (0, i)),
                   pl.BlockSpec((SUBLANES, bm), lambda i: (0, i)),
                   pl.BlockSpec((n_exp, LANES), lambda i: (0, 0))],
        out_shape=[jax.ShapeDtypeStruct((SUBLANES, M), I32), jax.ShapeDtypeStruct((SUBLANES, M), F32),
                   jax.ShapeDtypeStruct((n_exp, LANES), F32)],
        scratch_shapes=[pltpu.VMEM((n_exp, LANES), F32)],
        compiler_params=_cparams(1), name="moe_router")(h2, wr_t, rb, tri)


DISPATCH_ROWS = 256


def _dispatch_body(n_live_ref, src_ref, h_hbm, xs_ref, sem):
    def copy(t):
        return pltpu.make_async_copy(h_hbm.at[pl.ds(src_ref[0, 0, t], 1)], xs_ref.at[pl.ds(t, 1)], sem)

    def start(t, c):
        copy(t).start()
        return c

    def wait(t, c):
        copy(t).wait()
        return c

    live = pl.program_id(0) * DISPATCH_ROWS < n_live_ref[0]

    @pl.when(live)
    def _():
        lax.fori_loop(0, DISPATCH_ROWS, start, 0)
        lax.fori_loop(0, DISPATCH_ROWS, wait, 0)

    @pl.when(jnp.logical_not(live))
    def _():
        xs_ref[...] = jnp.zeros(xs_ref.shape, xs_ref.dtype)


def _dispatch(h3, src, n_live, n_slots):
    slab = h3.shape[1:]
    return pl.pallas_call(
        _dispatch_body,
        grid_spec=pltpu.PrefetchScalarGridSpec(
            num_scalar_prefetch=1, grid=(n_slots // DISPATCH_ROWS,),
            in_specs=[pl.BlockSpec((1, 1, DISPATCH_ROWS), lambda i, nl: (i, 0, 0), memory_space=pltpu.SMEM),
                      pl.BlockSpec(memory_space=pl.ANY)],
            out_specs=pl.BlockSpec((DISPATCH_ROWS,) + slab, lambda i, nl: (i, 0, 0)),
            scratch_shapes=[pltpu.SemaphoreType.DMA(())]),
        out_shape=jax.ShapeDtypeStruct((n_slots,) + slab, h3.dtype),
        compiler_params=_cparams(1), name="moe_dispatch")(n_live, src, h3)


def _expert_up_body(be_ref, nu_ref, x_ref, wg_ref, wu_ref, o_ref, wg_bf, wu_bf):
    i = pl.program_id(1)
    fresh = jnp.logical_or(i == 0, be_ref[i] != be_ref[jnp.maximum(i - 1, 0)])

    @pl.when(jnp.logical_and(fresh, i < nu_ref[0]))
    def _():
        wg_bf[...] = wg_ref[...].astype(BF16)
        wu_bf[...] = wu_ref[...].astype(BF16)

    @pl.when(i < nu_ref[0])
    def _():
        x = x_ref[...]
        g = jnp.dot(x, wg_bf[...], preferred_element_type=F32)
        u = jnp.dot(x, wu_bf[...], preferred_element_type=F32)
        o_ref[...] = (g * jax.nn.sigmoid(g) * u).astype(o_ref.dtype)

    @pl.when(i >= nu_ref[0])
    def _():
        o_ref[...] = jnp.zeros(o_ref.shape, o_ref.dtype)


def _expert_up(xs, w_gate, w_up, blk_expert, n_used, *, layer, bn):
    P, D = xs.shape
    de = w_gate.shape[-1]
    nblk = P // EXPERT_BLOCK

    def xmap(n, i, be, nu):
        return (jnp.minimum(i, nu[0] - 1), 0)

    def wmap(n, i, be, nu):
        return (layer, be[i], 0, n)

    return pl.pallas_call(
        _expert_up_body,
        grid_spec=pltpu.PrefetchScalarGridSpec(
            num_scalar_prefetch=2, grid=(de // bn, nblk),
            in_specs=[pl.BlockSpec((EXPERT_BLOCK, D), xmap),
                      pl.BlockSpec((None, None, D, bn), wmap),
                      pl.BlockSpec((None, None, D, bn), wmap)],
            out_specs=pl.BlockSpec((EXPERT_BLOCK, bn), lambda n, i, be, nu: (i, n)),
            scratch_shapes=[pltpu.VMEM((D, bn), BF16), pltpu.VMEM((D, bn), BF16)]),
        out_shape=jax.ShapeDtypeStruct((P, de), BF16),
        compiler_params=_cparams(2), name="moe_expert_up")(blk_expert, n_used, xs, w_gate, w_up)


def _expert_down_body(be_ref, nu_ref, x_ref, wd_ref, o_ref, wd_bf):
    i = pl.program_id(1)
    fresh = jnp.logical_or(i == 0, be_ref[i] != be_ref[jnp.maximum(i - 1, 0)])

    @pl.when(jnp.logical_and(fresh, i < nu_ref[0]))
    def _():
        wd_bf[...] = wd_ref[...].astype(BF16)

    @pl.when(i < nu_ref[0])
    def _():
        o_ref[...] = jnp.dot(x_ref[...], wd_bf[...], preferred_element_type=F32).astype(o_ref.dtype)

    @pl.when(i >= nu_ref[0])
    def _():
        o_ref[...] = jnp.zeros(o_ref.shape, o_ref.dtype)


def _expert_down(he, w_down, blk_expert, n_used, *, layer, bn):
    P, de = he.shape
    D = w_down.shape[-1]
    nblk = P // EXPERT_BLOCK
    return pl.pallas_call(
        _expert_down_body,
        grid_spec=pltpu.PrefetchScalarGridSpec(
            num_scalar_prefetch=2, grid=(D // bn, nblk),
            in_specs=[pl.BlockSpec((EXPERT_BLOCK, de), lambda n, i, be, nu: (jnp.minimum(i, nu[0] - 1), 0)),
                      pl.BlockSpec((None, None, de, bn), lambda n, i, be, nu: (layer, be[i], 0, n))],
            out_specs=pl.BlockSpec((EXPERT_BLOCK, bn), lambda n, i, be, nu: (i, n)),
            scratch_shapes=[pltpu.VMEM((de, bn), BF16)]),
        out_shape=jax.ShapeDtypeStruct((P, D), BF16),
        compiler_params=_cparams(2), name="moe_expert_down")(blk_expert, n_used, he, w_down)


COMBINE_UNROLL = 8


def _combine_body(pos_ref, w_ref, o_hbm, y_ref, buf, sem):
    def copy(t, k):
        return pltpu.make_async_copy(o_hbm.at[pl.ds(pos_ref[0, k, t], 1)], buf.at[k, pl.ds(t, 1)], sem)

    def start(t, c):
        copy(t, 0).start()
        copy(t, 1).start()
        return c

    def wait(t, c):
        copy(t, 0).wait()
        copy(t, 1).wait()
        return c

    def mix(t, c):
        y_ref[t] = w_ref[0, 0, t] * buf[0, t].astype(F32) + w_ref[0, 1, t] * buf[1, t].astype(F32)
        return c

    lax.fori_loop(0, DISPATCH_ROWS, start, 0)
    lax.fori_loop(0, DISPATCH_ROWS, wait, 0)
    lax.fori_loop(0, DISPATCH_ROWS, mix, 0, unroll=COMBINE_UNROLL)


def _combine(o3, pos, wts, n_rows):
    slab = o3.shape[1:]
    smem = pl.BlockSpec((1, 2, DISPATCH_ROWS), lambda i: (i, 0, 0), memory_space=pltpu.SMEM)
    return pl.pallas_call(
        _combine_body, grid=(n_rows // DISPATCH_ROWS,),
        in_specs=[smem, smem, pl.BlockSpec(memory_space=pl.ANY)],
        out_specs=pl.BlockSpec((DISPATCH_ROWS,) + slab, lambda i: (i, 0, 0)),
        out_shape=jax.ShapeDtypeStruct((n_rows,) + slab, F32),
        scratch_shapes=[pltpu.VMEM((2, DISPATCH_ROWS) + slab, o3.dtype), pltpu.SemaphoreType.DMA(())],
        compiler_params=_cparams(1), name="moe_combine")(pos, wts, o3)


def _moe(h2, w_router, router_bias, w_gate, w_up, w_down, *, layer):
    M, D = h2.shape
    n_exp = w_router.shape[1]
    ids, wts, cnt = _router(h2, w_router, router_bias)
    counts = cnt[:, 0].astype(I32)
    padded = (counts + EXPERT_BLOCK - 1) // EXPERT_BLOCK * EXPERT_BLOCK
    ends = jnp.cumsum(padded)
    offs = ends - padded
    n_slots = (2 * M + n_exp * (EXPERT_BLOCK - 1)) // EXPERT_BLOCK * EXPERT_BLOCK
    nblk = n_slots // EXPERT_BLOCK
    pos0 = offs[ids[0]] + ids[2]
    pos1 = offs[ids[1]] + ids[3]
    pos = jnp.stack([pos0.reshape(-1, DISPATCH_ROWS), pos1.reshape(-1, DISPATCH_ROWS)], axis=1)
    wpair = jnp.stack([wts[0].reshape(-1, DISPATCH_ROWS), wts[1].reshape(-1, DISPATCH_ROWS)], axis=1)
    blk_start = jnp.arange(nblk, dtype=I32) * EXPERT_BLOCK
    blk_expert = jnp.minimum(jnp.sum((ends[None, :] <= blk_start[:, None]).astype(I32), axis=1), n_exp - 1)
    n_used = (ends[-1] // EXPERT_BLOCK).astype(I32).reshape(1)
    tok_ids = jnp.arange(M, dtype=I32)
    src = jnp.zeros((n_slots,), I32).at[jnp.concatenate([pos0, pos1])].set(jnp.concatenate([tok_ids, tok_ids]))
    xs = _dispatch(h2.reshape(M, D // LANES, LANES), src.reshape(-1, 1, DISPATCH_ROWS),
                   ends[-1].astype(I32).reshape(1), n_slots).reshape(n_slots, D)
    he = _expert_up(xs, w_gate, w_up, blk_expert, n_used, layer=layer, bn=_pick(w_gate.shape[-1], (256, 128)))
    o = _expert_down(he, w_down, blk_expert, n_used, layer=layer, bn=_pick(D, (2048, 1024, 512)))
    return _combine(o.reshape(n_slots, D // LANES, LANES), pos, wpair, M).reshape(M, D)


class _Tok:
    def __init__(self, n_batch, seq, ctx):
        self.n_batch, self.seq, self.ctx = n_batch, seq, ctx
        self.t_all = seq + ctx
        self.lat_blocks = seq // ROW_BLOCK
        self.blocks_per_batch = self.t_all // ROW_BLOCK


def _rope_tables(seq, ctx):
    t = jnp.arange(seq, dtype=I32)
    row = (t // GRID_W).astype(F32)[:, None]
    col = (t % GRID_W).astype(F32)[:, None]

    def cs(rot_dim):
        n_freq = rot_dim // 4
        freqs = ROPE_THETA ** (-jnp.arange(n_freq, dtype=F32) / n_freq)
        ang = jnp.concatenate([row * freqs, col * freqs], axis=-1)
        cos = jnp.concatenate([jnp.cos(ang), jnp.ones((ctx, rot_dim // 2), F32)], axis=0)
        sin = jnp.concatenate([jnp.sin(ang), jnp.zeros((ctx, rot_dim // 2), F32)], axis=0)
        return cos, sin

    cos, sin = cs(HEAD_DIM)
    rope128 = (jnp.concatenate([cos, cos], axis=1), jnp.concatenate([-sin, sin], axis=1))
    cos, sin = cs(MLA_ROPE)
    z32 = jnp.zeros_like(sin)
    z64 = jnp.zeros((seq + ctx, LANES - MLA_ROPE), F32)
    rope64 = (jnp.concatenate([cos, cos, z64], axis=1), jnp.concatenate([sin, z32, z64], axis=1),
              jnp.concatenate([z32, sin, z64], axis=1))
    return rope128, rope64


def _q_rope_epilogue(acc, c_ref, slo_ref, shi_ref, *, scale):
    nope = acc[:, :MLA_NOPE] * scale
    rope = _rope64(acc[:, MLA_NOPE:], c_ref[...], slo_ref[...], shi_ref[...]) * scale
    return jnp.concatenate([nope, rope], axis=1)


def _kv_assemble_epilogue(acc, krr_ref):
    keys = jnp.concatenate([acc[:, :MLA_NOPE], krr_ref[...].astype(F32)], axis=1)
    vals = jnp.concatenate([acc[:, MLA_NOPE:], jnp.ones((acc.shape[0], LANES), F32)], axis=1)
    return keys, vals


def kernel(x, c, ctx, c_ctx, w_mod, b_mod, norm_mix, norm_ffn, w_in, mla_q_norm, mla_w_uq, mla_kv_norm,
           mla_w_ukv, gqa_q_norm, gqa_k_norm, na_rpb, lru_conv_w, lru_conv_b, lru_w_a, lru_b_a, lru_w_x,
           lru_b_x, lru_lambda, w_branch_gate, b_branch_gate, w_branch, w_out, w_router, router_bias,
           w_exp_gate, w_exp_up, w_exp_down, final_norm):
    B, S, D = x.shape
    CTX = ctx.shape[1]
    depth = w_in.shape[0]
    tok = _Tok(B, S, CTX)
    T = tok.t_all
    M = B * T
    bw = D // 4
    n_heads = bw // HEAD_DIM
    ql, kvl = mla_q_norm.shape[-1], mla_kv_norm.shape[-1]
    d_in = w_in.shape[-1]
    n_kv = (d_in - (ql + kvl + MLA_ROPE + bw + 3 * bw + 2 * bw)) // (2 * HEAD_DIM)
    q_per_kv = n_heads // n_kv
    n_lru_blocks = lru_w_a.shape[2]
    assert bw // n_lru_blocks == LANES and S % ROW_BLOCK == 0 and CTX == ROW_BLOCK and S % (GRID_W * NA_Q_ROWS) == 0
    mla_dq = 2 * LANES
    mla_scale = (MLA_NOPE + MLA_ROPE) ** -0.5 * LOG2E
    bm = _pick(M, (768, 512, 256))
    bm_tok = _pick(T, (768, 256))

    rope128, rope64 = _rope_tables(S, CTX)
    xa = jnp.concatenate([x, ctx], axis=1).reshape(M, D)
    cc = jnp.zeros((MOD_ROWS, D), F32).at[:B].set(c).at[B].set(c_ctx)
    silu_c = (cc * jax.nn.sigmoid(cc)).astype(BF16)

    o_mla = ql + kvl
    o_gqa = o_mla + MLA_ROPE
    w_gqa = bw + 2 * n_kv * HEAD_DIM
    o_na = o_gqa + w_gqa
    o_lru = o_na + 3 * bw

    bn_mod = _pick(N_MOD * D, (512, 256))
    mods = []
    for layer in range(depth):
        modv = _matmul(silu_c, w_mod, w_prefix=(layer,), n_cols=N_MOD * D, bm=MOD_ROWS, bn=bn_mod, out_dtype=F32,
                       epilogue=lambda acc, b_ref: acc + b_ref[0],
                       extras=[(b_mod.reshape(depth, 1, N_MOD * D), (1, 1, bn_mod),
                                lambda n, m, layer=layer: (layer, 0, n))],
                       name="mod_proj")
        mods.append(modv[:B + 1].reshape(B + 1, N_MOD, D))
    wg_bf, wb_bf = w_branch_gate.astype(BF16), w_branch.astype(BF16)

    y_ffn = mod_prev = None
    for layer in range(depth):
        need_ctx = layer < depth - 1
        modv = mods[layer]
        xa, h = _resid_norm(xa, y_ffn, mod_prev, modv, norm_mix[layer], gate_idx=5, shift_idx=0, scale_idx=1,
                            out_dtype=BF16, tok=tok, name="norm_mix")
        cqkv = _matmul(h, w_in, w_prefix=(layer,), n_cols=o_mla, bm=bm, bn=_pick_bn(o_mla, 0), out_dtype=F32,
                       name="in_mla")
        w_kr = jnp.pad(w_in[layer, :, o_mla:o_gqa], ((0, 0), (0, LANES - MLA_ROPE)))
        kr = _matmul(h, w_kr, n_cols=LANES, bm=bm, bn=LANES, out_dtype=F32, name="in_kr")
        w_rest = w_in[layer, :, o_gqa:o_lru]
        p_gqa = _matmul(h, w_rest, n_cols=w_gqa, bm=bm, bn=_pick_bn(w_gqa, 0), out_dtype=F32, name="in_gqa")
        p_na = _matmul(h, w_rest, n_off=w_gqa, n_cols=3 * bw, bm=bm, bn=_pick_bn(3 * bw, w_gqa),
                       out_dtype=BF16, name="in_na")
        p_lru = _matmul(h, w_in[layer, :, o_lru:], n_cols=2 * bw, bm=bm, bn=_pick_bn(2 * bw, 0), out_dtype=F32,
                        name="in_lru")

        qn, kvn, krr = _mla_prep(cqkv, kr, mla_q_norm[layer], mla_kv_norm[layer], rope64, tok)
        w_uq = jnp.pad(mla_w_uq[layer].reshape(ql, n_heads, MLA_NOPE + MLA_ROPE),
                       ((0, 0), (0, 0), (0, mla_dq - MLA_NOPE - MLA_ROPE))).reshape(ql, n_heads * mla_dq)
        tpb = T // bm_tok
        tab_spec = ((bm_tok, LANES), lambda n, m: (m % tpb, 0))
        q_mla = _matmul(qn, w_uq, n_cols=n_heads * mla_dq, bm=bm_tok, bn=mla_dq, out_dtype=BF16,
                        epilogue=functools.partial(_q_rope_epilogue, scale=mla_scale),
                        extras=[(t, *tab_spec) for t in rope64], name="mla_uq")
        k_mla, v_mla = _matmul(kvn, mla_w_ukv, w_prefix=(layer,), n_cols=n_heads * mla_dq, bm=bm, bn=mla_dq,
                               out_dtype=BF16, n_out=2, epilogue=_kv_assemble_epilogue,
                               extras=[(krr, (bm, LANES), lambda n, m: (m, 0))], name="mla_ukv")
        bq = _pick(S, (512, 256))
        bk = _pick(T, (768, 256))
        ctx_blk = S // ROW_BLOCK
        lat_q = dict(bq=bq, q_off=0, n_q=S // bq, bk=bk, k_off=0, n_k=T // bk, pipelined=True)
        q3, k3, v3 = (a.reshape(B, T, -1) for a in (q_mla, k_mla, v_mla))
        mla_args = dict(n_heads=n_heads, dq=mla_dq, dk=mla_dq, k_lane=lambda h: h, v_lane=lambda h: h,
                        v_has_ones=True)
        ya = _attention(q3, k3, v3, name="mla_attn", **lat_q, **mla_args)
        q_g, k_g, v_g = (a.reshape(B, T, -1) for a in
                         _gqa_prep(p_gqa, gqa_q_norm[layer], gqa_k_norm[layer], rope128, n_heads, n_kv, tok))
        gqa_args = dict(n_heads=n_heads, dq=HEAD_DIM, dk=HEAD_DIM, k_lane=lambda h: h // q_per_kv,
                        v_lane=lambda h: h // q_per_kv, v_has_ones=True)
        yb = _attention(q_g, k_g, v_g, name="gqa_attn", **lat_q, **gqa_args)
        na3 = p_na.reshape(B, T, 3 * bw)
        yc = _na_attention(na3, _na_bias_tables(na_rpb[layer], S // GRID_W), n_heads=n_heads, seq=S,
                           ctx_chunk=S // NA_CHUNK)
        if need_ctx:
            ctx_q = dict(bq=ROW_BLOCK, q_off=ctx_blk, n_q=1, bk=ROW_BLOCK, k_off=ctx_blk, n_k=1, pipelined=False)
            ya_c = _attention(q3, k3, v3, name="mla_attn_ctx", **ctx_q, **mla_args)
            yb_c = _attention(q_g, k_g, v_g, name="gqa_attn_ctx", **ctx_q, **gqa_args)
            yc_c = _attention(na3, na3, na3, name="na_attn_ctx", n_heads=n_heads, dq=HEAD_DIM, dk=HEAD_DIM,
                              k_lane=lambda h: n_heads + h, v_lane=lambda h: 2 * n_heads + h, v_has_ones=False,
                              s_scale=HEAD_DIM ** -0.5 * LOG2E, **ctx_q)
        else:
            ya_c = yb_c = yc_c = jnp.zeros((B, CTX, bw), BF16)
        ya, yb, yc = (jnp.concatenate([l_, c_], axis=1).reshape(M, bw)
                      for l_, c_ in ((ya, ya_c), (yb, yb_c), (yc, yc_c)))
        w_cat = jnp.concatenate([lru_w_a[layer, 0], lru_w_x[layer, 0], lru_w_a[layer, 1], lru_w_x[layer, 1]],
                                axis=-1)
        gates = _lru_gates(p_lru, lru_conv_w[layer], lru_conv_b[layer], w_cat, lru_b_a[layer], lru_b_x[layer],
                           lru_lambda[layer], tok)
        hf, hb = _lru_scan(*(g.reshape(B, T, n_lru_blocks, LANES) for g in gates), tok)
        yd = _lru_out(hf.reshape(M, bw), hb.reshape(M, bw), p_lru)

        bn_merge = _pick(D, (1024, 512, 256))
        bm_merge = _pick(M, (512, 256))
        merged = None
        for i_br, y_br in enumerate((ya, yb, yc, yd)):
            merged = _merge_branch(h, y_br, wg_bf, b_branch_gate, wb_bf, merged, layer=layer,
                                   branch=i_br, bm=bm_merge, bn=bn_merge,
                                   out_dtype=BF16 if i_br == 3 else F32)
        y = _matmul(merged, w_out, w_prefix=(layer,), n_cols=D, bm=bm, bn=_pick_bn(D, 0), out_dtype=F32,
                    name="out_proj")
        xa, h2 = _resid_norm(xa, y, modv, modv, norm_ffn[layer], gate_idx=2, shift_idx=3, scale_idx=4,
                             out_dtype=BF16, tok=tok, name="resid_norm_ffn")
        y_ffn = _moe(h2, w_router, router_bias, w_exp_gate, w_exp_up, w_exp_down, layer=layer)
        mod_prev = modv

    _, out = _resid_norm(xa, y_ffn, mod_prev, None, final_norm, gate_idx=5, shift_idx=None, scale_idx=None,
                         out_dtype=F32, tok=tok, name="final_norm")
    return out.reshape(B, T, D)[:, :S]
```

```python
import functools
import math

import numpy as np
import jax
import jax.numpy as jnp
from jax import lax
from jax.experimental import pallas as pl
from jax.experimental.pallas import tpu as pltpu

F32 = jnp.float32
BF16 = jnp.bfloat16
I32 = jnp.int32

HEAD_DIM = 128
GRID_W = 64
ROPE_THETA = 10000.0
NORM_EPS = 1e-6
MLA_NOPE = 128
MLA_ROPE = 64
NA_WIN_H = 8
NA_WIN_W = 16
LRU_CONV = 4
LRU_C = 8.0
N_GROUPS = 4
N_MOD = 6

LANES = 128
SUBLANES = 8
MOD_ROWS = 16
VMEM_LIMIT_BYTES = 58 * 1024 * 1024
ROW_BLOCK = 256
NEG_BIG = -1e30
LOG2E = math.log2(math.e)


def _pick(n, prefs):
    for p in prefs:
        if n % p == 0:
            return p
    return n


def _pick_bn(n_cols, n_off):
    return _pick(math.gcd(n_cols, n_off) if n_off else n_cols, (1024, 768, 512, 256, 128))


def _cparams(n_axes):
    return pltpu.CompilerParams(dimension_semantics=("arbitrary",) * n_axes,
                                vmem_limit_bytes=VMEM_LIMIT_BYTES)


def _mm_body(*refs, n_extra, n_out, epilogue, cache_w):
    a_ref, w_ref = refs[0], refs[1]
    extra = refs[2:2 + n_extra]
    o_refs = refs[2 + n_extra:2 + n_extra + n_out]
    if cache_w:
        wbf_ref = refs[2 + n_extra + n_out]

        @pl.when(pl.program_id(1) == 0)
        def _():
            wbf_ref[...] = w_ref[...].astype(BF16)

        w = wbf_ref[...]
    else:
        w = w_ref[...].astype(BF16)
    acc = jnp.dot(a_ref[...], w, preferred_element_type=F32)
    if epilogue is not None:
        acc = epilogue(acc, *extra)
    outs = acc if n_out > 1 else (acc,)
    for o_ref, val in zip(o_refs, outs):
        o_ref[...] = val.astype(o_ref.dtype)


def _matmul(a, w, *, n_cols, bm, bn, out_dtype, w_prefix=(), n_off=0, epilogue=None, extras=(), n_out=1, name):
    M, K = a.shape
    assert M % bm == 0 and n_cols % bn == 0 and n_off % bn == 0
    nb_off = n_off // bn
    grid = (n_cols // bn, M // bm)
    w_block = (None,) * len(w_prefix) + (K, bn)
    cache_w = (w.dtype != BF16) and grid[1] > 1
    w_mode = dict(pipeline_mode=pl.Buffered(1)) if cache_w else {}
    in_specs = [pl.BlockSpec((bm, K), lambda n, m: (m, 0)),
                pl.BlockSpec(w_block, lambda n, m: (*w_prefix, 0, n + nb_off), **w_mode)]
    args = [a, w]
    for arr, blk, imap in extras:
        in_specs.append(pl.BlockSpec(blk, imap))
        args.append(arr)
    scratch = [pltpu.VMEM((K, bn), BF16)] if cache_w else []
    out_spec = pl.BlockSpec((bm, bn), lambda n, m: (m, n))
    out_sds = jax.ShapeDtypeStruct((M, n_cols), out_dtype)
    res = pl.pallas_call(
        functools.partial(_mm_body, n_extra=len(extras), n_out=n_out, epilogue=epilogue, cache_w=cache_w),
        grid=grid, in_specs=in_specs, out_specs=[out_spec] * n_out, out_shape=[out_sds] * n_out,
        scratch_shapes=scratch, compiler_params=_cparams(2), name=name)(*args)
    return res if n_out > 1 else res[0]


def _rms(x, gain):
    return x * lax.rsqrt(jnp.mean(x * x, axis=-1, keepdims=True) + NORM_EPS) * gain


def _resid_norm_body(*refs, has_y, has_mod, gate_idx, shift_idx, scale_idx):
    it = iter(refs)
    x_ref = next(it)
    y_ref, modg_ref = (next(it), next(it)) if has_y else (None, None)
    modn_ref = next(it) if has_mod else None
    gain_ref = next(it)
    xo_ref = next(it) if has_y else None
    h_ref = next(it)
    x = x_ref[...]
    if has_y:
        x = x + modg_ref[0, gate_idx:gate_idx + 1, :] * y_ref[...]
        xo_ref[...] = x
    h = _rms(x, gain_ref[...])
    if has_mod:
        h = h * (1.0 + modn_ref[0, scale_idx:scale_idx + 1, :]) + modn_ref[0, shift_idx:shift_idx + 1, :]
    h_ref[...] = h.astype(h_ref.dtype)


def _mod_row_map(blocks_per_batch, lat_blocks, n_batch):
    def imap(i):
        b = i // blocks_per_batch
        j = i % blocks_per_batch
        return (jnp.where(j < lat_blocks, b, n_batch), 0, 0)
    return imap


def _resid_norm(x, y, mod_gate, mod_norm, gain, *, gate_idx, shift_idx, scale_idx, out_dtype, tok, name):
    M, D = x.shape
    nblk = M // ROW_BLOCK
    row = pl.BlockSpec((ROW_BLOCK, D), lambda i: (i, 0))
    mod_spec = pl.BlockSpec((1, N_MOD, D), _mod_row_map(tok.blocks_per_batch, tok.lat_blocks, tok.n_batch))
    in_specs, args = [row], [x]
    if y is not None:
        in_specs += [row, mod_spec]
        args += [y, mod_gate]
    if mod_norm is not None:
        in_specs.append(mod_spec)
        args.append(mod_norm)
    in_specs.append(pl.BlockSpec((1, D), lambda i: (0, 0)))
    args.append(gain.reshape(1, D))
    out_specs, out_shape = [], []
    if y is not None:
        out_specs.append(row)
        out_shape.append(jax.ShapeDtypeStruct((M, D), F32))
    out_specs.append(row)
    out_shape.append(jax.ShapeDtypeStruct((M, D), out_dtype))
    res = pl.pallas_call(
        functools.partial(_resid_norm_body, has_y=y is not None, has_mod=mod_norm is not None, gate_idx=gate_idx,
                          shift_idx=shift_idx, scale_idx=scale_idx),
        grid=(nblk,), in_specs=in_specs, out_specs=out_specs, out_shape=out_shape,
        compiler_params=_cparams(1), name=name)(*args)
    return res if y is not None else (x, res[0])


def _rope64(x, c, slo, shi):
    return x * c - pltpu.roll(x, 96, axis=1) * slo + pltpu.roll(x, 32, axis=1) * shi


def _rope128(x, c, ss):
    return x * c + pltpu.roll(x, 64, axis=1) * ss


def _mla_prep_body(cqkv_ref, kr_ref, qg_ref, kvg_ref, c_ref, slo_ref, shi_ref, qn_ref, kvn_ref, krr_ref, *, ql):
    cqkv = cqkv_ref[...]
    qn_ref[...] = _rms(cqkv[:, :ql], qg_ref[...]).astype(BF16)
    kvn_ref[...] = _rms(cqkv[:, ql:], kvg_ref[...]).astype(BF16)
    krr_ref[...] = _rope64(kr_ref[...], c_ref[...], slo_ref[...], shi_ref[...]).astype(BF16)


def _mla_prep(cqkv, kr, q_gain, kv_gain, rope64, tok):
    M, W = cqkv.shape
    ql, kvl = q_gain.shape[-1], kv_gain.shape[-1]
    bpb = tok.blocks_per_batch
    tab = pl.BlockSpec((ROW_BLOCK, LANES), lambda i: (i % bpb, 0))
    return pl.pallas_call(
        functools.partial(_mla_prep_body, ql=ql),
        grid=(M // ROW_BLOCK,),
        in_specs=[pl.BlockSpec((ROW_BLOCK, W), lambda i: (i, 0)),
                  pl.BlockSpec((ROW_BLOCK, LANES), lambda i: (i, 0)),
                  pl.BlockSpec((1, ql), lambda i: (0, 0)),
                  pl.BlockSpec((1, kvl), lambda i: (0, 0)), tab, tab, tab],
        out_specs=[pl.BlockSpec((ROW_BLOCK, ql), lambda i: (i, 0)),
                   pl.BlockSpec((ROW_BLOCK, kvl), lambda i: (i, 0)),
                   pl.BlockSpec((ROW_BLOCK, LANES), lambda i: (i, 0))],
        out_shape=[jax.ShapeDtypeStruct((M, ql), BF16), jax.ShapeDtypeStruct((M, kvl), BF16),
                   jax.ShapeDtypeStruct((M, LANES), BF16)],
        compiler_params=_cparams(1), name="mla_prep")(
            cqkv, kr, q_gain.reshape(1, ql), kv_gain.reshape(1, kvl), *rope64)


def _gqa_prep_body(p_ref, qg_ref, kg_ref, c_ref, ss_ref, q_ref, k_ref, v_ref, *, n_q, n_kv, scale):
    c, ss = c_ref[...], ss_ref[...]
    for h in range(n_q):
        xh = p_ref[:, h * HEAD_DIM:(h + 1) * HEAD_DIM]
        q_ref[:, h * HEAD_DIM:(h + 1) * HEAD_DIM] = (_rope128(_rms(xh, qg_ref[...]), c, ss) * scale).astype(BF16)
    off = n_q * HEAD_DIM
    for h in range(n_kv):
        xh = p_ref[:, off + h * HEAD_DIM:off + (h + 1) * HEAD_DIM]
        k_ref[:, h * HEAD_DIM:(h + 1) * HEAD_DIM] = _rope128(_rms(xh, kg_ref[...]), c, ss).astype(BF16)
    off += n_kv * HEAD_DIM
    for h in range(n_kv):
        v_ref[:, 2 * h * HEAD_DIM:(2 * h + 1) * HEAD_DIM] = p_ref[:, off + h * HEAD_DIM:off + (h + 1) * HEAD_DIM].astype(BF16)
        v_ref[:, (2 * h + 1) * HEAD_DIM:(2 * h + 2) * HEAD_DIM] = jnp.ones((ROW_BLOCK, HEAD_DIM), BF16)


def _gqa_prep(p, q_gain, k_gain, rope128, n_q, n_kv, tok):
    M, W = p.shape
    bpb = tok.blocks_per_batch
    tab = pl.BlockSpec((ROW_BLOCK, LANES), lambda i: (i % bpb, 0))
    wq, wk = n_q * HEAD_DIM, n_kv * HEAD_DIM
    return pl.pallas_call(
        functools.partial(_gqa_prep_body, n_q=n_q, n_kv=n_kv, scale=HEAD_DIM ** -0.5 * LOG2E),
        grid=(M // ROW_BLOCK,),
        in_specs=[pl.BlockSpec((ROW_BLOCK, W), lambda i: (i, 0)),
                  pl.BlockSpec((1, HEAD_DIM), lambda i: (0, 0)),
                  pl.BlockSpec((1, HEAD_DIM), lambda i: (0, 0)), tab, tab],
        out_specs=[pl.BlockSpec((ROW_BLOCK, wq), lambda i: (i, 0)),
                   pl.BlockSpec((ROW_BLOCK, wk), lambda i: (i, 0)),
                   pl.BlockSpec((ROW_BLOCK, 2 * wk), lambda i: (i, 0))],
        out_shape=[jax.ShapeDtypeStruct((M, wq), BF16), jax.ShapeDtypeStruct((M, wk), BF16),
                   jax.ShapeDtypeStruct((M, 2 * wk), BF16)],
        compiler_params=_cparams(1), name="gqa_prep")(
            p, q_gain.reshape(1, HEAD_DIM), k_gain.reshape(1, HEAD_DIM), *rope128)


def _lane_fold(x, op, init):
    for j in range(x.shape[1] // LANES):
        init = op(init, x[:, j * LANES:(j + 1) * LANES])
    return init


def _score_pass(q, k_ref, s_ref, *, n_k, bk, s_scale):
    m_lane = jnp.full((q.shape[0], LANES), -jnp.inf, F32)
    for c in range(n_k):
        s = lax.dot_general(q, k_ref[0, c * bk:(c + 1) * bk, :], (((1,), (1,)), ((), ())),
                            preferred_element_type=F32)
        if s_scale is not None:
            s = s * s_scale
        s_ref[c] = s
        m_lane = _lane_fold(s, jnp.maximum, m_lane)
    return jnp.max(m_lane, axis=1, keepdims=True)


def _value_pass(s_ref, m, v_ref, *, n_k, bk, v_has_ones):
    acc = jnp.zeros((s_ref.shape[1], HEAD_DIM + LANES), F32)
    for c in range(n_k):
        p = jnp.exp2(s_ref[c] - m)
        v1 = v_ref[0, c * bk:(c + 1) * bk, :]
        if not v_has_ones:
            v1 = jnp.concatenate([v1, jnp.ones((bk, LANES), BF16)], axis=-1)
        acc = acc + jnp.dot(p.astype(BF16), v1, preferred_element_type=F32)
    return acc[:, :HEAD_DIM] / acc[:, HEAD_DIM:HEAD_DIM + 1]


def _attn_body(q_ref, k_ref, v_ref, o_ref, s_sc, *, v_has_ones, n_k, bk, s_scale):
    m = _score_pass(q_ref[0], k_ref, s_sc, n_k=n_k, bk=bk, s_scale=s_scale)
    o_ref[0] = _value_pass(s_sc, m, v_ref, n_k=n_k, bk=bk, v_has_ones=v_has_ones).astype(o_ref.dtype)


def _attn_pipelined_body(q_ref, k_ref, v_ref, o_ref, sa, sb, ma, mb, *, n_k, bk, s_scale):
    i = pl.program_id(1)

    @pl.when(i == 0)
    def _():
        sb[...] = jnp.zeros(sb.shape, F32)
        mb[...] = jnp.zeros(mb.shape, F32)

    def step(s_w, m_w, s_r, m_r):
        m = _score_pass(q_ref[0], k_ref, s_w, n_k=n_k, bk=bk, s_scale=s_scale)
        m_w[...] = jnp.broadcast_to(m, m_w.shape)
        o_ref[0] = _value_pass(s_r, m_r[:, :1], v_ref, n_k=n_k, bk=bk, v_has_ones=True).astype(o_ref.dtype)

    @pl.when(i % 2 == 0)
    def _():
        step(sa, ma, sb, mb)

    @pl.when(i % 2 == 1)
    def _():
        step(sb, mb, sa, ma)


def _attention(q, k, v, *, n_heads, dq, dk, k_lane, v_lane, v_has_ones, bq, q_off, n_q, bk, k_off, n_k, name,
               pipelined, s_scale=None):
    B = q.shape[0]
    tk = n_k * bk
    dv = HEAD_DIM + LANES if v_has_ones else HEAD_DIM
    out_shape = jax.ShapeDtypeStruct((B, n_q * bq, n_heads * HEAD_DIM), BF16)
    if not pipelined:
        return pl.pallas_call(
            functools.partial(_attn_body, v_has_ones=v_has_ones, n_k=n_k, bk=bk, s_scale=s_scale),
            grid=(B, n_heads, n_q),
            in_specs=[pl.BlockSpec((1, bq, dq), lambda b, h, i: (b, i + q_off, h)),
                      pl.BlockSpec((1, tk, dk), lambda b, h, i: (b, k_off, k_lane(h))),
                      pl.BlockSpec((1, tk, dv), lambda b, h, i: (b, k_off, v_lane(h)))],
            out_specs=pl.BlockSpec((1, bq, HEAD_DIM), lambda b, h, i: (b, i, h)),
            out_shape=out_shape, scratch_shapes=[pltpu.VMEM((n_k, bq, bk), F32)],
            compiler_params=_cparams(3), name=name)(q, k, v)
    assert v_has_ones
    resident = dict(pipeline_mode=pl.Buffered(1))
    n_steps = n_heads * n_q

    def scored(t):
        t = jnp.minimum(t, n_steps - 1)
        return t // n_q, t % n_q

    def finished(t):
        t = jnp.maximum(t - 1, 0)
        return t // n_q, t % n_q

    return pl.pallas_call(
        functools.partial(_attn_pipelined_body, n_k=n_k, bk=bk, s_scale=s_scale),
        grid=(B, n_steps + 1),
        in_specs=[pl.BlockSpec((1, bq, dq), lambda b, t: (b, scored(t)[1] + q_off, scored(t)[0])),
                  pl.BlockSpec((1, tk, dk), lambda b, t: (b, k_off, k_lane(scored(t)[0])), **resident),
                  pl.BlockSpec((1, tk, dv), lambda b, t: (b, k_off, v_lane(finished(t)[0])), **resident)],
        out_specs=pl.BlockSpec((1, bq, HEAD_DIM), lambda b, t: (b, finished(t)[1], finished(t)[0])),
        out_shape=out_shape,
        scratch_shapes=[pltpu.VMEM((n_k, bq, bk), F32), pltpu.VMEM((n_k, bq, bk), F32),
                        pltpu.VMEM((bq, LANES), F32), pltpu.VMEM((bq, LANES), F32)],
        compiler_params=_cparams(2), name=name)(q, k, v)


NA_Q_ROWS = 8
NA_K_ROWS = 16
NA_CHUNK = 256
NA_N_CHUNKS = NA_K_ROWS * GRID_W // NA_CHUNK
NA_HEADS_PER_STEP = 4


def _na_body(*refs, scale, hps):
    q_ref = refs[0]
    k_refs = refs[1:1 + NA_N_CHUNKS + 1]
    v_refs = refs[2 + NA_N_CHUNKS:3 + 2 * NA_N_CHUNKS]
    bias_ref = refs[3 + 2 * NA_N_CHUNKS]
    o_ref = refs[4 + 2 * NA_N_CHUNKS]
    for hh in range(hps):
        lanes = slice(hh * HEAD_DIM, (hh + 1) * HEAD_DIM)
        q = q_ref[0, :, lanes]
        parts = [lax.dot_general(q, kr[0, :, lanes], (((1,), (1,)), ((), ())), preferred_element_type=F32) * scale
                 for kr in k_refs]
        s_loc = jnp.concatenate(parts[:NA_N_CHUNKS], axis=1) + bias_ref[0, hh]
        s = jnp.concatenate([s_loc, parts[NA_N_CHUNKS]], axis=1)
        m = jnp.max(s, axis=1, keepdims=True)
        p = jnp.exp(s - m)
        l = jnp.sum(p, axis=1, keepdims=True)
        pb = p.astype(BF16)
        acc = None
        for j, vr in enumerate(v_refs):
            t = jnp.dot(pb[:, j * NA_CHUNK:(j + 1) * NA_CHUNK], vr[0, :, lanes], preferred_element_type=F32)
            acc = t if acc is None else acc + t
        o_ref[0, :, lanes] = (acc / l).astype(o_ref.dtype)


def _na_bias_tables(rpb, rows):
    n_groups = rows // NA_Q_ROWS
    n_h = rpb.shape[0]
    n_dr, n_dc = 2 * NA_WIN_H - 1, 2 * NA_WIN_W - 1
    exact = lax.Precision.HIGHEST
    qc = np.arange(GRID_W)[:, None]
    kc = np.arange(GRID_W)[None, :]
    cs = np.clip(qc - NA_WIN_W // 2, 0, GRID_W - NA_WIN_W)
    col_valid = (kc >= cs) & (kc < cs + NA_WIN_W)
    dc = np.clip(kc - qc + (NA_WIN_W - 1), 0, n_dc - 1)
    oh_c = (np.arange(n_dc)[:, None, None] == dc[None]).astype(np.float32).reshape(n_dc, GRID_W * GRID_W)
    cols = jnp.einsum('hrd,dx->hrx', rpb.astype(F32), jnp.asarray(oh_c), precision=exact)
    tabs = []
    j = np.arange(NA_Q_ROWS)[:, None]
    i = np.arange(NA_K_ROWS)[None, :]
    for g in (0, 1, n_groups - 1):
        qr = NA_Q_ROWS * g + j
        kr = np.clip(NA_Q_ROWS * g - NA_WIN_H // 2, 0, rows - NA_K_ROWS) + i
        rs = np.clip(qr - NA_WIN_H // 2, 0, rows - NA_WIN_H)
        row_valid = (kr >= rs) & (kr < rs + NA_WIN_H)
        dr = np.clip(kr - qr + (NA_WIN_H - 1), 0, n_dr - 1)
        oh_r = (np.arange(n_dr)[None, None, :] == dr[:, :, None]).astype(np.float32)
        oh_r = oh_r.reshape(NA_Q_ROWS * NA_K_ROWS, n_dr)
        t = jnp.einsum('pr,hrx->hpx', jnp.asarray(oh_r), cols, precision=exact)
        t = t.reshape(n_h, NA_Q_ROWS, NA_K_ROWS, GRID_W, GRID_W)
        valid = row_valid[:, :, None, None] & col_valid[None, None]
        t = jnp.where(valid[None], t, NEG_BIG).transpose(0, 1, 3, 2, 4)
        tabs.append(t.reshape(n_h, NA_Q_ROWS * GRID_W, NA_K_ROWS * GRID_W))
    return jnp.stack(tabs)


def _na_attention(qkv, bias_tab, *, n_heads, seq, ctx_chunk):
    B = qkv.shape[0]
    rows = seq // GRID_W
    n_groups = rows // NA_Q_ROWS
    bq = NA_Q_ROWS * GRID_W
    max_cb = seq // NA_CHUNK - NA_N_CHUNKS

    def cb(g):
        return jnp.clip(2 * g - 1, 0, max_cb)

    hps = math.gcd(n_heads, NA_HEADS_PER_STEP)
    hw = hps * HEAD_DIM
    n_hblk = n_heads // hps
    in_specs = [pl.BlockSpec((1, bq, hw), lambda b, g, h: (b, g, h))]
    for part in (1, 2):
        for c in range(NA_N_CHUNKS):
            in_specs.append(pl.BlockSpec((1, NA_CHUNK, hw),
                                         lambda b, g, h, c=c, part=part: (b, cb(g) + c, part * n_hblk + h)))
        in_specs.append(pl.BlockSpec((1, NA_CHUNK, hw),
                                     lambda b, g, h, part=part: (b, ctx_chunk, part * n_hblk + h)))
    in_specs.append(pl.BlockSpec(
        (1, hps, bq, NA_K_ROWS * GRID_W),
        lambda b, g, h: (jnp.where(g == 0, 0, jnp.where(g == n_groups - 1, 2, 1)), h, 0, 0)))
    n_in = 2 * (NA_N_CHUNKS + 1)
    return pl.pallas_call(
        functools.partial(_na_body, scale=HEAD_DIM ** -0.5, hps=hps),
        grid=(B, n_groups, n_hblk), in_specs=in_specs,
        out_specs=pl.BlockSpec((1, bq, hw), lambda b, g, h: (b, g, h)),
        out_shape=jax.ShapeDtypeStruct((B, seq, n_heads * HEAD_DIM), BF16),
        compiler_params=_cparams(3), name="na_attention")(qkv, *([qkv] * n_in), bias_tab)


def _softplus(z):
    return jnp.maximum(z, 0.0) + jnp.log(1.0 + jnp.exp(-jnp.abs(z)))


def _lru_gates_body(x_ref, hp_ref, hn_ref, cw_ref, cb_ref, w_ref, ba_ref, bx_ref, lam_ref,
                    af_ref, bf_ref, ab_ref, bb_ref, xs_ref, *, bpb, lat_blocks, n_blocks_lru):
    i = pl.program_id(0)
    j = i % bpb
    first = jnp.logical_or(j == 0, j == lat_blocks)
    last = jnp.logical_or(j == lat_blocks - 1, j == bpb - 1)
    x = x_ref[...]
    xs_ref[SUBLANES:SUBLANES + ROW_BLOCK, :] = x
    xs_ref[0:SUBLANES, :] = jnp.where(first, 0.0, hp_ref[...])
    xs_ref[SUBLANES + ROW_BLOCK:, :] = jnp.where(last, 0.0, hn_ref[...])
    xc = cb_ref[...] + cw_ref[2:3, :] * x
    for tap, off in ((0, -2), (1, -1), (3, 1)):
        xc = xc + cw_ref[tap:tap + 1, :] * xs_ref[SUBLANES + off:SUBLANES + off + ROW_BLOCK, :]
    xcb = xc.astype(BF16)
    zs = [jnp.dot(xcb[:, n * LANES:(n + 1) * LANES], w_ref[n].astype(BF16), preferred_element_type=F32)
          for n in range(n_blocks_lru)]
    for d, (a_ref, b_ref) in enumerate(((af_ref, bf_ref), (ab_ref, bb_ref))):
        za = jnp.concatenate([z[:, (2 * d) * LANES:(2 * d + 1) * LANES] for z in zs], axis=1)
        zx = jnp.concatenate([z[:, (2 * d + 1) * LANES:(2 * d + 2) * LANES] for z in zs], axis=1)
        r = jax.nn.sigmoid(za + ba_ref[d:d + 1, :])
        ig = jax.nn.sigmoid(zx + bx_ref[d:d + 1, :])
        log_a = (-LRU_C) * r * _softplus(-lam_ref[d:d + 1, :])
        a = jnp.exp(log_a)
        a_ref[...] = a
        b_ref[...] = jnp.sqrt(1.0 - jnp.exp(2.0 * log_a)) * (ig * xc)


def _lru_gates(p_lru, conv_w, conv_b, w_cat, b_a, b_x, lam, tok):
    M = p_lru.shape[0]
    W = conv_b.shape[-1]
    nb = W // LANES
    rb8 = ROW_BLOCK // SUBLANES
    n_tiles8 = M // SUBLANES
    row = pl.BlockSpec((ROW_BLOCK, W), lambda i: (i, 0))
    vec2 = pl.BlockSpec((2, W), lambda i: (0, 0))
    return pl.pallas_call(
        functools.partial(_lru_gates_body, bpb=tok.blocks_per_batch, lat_blocks=tok.lat_blocks, n_blocks_lru=nb),
        grid=(M // ROW_BLOCK,),
        in_specs=[row,
                  pl.BlockSpec((SUBLANES, W), lambda i: (jnp.maximum(i * rb8 - 1, 0), 0)),
                  pl.BlockSpec((SUBLANES, W), lambda i: (jnp.minimum((i + 1) * rb8, n_tiles8 - 1), 0)),
                  pl.BlockSpec((LRU_CONV, W), lambda i: (0, 0)),
                  pl.BlockSpec((1, W), lambda i: (0, 0)),
                  pl.BlockSpec((nb, LANES, 4 * LANES), lambda i: (0, 0, 0)),
                  vec2, vec2, vec2],
        out_specs=[row] * 4,
        out_shape=[jax.ShapeDtypeStruct((M, W), F32)] * 4,
        scratch_shapes=[pltpu.VMEM((ROW_BLOCK + 2 * SUBLANES, W), F32)],
        compiler_params=_cparams(1), name="lru_gates")(
            p_lru, p_lru, p_lru, conv_w, conv_b.reshape(1, W), w_cat, b_a, b_x, lam)


LRU_UNROLL = 8


def _lru_scan_body(af_ref, bf_ref, ab_ref, bb_ref, hf_ref, hb_ref, carry_ref):
    @pl.when(pl.program_id(1) == 0)
    def _():
        carry_ref[...] = jnp.zeros(carry_ref.shape, F32)

    n = af_ref.shape[1]

    def step(t, hs):
        hf, hb = hs
        tb = n - 1 - t
        hf = af_ref[0, t] * hf + bf_ref[0, t]
        hb = ab_ref[0, tb] * hb + bb_ref[0, tb]
        hf_ref[0, t] = hf
        hb_ref[0, tb] = hb
        return hf, hb

    hf, hb = lax.fori_loop(0, n, step, (carry_ref[0], carry_ref[1]), unroll=LRU_UNROLL)
    carry_ref[0] = hf
    carry_ref[1] = hb


def _lru_scan(a_f, b_f, a_b, b_b, tok):
    B, T, R, _ = a_f.shape
    lat = tok.lat_blocks
    n_chunks = tok.blocks_per_batch

    def fwd(b, k):
        return (b, jnp.where(k == 0, lat, k - 1), 0, 0)

    def bwd(b, k):
        return (b, jnp.where(k == 0, lat, lat - k), 0, 0)

    blk = (1, ROW_BLOCK, R, LANES)
    return pl.pallas_call(
        _lru_scan_body, grid=(B, n_chunks),
        in_specs=[pl.BlockSpec(blk, fwd), pl.BlockSpec(blk, fwd), pl.BlockSpec(blk, bwd), pl.BlockSpec(blk, bwd)],
        out_specs=[pl.BlockSpec(blk, fwd), pl.BlockSpec(blk, bwd)],
        out_shape=[jax.ShapeDtypeStruct(a_f.shape, F32)] * 2,
        scratch_shapes=[pltpu.VMEM((2, R, LANES), F32)],
        compiler_params=_cparams(2), name="lru_scan")(a_f, b_f, a_b, b_b)


def _gelu_tanh(x):
    return 0.5 * x * (1.0 + jnp.tanh(math.sqrt(2.0 / math.pi) * (x + 0.044715 * (x * x * x))))


def _lru_out_body(hf_ref, hb_ref, g_ref, y_ref):
    y_ref[...] = ((hf_ref[...] + hb_ref[...]) * _gelu_tanh(g_ref[...])).astype(y_ref.dtype)


def _lru_out(hf, hb, p_lru):
    M, W = hf.shape
    row = pl.BlockSpec((ROW_BLOCK, W), lambda i: (i, 0))
    return pl.pallas_call(
        _lru_out_body, grid=(M // ROW_BLOCK,),
        in_specs=[row, row, pl.BlockSpec((ROW_BLOCK, W), lambda i: (i, 1))],
        out_specs=row, out_shape=jax.ShapeDtypeStruct((M, W), BF16),
        compiler_params=_cparams(1), name="lru_out")(hf, hb, p_lru)


def _merge_body(*refs, has_prev):
    if has_prev:
        h_ref, y_ref, wg_ref, wb_ref, bg_ref, prev_ref, o_ref = refs
    else:
        h_ref, y_ref, wg_ref, wb_ref, bg_ref, o_ref = refs
    gate = jax.nn.sigmoid(jnp.dot(h_ref[...], wg_ref[...], preferred_element_type=F32) + bg_ref[0])
    term = gate * jnp.dot(y_ref[...], wb_ref[...], preferred_element_type=F32)
    if has_prev:
        term = term + prev_ref[...]
    o_ref[...] = term.astype(o_ref.dtype)


def _merge_branch(h, y, w_gate, b_gate, w_branch, prev, *, layer, branch, bm, bn, out_dtype):
    M, D = h.shape
    kb = y.shape[1]
    in_specs = [pl.BlockSpec((bm, D), lambda n, m: (m, 0)),
                pl.BlockSpec((bm, kb), lambda n, m: (m, 0)),
                pl.BlockSpec((None, None, D, bn), lambda n, m: (layer, branch, 0, n)),
                pl.BlockSpec((None, None, kb, bn), lambda n, m: (layer, branch, 0, n)),
                pl.BlockSpec((None, 1, 1, bn), lambda n, m: (layer, branch, 0, n))]
    args = [h, y, w_gate, w_branch, b_gate.reshape(b_gate.shape[0], b_gate.shape[1], 1, D)]
    if prev is not None:
        in_specs.append(pl.BlockSpec((bm, bn), lambda n, m: (m, n)))
        args.append(prev)
    return pl.pallas_call(
        functools.partial(_merge_body, has_prev=prev is not None),
        grid=(D // bn, M // bm), in_specs=in_specs,
        out_specs=pl.BlockSpec((bm, bn), lambda n, m: (m, n)),
        out_shape=jax.ShapeDtypeStruct((M, D), out_dtype),
        compiler_params=_cparams(2), name=f"merge_branch{branch}")(*args)


ROUTER_BLOCK = 512
EXPERT_BLOCK = 512


def _router_body(h_ref, wr_ref, rb_ref, tri_ref, ids_ref, wts_ref, cnt_ref, carry_ref, *, n_exp):
    @pl.when(pl.program_id(0) == 0)
    def _():
        carry_ref[...] = jnp.zeros(carry_ref.shape, F32)

    per = n_exp // N_GROUPS
    logits = lax.dot_general(wr_ref[...], h_ref[...], (((1,), (1,)), ((), ())), preferred_element_type=F32)
    scores = jax.nn.sigmoid(logits)
    sel = scores + rb_ref[:, :1]
    sel_r = [sel[e:e + 1, :] for e in range(n_exp)]
    sc_r = [scores[e:e + 1, :] for e in range(n_exp)]
    gs = []
    for g in range(N_GROUPS):
        rows = sel_r[g * per:(g + 1) * per]
        best = None
        for a in range(per):
            for b in range(a + 1, per):
                ps = rows[a] + rows[b]
                best = ps if best is None else jnp.maximum(best, ps)
        gs.append(best)
    bestg = jnp.full(gs[0].shape, N_GROUPS - 1, I32)
    run = gs[N_GROUPS - 1]
    for g in range(N_GROUPS - 2, -1, -1):
        take = gs[g] >= run
        bestg = jnp.where(take, g, bestg)
        run = jnp.where(take, gs[g], run)
    v = [sel_r[i] for i in range(per)]
    sc = [sc_r[i] for i in range(per)]
    for g in range(1, N_GROUPS):
        isg = bestg == g
        v = [jnp.where(isg, sel_r[g * per + i], v[i]) for i in range(per)]
        sc = [jnp.where(isg, sc_r[g * per + i], sc[i]) for i in range(per)]

    def first_argmax(vals):
        idx = jnp.full(vals[0].shape, per - 1, I32)
        mx = vals[per - 1]
        sv = sc[per - 1]
        for i in range(per - 2, -1, -1):
            take = vals[i] >= mx
            idx = jnp.where(take, i, idx)
            mx = jnp.where(take, vals[i], mx)
            sv = jnp.where(take, sc[i], sv)
        return idx, sv

    i1, s1 = first_argmax(v)
    v2 = [jnp.where(i1 == i, -jnp.inf, v[i]) for i in range(per)]
    i2, s2 = first_argmax(v2)
    denom = s1 + s2
    e0 = bestg * per + i1
    e1 = bestg * per + i2
    eid = lax.broadcasted_iota(I32, scores.shape, 0)
    oh0 = (eid == e0).astype(F32)
    oh1 = (eid == e1).astype(F32)
    oh = oh0 + oh1
    prefix = jnp.dot(oh.astype(BF16), tri_ref[...], preferred_element_type=F32) + carry_ref[:, :1]
    r0 = jnp.sum(oh0 * prefix, axis=0, keepdims=True)
    r1 = jnp.sum(oh1 * prefix, axis=0, keepdims=True)
    new_carry = carry_ref[:, :1] + jnp.sum(oh, axis=1, keepdims=True)
    carry_ref[...] = jnp.broadcast_to(new_carry, carry_ref.shape)
    cnt_ref[...] = jnp.broadcast_to(new_carry, cnt_ref.shape)
    ids_ref[...] = jnp.zeros(ids_ref.shape, I32)
    wts_ref[...] = jnp.zeros(wts_ref.shape, F32)
    for r, val in enumerate((e0, e1, r0.astype(I32), r1.astype(I32))):
        ids_ref[r:r + 1, :] = val
    wts_ref[0:1, :] = s1 / denom
    wts_ref[1:2, :] = s2 / denom


def _router(h2, w_router, router_bias):
    M, D = h2.shape
    n_exp = w_router.shape[1]
    bm = ROUTER_BLOCK
    tri = jnp.asarray(np.triu(np.ones((bm, bm), np.float32), k=1), BF16)
    wr_t = w_router.T.astype(BF16)
    rb = jnp.broadcast_to(router_bias.astype(F32)[:, None], (n_exp, LANES))
    return pl.pallas_call(
        functools.partial(_router_body, n_exp=n_exp),
        grid=(M // bm,),
        in_specs=[pl.BlockSpec((bm, D), lambda i: (i, 0)),
                  pl.BlockSpec((n_exp, D), lambda i: (0, 0)),
                  pl.BlockSpec((n_exp, LANES), lambda i: (0, 0)),
                  pl.BlockSpec((bm, bm), lambda i: (0, 0))],
        out_specs=[pl.BlockSpec((SUBLANES, bm), lambda i: (0, i)),
                   pl.BlockSpec((SUBLANES, bm), lambda i: (0, i)),
                   pl.BlockSpec((n_exp, LANES), lambda i: (0, 0))],
        out_shape=[jax.ShapeDtypeStruct((SUBLANES, M), I32), jax.ShapeDtypeStruct((SUBLANES, M), F32),
                   jax.ShapeDtypeStruct((n_exp, LANES), F32)],
        scratch_shapes=[pltpu.VMEM((n_exp, LANES), F32)],
        compiler_params=_cparams(1), name="moe_router")(h2, wr_t, rb, tri)


DISPATCH_ROWS = 256


def _dispatch_body(n_live_ref, src_ref, h_hbm, xs_ref, sem):
    def copy(t):
        return pltpu.make_async_copy(h_hbm.at[pl.ds(src_ref[0, 0, t], 1)], xs_ref.at[pl.ds(t, 1)], sem)

    def start(t, c):
        copy(t).start()
        return c

    def wait(t, c):
        copy(t).wait()
        return c

    live = pl.program_id(0) * DISPATCH_ROWS < n_live_ref[0]

    @pl.when(live)
    def _():
        lax.fori_loop(0, DISPATCH_ROWS, start, 0)
        lax.fori_loop(0, DISPATCH_ROWS, wait, 0)

    @pl.when(jnp.logical_not(live))
    def _():
        xs_ref[...] = jnp.zeros(xs_ref.shape, xs_ref.dtype)


def _dispatch(h3, src, n_live, n_slots):
    slab = h3.shape[1:]
    return pl.pallas_call(
        _dispatch_body,
        grid_spec=pltpu.PrefetchScalarGridSpec(
            num_scalar_prefetch=1, grid=(n_slots // DISPATCH_ROWS,),
            in_specs=[pl.BlockSpec((1, 1, DISPATCH_ROWS), lambda i, nl: (i, 0, 0), memory_space=pltpu.SMEM),
                      pl.BlockSpec(memory_space=pl.ANY)],
            out_specs=pl.BlockSpec((DISPATCH_ROWS,) + slab, lambda i, nl: (i, 0, 0)),
            scratch_shapes=[pltpu.SemaphoreType.DMA(())]),
        out_shape=jax.ShapeDtypeStruct((n_slots,) + slab, h3.dtype),
        compiler_params=_cparams(1), name="moe_dispatch")(n_live, src, h3)


def _expert_up_body(be_ref, nu_ref, x_ref, wg_ref, wu_ref, o_ref, wg_bf, wu_bf):
    i = pl.program_id(1)
    fresh = jnp.logical_or(i == 0, be_ref[i] != be_ref[jnp.maximum(i - 1, 0)])

    @pl.when(jnp.logical_and(fresh, i < nu_ref[0]))
    def _():
        wg_bf[...] = wg_ref[...].astype(BF16)
        wu_bf[...] = wu_ref[...].astype(BF16)

    @pl.when(i < nu_ref[0])
    def _():
        x = x_ref[...]
        g = jnp.dot(x, wg_bf[...], preferred_element_type=F32)
        u = jnp.dot(x, wu_bf[...], preferred_element_type=F32)
        o_ref[...] = (g * jax.nn.sigmoid(g) * u).astype(o_ref.dtype)

    @pl.when(i >= nu_ref[0])
    def _():
        o_ref[...] = jnp.zeros(o_ref.shape, o_ref.dtype)


def _expert_up(xs, w_gate, w_up, blk_expert, n_used, *, layer, bn):
    P, D = xs.shape
    de = w_gate.shape[-1]
    nblk = P // EXPERT_BLOCK

    def xmap(n, i, be, nu):
        return (jnp.minimum(i, nu[0] - 1), 0)

    def wmap(n, i, be, nu):
        return (layer, be[i], 0, n)

    return pl.pallas_call(
        _expert_up_body,
        grid_spec=pltpu.PrefetchScalarGridSpec(
            num_scalar_prefetch=2, grid=(de // bn, nblk),
            in_specs=[pl.BlockSpec((EXPERT_BLOCK, D), xmap),
                      pl.BlockSpec((None, None, D, bn), wmap),
                      pl.BlockSpec((None, None, D, bn), wmap)],
            out_specs=pl.BlockSpec((EXPERT_BLOCK, bn), lambda n, i, be, nu: (i, n)),
            scratch_shapes=[pltpu.VMEM((D, bn), BF16), pltpu.VMEM((D, bn), BF16)]),
        out_shape=jax.ShapeDtypeStruct((P, de), BF16),
        compiler_params=_cparams(2), name="moe_expert_up")(blk_expert, n_used, xs, w_gate, w_up)


def _expert_down_body(be_ref, nu_ref, x_ref, wd_ref, o_ref, wd_bf):
    i = pl.program_id(1)
    fresh = jnp.logical_or(i == 0, be_ref[i] != be_ref[jnp.maximum(i - 1, 0)])

    @pl.when(jnp.logical_and(fresh, i < nu_ref[0]))
    def _():
        wd_bf[...] = wd_ref[...].astype(BF16)

    @pl.when(i < nu_ref[0])
    def _():
        o_ref[...] = jnp.dot(x_ref[...], wd_bf[...], preferred_element_type=F32).astype(o_ref.dtype)

    @pl.when(i >= nu_ref[0])
    def _():
        o_ref[...] = jnp.zeros(o_ref.shape, o_ref.dtype)


def _expert_down(he, w_down, blk_expert, n_used, *, layer, bn):
    P, de = he.shape
    D = w_down.shape[-1]
    nblk = P // EXPERT_BLOCK
    return pl.pallas_call(
        _expert_down_body,
        grid_spec=pltpu.PrefetchScalarGridSpec(
            num_scalar_prefetch=2, grid=(D // bn, nblk),
            in_specs=[pl.BlockSpec((EXPERT_BLOCK, de), lambda n, i, be, nu: (jnp.minimum(i, nu[0] - 1), 0)),
                      pl.BlockSpec((None, None, de, bn), lambda n, i, be, nu: (layer, be[i], 0, n))],
            out_specs=pl.BlockSpec((EXPERT_BLOCK, bn), lambda n, i, be, nu: (i, n)),
            scratch_shapes=[pltpu.VMEM((de, bn), BF16)]),
        out_shape=jax.ShapeDtypeStruct((P, D), BF16),
        compiler_params=_cparams(2), name="moe_expert_down")(blk_expert, n_used, he, w_down)


COMBINE_UNROLL = 8


def _combine_body(pos_ref, w_ref, o_hbm, y_ref, buf, sem):
    def copy(t, k):
        return pltpu.make_async_copy(o_hbm.at[pl.ds(pos_ref[0, k, t], 1)], buf.at[k, pl.ds(t, 1)], sem)

    def start(t, c):
        copy(t, 0).start()
        copy(t, 1).start()
        return c

    def wait(t, c):
        copy(t, 0).wait()
        copy(t, 1).wait()
        return c

    def mix(t, c):
        y_ref[t] = w_ref[0, 0, t] * buf[0, t].astype(F32) + w_ref[0, 1, t] * buf[1, t].astype(F32)
        return c

    lax.fori_loop(0, DISPATCH_ROWS, start, 0)
    lax.fori_loop(0, DISPATCH_ROWS, wait, 0)
    lax.fori_loop(0, DISPATCH_ROWS, mix, 0, unroll=COMBINE_UNROLL)


def _combine(o3, pos, wts, n_rows):
    slab = o3.shape[1:]
    smem = pl.BlockSpec((1, 2, DISPATCH_ROWS), lambda i: (i, 0, 0), memory_space=pltpu.SMEM)
    return pl.pallas_call(
        _combine_body, grid=(n_rows // DISPATCH_ROWS,),
        in_specs=[smem, smem, pl.BlockSpec(memory_space=pl.ANY)],
        out_specs=pl.BlockSpec((DISPATCH_ROWS,) + slab, lambda i: (i, 0, 0)),
        out_shape=jax.ShapeDtypeStruct((n_rows,) + slab, F32),
        scratch_shapes=[pltpu.VMEM((2, DISPATCH_ROWS) + slab, o3.dtype), pltpu.SemaphoreType.DMA(())],
        compiler_params=_cparams(1), name="moe_combine")(pos, wts, o3)


def _moe(h2, w_router, router_bias, w_gate, w_up, w_down, *, layer):
    M, D = h2.shape
    n_exp = w_router.shape[1]
    ids, wts, cnt = _router(h2, w_router, router_bias)
    counts = cnt[:, 0].astype(I32)
    padded = (counts + EXPERT_BLOCK - 1) // EXPERT_BLOCK * EXPERT_BLOCK
    ends = jnp.cumsum(padded)
    offs = ends - padded
    n_slots = (2 * M + n_exp * (EXPERT_BLOCK - 1)) // EXPERT_BLOCK * EXPERT_BLOCK
    nblk = n_slots // EXPERT_BLOCK
    pos0 = offs[ids[0]] + ids[2]
    pos1 = offs[ids[1]] + ids[3]
    pos = jnp.stack([pos0.reshape(-1, DISPATCH_ROWS), pos1.reshape(-1, DISPATCH_ROWS)], axis=1)
    wpair = jnp.stack([wts[0].reshape(-1, DISPATCH_ROWS), wts[1].reshape(-1, DISPATCH_ROWS)], axis=1)
    blk_start = jnp.arange(nblk, dtype=I32) * EXPERT_BLOCK
    blk_expert = jnp.minimum(jnp.sum((ends[None, :] <= blk_start[:, None]).astype(I32), axis=1), n_exp - 1)
    n_used = (ends[-1] // EXPERT_BLOCK).astype(I32).reshape(1)
    tok_ids = jnp.arange(M, dtype=I32)
    src = jnp.zeros((n_slots,), I32).at[jnp.concatenate([pos0, pos1])].set(jnp.concatenate([tok_ids, tok_ids]))
    xs = _dispatch(h2.reshape(M, D // LANES, LANES), src.reshape(-1, 1, DISPATCH_ROWS),
                   ends[-1].astype(I32).reshape(1), n_slots).reshape(n_slots, D)
    he = _expert_up(xs, w_gate, w_up, blk_expert, n_used, layer=layer, bn=_pick(w_gate.shape[-1], (256, 128)))
    o = _expert_down(he, w_down, blk_expert, n_used, layer=layer, bn=_pick(D, (2048, 1024, 512)))
    return _combine(o.reshape(n_slots, D // LANES, LANES), pos, wpair, M).reshape(M, D)


class _Tok:
    def __init__(self, n_batch, seq, ctx):
        self.n_batch, self.seq, self.ctx = n_batch, seq, ctx
        self.t_all = seq + ctx
        self.lat_blocks = seq // ROW_BLOCK
        self.blocks_per_batch = self.t_all // ROW_BLOCK


def _rope_tables(seq, ctx):
    t = jnp.arange(seq, dtype=I32)
    row = (t // GRID_W).astype(F32)[:, None]
    col = (t % GRID_W).astype(F32)[:, None]

    def cs(rot_dim):
        n_freq = rot_dim // 4
        freqs = ROPE_THETA ** (-jnp.arange(n_freq, dtype=F32) / n_freq)
        ang = jnp.concatenate([row * freqs, col * freqs], axis=-1)
        cos = jnp.concatenate([jnp.cos(ang), jnp.ones((ctx, rot_dim // 2), F32)], axis=0)
        sin = jnp.concatenate([jnp.sin(ang), jnp.zeros((ctx, rot_dim // 2), F32)], axis=0)
        return cos, sin

    cos, sin = cs(HEAD_DIM)
    rope128 = (jnp.concatenate([cos, cos], axis=1), jnp.concatenate([-sin, sin], axis=1))
    cos, sin = cs(MLA_ROPE)
    z32 = jnp.zeros_like(sin)
    z64 = jnp.zeros((seq + ctx, LANES - MLA_ROPE), F32)
    rope64 = (jnp.concatenate([cos, cos, z64], axis=1), jnp.concatenate([sin, z32, z64], axis=1),
              jnp.concatenate([z32, sin, z64], axis=1))
    return rope128, rope64


def _q_rope_epilogue(acc, c_ref, slo_ref, shi_ref, *, scale):
    nope = acc[:, :MLA_NOPE] * scale
    rope = _rope64(acc[:, MLA_NOPE:], c_ref[...], slo_ref[...], shi_ref[...]) * scale
    return jnp.concatenate([nope, rope], axis=1)


def _kv_assemble_epilogue(acc, krr_ref):
    keys = jnp.concatenate([acc[:, :MLA_NOPE], krr_ref[...].astype(F32)], axis=1)
    vals = jnp.concatenate([acc[:, MLA_NOPE:], jnp.ones((acc.shape[0], LANES), F32)], axis=1)
    return keys, vals


def kernel(x, c, ctx, c_ctx, w_mod, b_mod, norm_mix, norm_ffn, w_in, mla_q_norm, mla_w_uq, mla_kv_norm,
           mla_w_ukv, gqa_q_norm, gqa_k_norm, na_rpb, lru_conv_w, lru_conv_b, lru_w_a, lru_b_a, lru_w_x,
           lru_b_x, lru_lambda, w_branch_gate, b_branch_gate, w_branch, w_out, w_router, router_bias,
           w_exp_gate, w_exp_up, w_exp_down, final_norm):
    B, S, D = x.shape
    CTX = ctx.shape[1]
    depth = w_in.shape[0]
    tok = _Tok(B, S, CTX)
    T = tok.t_all
    M = B * T
    bw = D // 4
    n_heads = bw // HEAD_DIM
    ql, kvl = mla_q_norm.shape[-1], mla_kv_norm.shape[-1]
    d_in = w_in.shape[-1]
    n_kv = (d_in - (ql + kvl + MLA_ROPE + bw + 3 * bw + 2 * bw)) // (2 * HEAD_DIM)
    q_per_kv = n_heads // n_kv
    n_lru_blocks = lru_w_a.shape[2]
    assert bw // n_lru_blocks == LANES and S % ROW_BLOCK == 0 and CTX == ROW_BLOCK and S % (GRID_W * NA_Q_ROWS) == 0
    mla_dq = 2 * LANES
    mla_scale = (MLA_NOPE + MLA_ROPE) ** -0.5 * LOG2E
    bm = _pick(M, (768, 512, 256))
    bm_tok = _pick(T, (768, 256))

    rope128, rope64 = _rope_tables(S, CTX)
    xa = jnp.concatenate([x, ctx], axis=1).reshape(M, D)
    cc = jnp.zeros((MOD_ROWS, D), F32).at[:B].set(c).at[B].set(c_ctx)
    silu_c = (cc * jax.nn.sigmoid(cc)).astype(BF16)

    o_mla = ql + kvl
    o_gqa = o_mla + MLA_ROPE
    w_gqa = bw + 2 * n_kv * HEAD_DIM
    o_na = o_gqa + w_gqa
    o_lru = o_na + 3 * bw

    bn_mod = _pick(N_MOD * D, (512, 256))
    mods = []
    for layer in range(depth):
        modv = _matmul(silu_c, w_mod, w_prefix=(layer,), n_cols=N_MOD * D, bm=MOD_ROWS, bn=bn_mod, out_dtype=F32,
                       epilogue=lambda acc, b_ref: acc + b_ref[0],
                       extras=[(b_mod.reshape(depth, 1, N_MOD * D), (1, 1, bn_mod),
                                lambda n, m, layer=layer: (layer, 0, n))],
                       name="mod_proj")
        mods.append(modv[:B + 1].reshape(B + 1, N_MOD, D))
    wg_bf, wb_bf = w_branch_gate.astype(BF16), w_branch.astype(BF16)

    y_ffn = mod_prev = None
    for layer in range(depth):
        need_ctx = layer < depth - 1
        modv = mods[layer]
        xa, h = _resid_norm(xa, y_ffn, mod_prev, modv, norm_mix[layer], gate_idx=5, shift_idx=0, scale_idx=1,
                            out_dtype=BF16, tok=tok, name="norm_mix")
        cqkv = _matmul(h, w_in, w_prefix=(layer,), n_cols=o_mla, bm=bm, bn=_pick_bn(o_mla, 0), out_dtype=F32,
                       name="in_mla")
        w_kr = jnp.pad(w_in[layer, :, o_mla:o_gqa], ((0, 0), (0, LANES - MLA_ROPE)))
        kr = _matmul(h, w_kr, n_cols=LANES, bm=bm, bn=LANES, out_dtype=F32, name="in_kr")
        w_rest = w_in[layer, :, o_gqa:o_lru]
        p_gqa = _matmul(h, w_rest, n_cols=w_gqa, bm=bm, bn=_pick_bn(w_gqa, 0), out_dtype=F32, name="in_gqa")
        p_na = _matmul(h, w_rest, n_off=w_gqa, n_cols=3 * bw, bm=bm, bn=_pick_bn(3 * bw, w_gqa),
                       out_dtype=BF16, name="in_na")
        p_lru = _matmul(h, w_in[layer, :, o_lru:], n_cols=2 * bw, bm=bm, bn=_pick_bn(2 * bw, 0), out_dtype=F32,
                        name="in_lru")

        qn, kvn, krr = _mla_prep(cqkv, kr, mla_q_norm[layer], mla_kv_norm[layer], rope64, tok)
        w_uq = jnp.pad(mla_w_uq[layer].reshape(ql, n_heads, MLA_NOPE + MLA_ROPE),
                       ((0, 0), (0, 0), (0, mla_dq - MLA_NOPE - MLA_ROPE))).reshape(ql, n_heads * mla_dq)
        tpb = T // bm_tok
        tab_spec = ((bm_tok, LANES), lambda n, m: (m % tpb, 0))
        q_mla = _matmul(qn, w_uq, n_cols=n_heads * mla_dq, bm=bm_tok, bn=mla_dq, out_dtype=BF16,
                        epilogue=functools.partial(_q_rope_epilogue, scale=mla_scale),
                        extras=[(t, *tab_spec) for t in rope64], name="mla_uq")
        k_mla, v_mla = _matmul(kvn, mla_w_ukv, w_prefix=(layer,), n_cols=n_heads * mla_dq, bm=bm, bn=mla_dq,
                               out_dtype=BF16, n_out=2, epilogue=_kv_assemble_epilogue,
                               extras=[(krr, (bm, LANES), lambda n, m: (m, 0))], name="mla_ukv")
        bq = _pick(S, (512, 256))
        ctx_blk = S // ROW_BLOCK

        def lat_q(prefs):
            bk = _pick(T, prefs)
            return dict(bq=bq, q_off=0, n_q=S // bq, bk=bk, k_off=0, n_k=T // bk, pipelined=True)
        q3, k3, v3 = (a.reshape(B, T, -1) for a in (q_mla, k_mla, v_mla))
        mla_args = dict(n_heads=n_heads, dq=mla_dq, dk=mla_dq, k_lane=lambda h: h, v_lane=lambda h: h,
                        v_has_ones=True)
        ya = _attention(q3, k3, v3, name="mla_attn", **lat_q((768, 256)), **mla_args)
        q_g, k_g, v_g = (a.reshape(B, T, -1) for a in
                         _gqa_prep(p_gqa, gqa_q_norm[layer], gqa_k_norm[layer], rope128, n_heads, n_kv, tok))
        gqa_args = dict(n_heads=n_heads, dq=HEAD_DIM, dk=HEAD_DIM, k_lane=lambda h: h // q_per_kv,
                        v_lane=lambda h: h // q_per_kv, v_has_ones=True)
        yb = _attention(q_g, k_g, v_g, name="gqa_attn", **lat_q((1408, 768, 256)), **gqa_args)
        na3 = p_na.reshape(B, T, 3 * bw)
        yc = _na_attention(na3, _na_bias_tables(na_rpb[layer], S // GRID_W), n_heads=n_heads, seq=S,
                           ctx_chunk=S // NA_CHUNK)
        if need_ctx:
            ctx_q = dict(bq=ROW_BLOCK, q_off=ctx_blk, n_q=1, bk=ROW_BLOCK, k_off=ctx_blk, n_k=1, pipelined=False)
            ya_c = _attention(q3, k3, v3, name="mla_attn_ctx", **ctx_q, **mla_args)
            yb_c = _attention(q_g, k_g, v_g, name="gqa_attn_ctx", **ctx_q, **gqa_args)
            yc_c = _attention(na3, na3, na3, name="na_attn_ctx", n_heads=n_heads, dq=HEAD_DIM, dk=HEAD_DIM,
                              k_lane=lambda h: n_heads + h, v_lane=lambda h: 2 * n_heads + h, v_has_ones=False,
                              s_scale=HEAD_DIM ** -0.5 * LOG2E, **ctx_q)
        else:
            ya_c = yb_c = yc_c = jnp.zeros((B, CTX, bw), BF16)
        ya, yb, yc = (jnp.concatenate([l_, c_], axis=1).reshape(M, bw)
                      for l_, c_ in ((ya, ya_c), (yb, yb_c), (yc, yc_c)))
        w_cat = jnp.concatenate([lru_w_a[layer, 0], lru_w_x[layer, 0], lru_w_a[layer, 1], lru_w_x[layer, 1]],
                                axis=-1)
        gates = _lru_gates(p_lru, lru_conv_w[layer], lru_conv_b[layer], w_cat, lru_b_a[layer], lru_b_x[layer],
                           lru_lambda[layer], tok)
        hf, hb = _lru_scan(*(g.reshape(B, T, n_lru_blocks, LANES) for g in gates), tok)
        yd = _lru_out(hf.reshape(M, bw), hb.reshape(M, bw), p_lru)

        bn_merge = _pick(D, (1024, 512, 256))
        bm_merge = _pick(M, (512, 256))
        merged = None
        for i_br, y_br in enumerate((ya, yb, yc, yd)):
            merged = _merge_branch(h, y_br, wg_bf, b_branch_gate, wb_bf, merged, layer=layer,
                                   branch=i_br, bm=bm_merge, bn=bn_merge,
                                   out_dtype=BF16 if i_br == 3 else F32)
        y = _matmul(merged, w_out, w_prefix=(layer,), n_cols=D, bm=bm, bn=_pick_bn(D, 0), out_dtype=F32,
                    name="out_proj")
        xa, h2 = _resid_norm(xa, y, modv, modv, norm_ffn[layer], gate_idx=2, shift_idx=3, scale_idx=4,
                             out_dtype=BF16, tok=tok, name="resid_norm_ffn")
        y_ffn = _moe(h2, w_router, router_bias, w_exp_gate, w_exp_up, w_exp_down, layer=layer)
        mod_prev = modv

    _, out = _resid_norm(xa, y_ffn, mod_prev, None, final_norm, gate_idx=5, shift_idx=None, scale_idx=None,
                         out_dtype=F32, tok=tok, name="final_norm")
    return out.reshape(B, T, D)[:, :S]
```
